```python
import jax
import jax.numpy as jnp
from jax import lax
import numpy as np

D_MODEL = 1024
BATCH = 2
SEQ = 8192
DEPTH = 4
DEC_BATCH = 32
DEC_SEQ = 8
PAST_LEN = 8192
PAGE_SIZE = 128

N_A_LAYERS = DEPTH // 2
N_B_LAYERS = DEPTH - N_A_LAYERS
HEAD_DIM = 64
N_HEADS = D_MODEL // HEAD_DIM
D_FF = ((8 * D_MODEL // 3 + 127) // 128) * 128
D_DECAY_LORA = max(32, int(round(1.8 * D_MODEL ** 0.5 / 32)) * 32)
D_AAA_LORA = max(32, int(round(1.8 * D_MODEL ** 0.5 / 32)) * 32)
D_MV_LORA = max(32, int(round(1.3 * D_MODEL ** 0.5 / 32)) * 32)
D_GATE_LORA = max(32, int(round(0.6 * D_MODEL ** 0.8 / 32)) * 32)
N_MIX = 6
Q_BLOCK = 128
NORM_EPS = 1e-6
GN_EPS = 64e-5
FORGET_BIAS = 3.0
MACARON_W = 0.5
NEG_INF = -1e30

kernel_name = 'yoco_rwkv7_fox_macaron_step'


def rms_norm(x, g):
    xf = x.astype(jnp.float32)
    y = xf * lax.rsqrt(jnp.mean(xf * xf, axis=-1, keepdims=True) + NORM_EPS)
    return (y * g.astype(jnp.float32)).astype(x.dtype)


def modulate(x, g_pre, m):
    return rms_norm(x, g_pre) * (1 + m[:, None, 1]) + m[:, None, 0]


def gated_post(y, g_post, m):
    return m[:, None, 2] * rms_norm(y, g_post)


def swiglu(h, w_in, w_out):
    gate, up = jnp.split(h @ w_in, 2, axis=-1)
    return (jax.nn.silu(gate) * up) @ w_out


def wkv_step(S, inp):
    r, w, k, v, kk, a = inp
    s_kk = jnp.einsum('bhij,bhj->bhi', S, kk)
    S = S * w[:, :, None, :] - s_kk[..., None] * (a * kk)[:, :, None, :] + v[..., None] * k[:, :, None, :]
    return S, jnp.einsum('bhij,bhj->bhi', S, r)


def rwkv7_time_mix(h, h_last, S0, v_first, i, p):
    B, T, _ = h.shape
    heads = lambda t: t.reshape(B, T, N_HEADS, HEAD_DIM)
    h_prev = jnp.concatenate([h_last[:, None, :].astype(h.dtype), h[:, :-1]], axis=1)
    xx = h_prev - h
    mu = p['mu_a'][i]
    xr, xw, xk, xv, xa, xg = [h + xx * mu[j] for j in range(N_MIX)]
    w_rkv = p['w_rkv_a'][i]
    r = xr @ w_rkv[0]
    k = xk @ w_rkv[1]
    v = xv @ w_rkv[2]
    w_log = -jax.nn.softplus(-(p['w0_a'][i] + jnp.tanh(xw @ p['w1_a'][i]) @ p['w2_a'][i])) - 0.5
    a = jax.nn.sigmoid(p['a0_a'][i] + (xa @ p['a1_a'][i]) @ p['a2_a'][i])
    g = jax.nn.sigmoid(xg @ p['g1_a'][i]) @ p['g2_a'][i]
    if i == 0:
        v_first = v
    else:
        j = i - 1
        v = v + (v_first - v) * jax.nn.sigmoid(p['v0_a'][j] + (xv @ p['v1_a'][j]) @ p['v2_a'][j])
    kk = heads(k * p['kk_a'][i]).astype(jnp.float32)
    kk = kk * lax.rsqrt(jnp.maximum(jnp.sum(kk * kk, axis=-1, keepdims=True), 1e-24))
    k = k * (1 + (a - 1) * p['ka_a'][i])
    r_h = heads(r).astype(jnp.float32)
    k_h = heads(k).astype(jnp.float32)
    v_h = heads(v).astype(jnp.float32)
    a_h = heads(a).astype(jnp.float32)
    w_h = jnp.exp(-jnp.exp(heads(w_log).astype(jnp.float32)))
    xs = tuple(jnp.moveaxis(t, 1, 0) for t in (r_h, w_h, k_h, v_h, kk, a_h))
    S, o = lax.scan(wkv_step, S0.astype(jnp.float32), xs)
    o = jnp.moveaxis(o, 0, 1)
    mean = jnp.mean(o, axis=-1, keepdims=True)
    var = jnp.mean(jnp.square(o - mean), axis=-1, keepdims=True)
    o = ((o - mean) * lax.rsqrt(var + GN_EPS)).reshape(B, T, D_MODEL)
    o = o * p['lnx_w_a'][i].astype(jnp.float32) + p['lnx_b_a'][i].astype(jnp.float32)
    bonus = jnp.sum(r_h * k_h * p['rk_a'][i].astype(jnp.float32), axis=-1, keepdims=True) * v_h
    o = (o + bonus.reshape(B, T, D_MODEL)).astype(h.dtype)
    y = (o * g) @ p['w_o_a'][i]
    return y, h[:, -1], S, v_first


def fox_attend(q, cq, q_pos, k, v, ck, k_pos):
    s = jnp.einsum('bqhd,bkhd->bhqk', q, k).astype(jnp.float32)
    bias = jnp.swapaxes(cq, 1, 2)[..., None] - jnp.swapaxes(ck, 1, 2)[:, :, None, :]
    mask = (k_pos[None, :] <= q_pos[:, None])[None, None]
    s = jnp.where(mask, s + bias, NEG_INF)
    prob = jax.nn.softmax(s, axis=-1)
    return jnp.einsum('bhqk,bkhd->bqhd', prob.astype(v.dtype), v)


def fox_attention(q, cq, q_pos, k, v, ck, k_pos):
    B, T = q.shape[:2]
    if T <= Q_BLOCK:
        return fox_attend(q, cq, q_pos, k, v, ck, k_pos)
    nb = -(-T // Q_BLOCK)
    pad = nb * Q_BLOCK - T
    qp = jnp.pad(q, ((0, 0), (0, pad), (0, 0), (0, 0)))
    cqp = jnp.pad(cq, ((0, 0), (0, pad), (0, 0)))
    posp = jnp.pad(q_pos, (0, pad), constant_values=-1)
    qb = jnp.swapaxes(qp.reshape(B, nb, Q_BLOCK, N_HEADS, HEAD_DIM), 0, 1)
    cqb = jnp.swapaxes(cqp.reshape(B, nb, Q_BLOCK, N_HEADS), 0, 1)
    pb = posp.reshape(nb, Q_BLOCK)
    out = lax.map(lambda blk: fox_attend(blk[0], blk[1], blk[2], k, v, ck, k_pos), (qb, cqb, pb))
    return jnp.swapaxes(out, 0, 1).reshape(B, nb * Q_BLOCK, N_HEADS, HEAD_DIM)[:, :T]


def shared_kv(x, past, p):
    B, T, _ = x.shape
    s = rms_norm(x, p['g_kv'])
    k_new, v_new = jnp.split(s @ p['w_kv'], 2, axis=-1)
    k_new = k_new.reshape(B, T, N_HEADS, HEAD_DIM)
    v_new = v_new.reshape(B, T, N_HEADS, HEAD_DIM)
    logf_new = jax.nn.log_sigmoid((s @ p['w_f'] + p['b_f']).astype(jnp.float32))
    if past is None:
        k_all, v_all, logf_all = k_new, v_new, logf_new
    else:
        k_past, v_past, logf_past = past
        k_all = jnp.concatenate([k_past.astype(k_new.dtype), k_new], axis=1)
        v_all = jnp.concatenate([v_past.astype(v_new.dtype), v_new], axis=1)
        logf_all = jnp.concatenate([logf_past.astype(jnp.float32), logf_new], axis=1)
    n_past = k_all.shape[1] - T
    ck = jnp.cumsum(logf_all, axis=1)
    return dict(k_new=k_new, v_new=v_new, logf_new=logf_new.astype(x.dtype), k_all=k_all, v_all=v_all,
                ck=ck, cq=ck[:, n_past:], q_pos=n_past + jnp.arange(T), k_pos=jnp.arange(n_past + T))


def fox_layer(h, kv, j, p):
    B, T, _ = h.shape
    q = (h @ p['w_q_b'][j]).reshape(B, T, N_HEADS, HEAD_DIM) * (HEAD_DIM ** -0.5)
    o = fox_attention(q, kv['cq'], kv['q_pos'], kv['k_all'], kv['v_all'], kv['ck'], kv['k_pos'])
    return o.reshape(B, T, D_MODEL) @ p['w_o_b'][j]


def trunk(x, c, wkv0, shift0, past, p):
    B = x.shape[0]
    c_act = jax.nn.silu(c)
    v_first = None
    kv = None
    wkv_new, shift_new = [], []
    for l in range(DEPTH):
        if l == N_A_LAYERS:
            kv = shared_kv(x, past, p)
        mods = (c_act @ p['w_mod'][l] + p['b_mod'][l]).reshape(B, 3, 3, D_MODEL)
        gn = p['g_norm'][l]
        m = mods[:, 0]
        x = x + MACARON_W * gated_post(swiglu(modulate(x, gn[0, 0], m), p['w_ffn_in'][l, 0], p['w_ffn_out'][l, 0]), gn[0, 1], m)
        m = mods[:, 1]
        h = modulate(x, gn[1, 0], m)
        if l < N_A_LAYERS:
            y, h_last, S, v_first = rwkv7_time_mix(h, shift0[l], wkv0[l], v_first, l, p)
            wkv_new.append(S)
            shift_new.append(h_last)
        else:
            y = fox_layer(h, kv, l - N_A_LAYERS, p)
        x = x + gated_post(y, gn[1, 1], m)
        m = mods[:, 2]
        x = x + MACARON_W * gated_post(swiglu(modulate(x, gn[2, 0], m), p['w_ffn_in'][l, 1], p['w_ffn_out'][l, 1]), gn[2, 1], m)
    return x, jnp.stack(wkv_new), jnp.stack(shift_new), kv['k_new'], kv['v_new'], kv['logf_new']


def setup_inputs(seed: int = 0) -> dict:
    key = jax.random.key(seed)
    keys = iter(jax.random.split(key, 64))

    def nrm(shape, scale):
        return jax.random.normal(next(keys), shape, jnp.float32) * scale

    D = D_MODEL
    sd = D ** -0.5
    n_pages = PAST_LEN // PAGE_SIZE
    n_used = DEC_BATCH * n_pages
    n_pool = n_used + (n_used + 3) // 4
    page_table = jax.random.permutation(next(keys), n_pool)[:n_used].reshape(DEC_BATCH, n_pages).astype(jnp.int32)
    return {
        'x_prompt': nrm((BATCH, SEQ, D), 1.0),
        'x_sample': nrm((DEC_BATCH, DEC_SEQ, D), 1.0),
        'state_wkv': nrm((N_A_LAYERS, DEC_BATCH, N_HEADS, HEAD_DIM, HEAD_DIM), 0.3),
        'state_shift': nrm((N_A_LAYERS, DEC_BATCH, D), 1.0),
        'cache_k': nrm((n_pool, PAGE_SIZE, N_HEADS, HEAD_DIM), 1.0),
        'cache_v': nrm((n_pool, PAGE_SIZE, N_HEADS, HEAD_DIM), 1.0),
        'cache_logf': jax.nn.log_sigmoid(FORGET_BIAS + nrm((n_pool, PAGE_SIZE, N_HEADS), 0.5)),
        'page_table': page_table,
        'c_prompt': nrm((BATCH, D), 1.0),
        'c_sample': nrm((DEC_BATCH, D), 1.0),
        'w_mod': nrm((DEPTH, D, 9 * D), 0.5 * sd),
        'b_mod': nrm((DEPTH, 9 * D), 0.01),
        'g_norm': 1.0 + nrm((DEPTH, 3, 2, D), 0.02),
        'w_ffn_in': nrm((DEPTH, 2, D, 2 * D_FF), sd),
        'w_ffn_out': nrm((DEPTH, 2, D_FF, D), D_FF ** -0.5),
        'mu_a': jax.random.uniform(next(keys), (N_A_LAYERS, N_MIX, D), jnp.float32),
        'w_rkv_a': nrm((N_A_LAYERS, 3, D, D), sd),
        'w_o_a': nrm((N_A_LAYERS, D, D), sd),
        'w0_a': nrm((N_A_LAYERS, D), 0.5),
        'w1_a': nrm((N_A_LAYERS, D, D_DECAY_LORA), 0.1 * sd),
        'w2_a': nrm((N_A_LAYERS, D_DECAY_LORA, D), 0.1 * D_DECAY_LORA ** -0.5),
        'a0_a': nrm((N_A_LAYERS, D), 0.1),
        'a1_a': nrm((N_A_LAYERS, D, D_AAA_LORA), 0.5 * sd),
        'a2_a': nrm((N_A_LAYERS, D_AAA_LORA, D), 0.5 * D_AAA_LORA ** -0.5),
        'v0_a': 1.0 + nrm((N_A_LAYERS - 1, D), 0.1),
        'v1_a': nrm((N_A_LAYERS - 1, D, D_MV_LORA), 0.5 * sd),
        'v2_a': nrm((N_A_LAYERS - 1, D_MV_LORA, D), 0.5 * D_MV_LORA ** -0.5),
        'g1_a': nrm((N_A_LAYERS, D, D_GATE_LORA), sd),
        'g2_a': nrm((N_A_LAYERS, D_GATE_LORA, D), D_GATE_LORA ** -0.5),
        'kk_a': 0.85 + nrm((N_A_LAYERS, D), 0.05),
        'ka_a': 1.0 + nrm((N_A_LAYERS, D), 0.05),
        'rk_a': nrm((N_A_LAYERS, N_HEADS, HEAD_DIM), 0.1),
        'lnx_w_a': 1.0 + nrm((N_A_LAYERS, D), 0.02),
        'lnx_b_a': nrm((N_A_LAYERS, D), 0.01),
        'g_kv': 1.0 + nrm((D,), 0.02),
        'w_kv': nrm((D, 2 * D), sd),
        'w_f': nrm((D, N_HEADS), 0.1 * sd),
        'b_f': FORGET_BIAS + nrm((N_HEADS,), 0.1),
        'w_q_b': nrm((N_B_LAYERS, D, D), sd),
        'w_o_b': nrm((N_B_LAYERS, D, D), sd),
    }


def reference(x_prompt, x_sample, state_wkv, state_shift, cache_k, cache_v, cache_logf, page_table, c_prompt, c_sample,
              w_mod, b_mod, g_norm, w_ffn_in, w_ffn_out, mu_a, w_rkv_a, w_o_a, w0_a, w1_a, w2_a, a0_a, a1_a, a2_a,
              v0_a, v1_a, v2_a, g1_a, g2_a, kk_a, ka_a, rk_a, lnx_w_a, lnx_b_a, g_kv, w_kv, w_f, b_f, w_q_b, w_o_b):
    p = dict(w_mod=w_mod, b_mod=b_mod, g_norm=g_norm, w_ffn_in=w_ffn_in, w_ffn_out=w_ffn_out, mu_a=mu_a,
             w_rkv_a=w_rkv_a, w_o_a=w_o_a, w0_a=w0_a, w1_a=w1_a, w2_a=w2_a, a0_a=a0_a, a1_a=a1_a, a2_a=a2_a,
             v0_a=v0_a, v1_a=v1_a, v2_a=v2_a, g1_a=g1_a, g2_a=g2_a, kk_a=kk_a, ka_a=ka_a, rk_a=rk_a,
             lnx_w_a=lnx_w_a, lnx_b_a=lnx_b_a, g_kv=g_kv, w_kv=w_kv, w_f=w_f, b_f=b_f, w_q_b=w_q_b, w_o_b=w_o_b)
    B = x_prompt.shape[0]
    wkv0 = jnp.zeros((N_A_LAYERS, B, N_HEADS, HEAD_DIM, HEAD_DIM), jnp.float32)
    shift0 = jnp.zeros((N_A_LAYERS, B, D_MODEL), x_prompt.dtype)
    y_prompt, wkv_p, shift_p, k_p, v_p, logf_p = trunk(x_prompt, c_prompt, wkv0, shift0, None, p)
    DB, n_pages = page_table.shape
    past_len = n_pages * PAGE_SIZE
    past = (cache_k[page_table].reshape(DB, past_len, N_HEADS, HEAD_DIM),
            cache_v[page_table].reshape(DB, past_len, N_HEADS, HEAD_DIM),
            cache_logf[page_table].reshape(DB, past_len, N_HEADS))
    y_sample, wkv_s, shift_s, k_s, v_s, logf_s = trunk(x_sample, c_sample, state_wkv, state_shift, past, p)
    return (y_prompt, y_sample, wkv_p.astype(x_prompt.dtype), shift_p, k_p, v_p, logf_p,
            wkv_s.astype(state_wkv.dtype), shift_s, k_s, v_s, logf_s)
```

```python
import functools

import jax
import jax.numpy as jnp
from jax import lax
from jax.experimental import pallas as pl
from jax.experimental.pallas import tpu as pltpu

F32 = jnp.float32
BF16 = jnp.bfloat16

NORM_EPS = 1e-6
GN_EPS = 64e-5
MACARON_W = 0.5
NEG_INF = -1e30
HEAD_DIM = 64
LANES = 128
PAIR = LANES // HEAD_DIM
VMEM_LIMIT = 56 * 1024 * 1024

_NT = (((1,), (1,)), ((), ()))
_TN = (((0,), (0,)), ((), ()))


def _cparams(*sem):
    return pltpu.CompilerParams(dimension_semantics=sem, vmem_limit_bytes=VMEM_LIMIT)


def _dot(a, b):
    return jnp.dot(a.astype(BF16), b.astype(BF16), preferred_element_type=F32)


def _dot_nt(a, b):
    return lax.dot_general(a.astype(BF16), b.astype(BF16), _NT, preferred_element_type=F32)


def _dot_tn(a, b):
    return lax.dot_general(a.astype(BF16), b.astype(BF16), _TN, preferred_element_type=F32)


def _split3(x):
    hi = x.astype(BF16)
    r1 = x - hi.astype(F32)
    mid = r1.astype(BF16)
    lo = (r1 - mid.astype(F32)).astype(BF16)
    return hi, mid, lo


def _dot3(sel, x):
    hi, mid, lo = _split3(x)
    out = jnp.dot(sel, hi, preferred_element_type=F32)
    out = out + jnp.dot(sel, mid, preferred_element_type=F32)
    return out + jnp.dot(sel, lo, preferred_element_type=F32)


def _dot3_rhs(x, sel):
    hi, mid, lo = _split3(x)
    out = jnp.dot(hi, sel, preferred_element_type=F32)
    out = out + jnp.dot(mid, sel, preferred_element_type=F32)
    return out + jnp.dot(lo, sel, preferred_element_type=F32)


def _rms(x, g):
    return x * lax.rsqrt(jnp.mean(x * x, axis=-1, keepdims=True) + NORM_EPS) * g


def _sigmoid(x):
    return 1.0 / (1.0 + jnp.exp(-x))


def _silu(x):
    return x * _sigmoid(x)


def _head_ones(n):
    r = lax.broadcasted_iota(jnp.int32, (n, n), 0) // HEAD_DIM
    c = lax.broadcasted_iota(jnp.int32, (n, n), 1) // HEAD_DIM
    return jnp.where(r == c, 1.0, 0.0).astype(BF16)


def _head_sum(x, ones):
    n = ones.shape[0]
    hi = x.astype(BF16)
    lo = (x - hi.astype(F32)).astype(BF16)
    cols = []
    for i in range(0, x.shape[1], n):
        s = jnp.dot(hi[:, i:i + n], ones, preferred_element_type=F32)
        cols.append(s + jnp.dot(lo[:, i:i + n], ones, preferred_element_type=F32))
    return cols[0] if len(cols) == 1 else jnp.concatenate(cols, axis=1)


def _mods_kernel(c_ref, w_ref, b_ref, o_ref):
    o_ref[...] = _dot(_silu(c_ref[...]), w_ref[...]) + b_ref[...]


def _mods(c_all, w_mod, b_mod):
    depth, d, n9 = w_mod.shape
    nb = c_all.shape[0]
    tn = d
    return pl.pallas_call(
        _mods_kernel,
        grid=(depth, n9 // tn),
        in_specs=[
            pl.BlockSpec((nb, d), lambda l, j: (0, 0)),
            pl.BlockSpec((None, d, tn), lambda l, j: (l, 0, j)),
            pl.BlockSpec((None, 1, tn), lambda l, j: (l, 0, j)),
        ],
        out_specs=pl.BlockSpec((None, nb, tn), lambda l, j: (l, 0, j)),
        out_shape=jax.ShapeDtypeStruct((depth, nb, n9), F32),
        compiler_params=_cparams("arbitrary", "arbitrary"),
        name="mods",
    )(c_all, w_mod, b_mod.reshape(depth, 1, n9))


def _mod_spec(m4, tm, tiles_per_seq):
    _, _, rows, d = m4.shape
    if rows == 1:
        return pl.BlockSpec((None, 3, 1, d), lambda i, *_: (i // tiles_per_seq, 0, 0, 0))
    return pl.BlockSpec((None, 3, tm, d), lambda i, *_: (0, 0, i, 0))


def _ffn_kernel(x_ref, m_ref, g_ref, wg_ref, wu_ref, wo_ref, o_ref, h_scr, acc_scr):
    j = pl.program_id(1)

    @pl.when(j == 0)
    def _():
        h = _rms(x_ref[...], g_ref[0:1]) * (1.0 + m_ref[1]) + m_ref[0]
        h_scr[...] = h.astype(BF16)
        acc_scr[...] = jnp.zeros_like(acc_scr)

    h = h_scr[...]
    gate = jnp.dot(h, wg_ref[...], preferred_element_type=F32)
    up = jnp.dot(h, wu_ref[...], preferred_element_type=F32)
    acc_scr[...] += _dot(_silu(gate) * up, wo_ref[...])

    @pl.when(j == pl.num_programs(1) - 1)
    def _():
        o_ref[...] = x_ref[...] + MACARON_W * m_ref[2] * _rms(acc_scr[...], g_ref[1:2])


def _ffn(x, m4, gn, w_in, w_out, tm, tiles_per_seq):
    n, d = x.shape
    f = w_out.shape[0]
    tf = 256 if f % 256 == 0 else LANES
    nf = f // tf
    return pl.pallas_call(
        _ffn_kernel,
        grid=(n // tm, nf),
        in_specs=[
            pl.BlockSpec((tm, d), lambda i, j: (i, 0)),
            _mod_spec(m4, tm, tiles_per_seq),
            pl.BlockSpec((2, d), lambda i, j: (0, 0)),
            pl.BlockSpec((d, tf), lambda i, j: (0, j)),
            pl.BlockSpec((d, tf), lambda i, j: (0, j + nf)),
            pl.BlockSpec((tf, d), lambda i, j: (j, 0)),
        ],
        out_specs=pl.BlockSpec((tm, d), lambda i, j: (i, 0)),
        out_shape=jax.ShapeDtypeStruct((n, d), F32),
        scratch_shapes=[pltpu.VMEM((tm, d), BF16), pltpu.VMEM((tm, d), F32)],
        compiler_params=_cparams("parallel", "arbitrary"),
        name="ffn",
    )(x, m4, gn, w_in, w_in, w_out)


def _rwkv_proj_kernel(has_vfirst, seq_len, x_ref, m_ref, g_ref, hl_ref, mu_ref, wrkv_ref, w0_ref, w1_ref,
                      w2_ref, a0_ref, a1_ref, a2_ref, g1_ref, g2_ref, kka_ref, kaa_ref, *rest):
    if has_vfirst:
        v0_ref, v1_ref, v2_ref, vf_ref = rest[:4]
        rest = rest[4:]
    r_ref, lw_ref, k_ref, v_ref, kk_ref, a_ref, gt_ref, h_ref, carry_scr = rest
    tm, d = x_ref.shape
    i = pl.program_id(0)

    h = _rms(x_ref[...], g_ref[0:1]) * (1.0 + m_ref[1]) + m_ref[0]
    h_ref[...] = h
    row = lax.broadcasted_iota(jnp.int32, (tm, d), 0)
    rolled = pltpu.roll(h, 1, 0)
    if hl_ref.shape[0] == 1:
        tiles_per_seq = seq_len // tm
        first = jnp.where(i % tiles_per_seq == 0, hl_ref[...], carry_scr[...])
        h_prev = jnp.where(row == 0, first, rolled)
        carry_scr[...] = h[tm - 1:tm, :]
    else:
        h_prev = jnp.where(row % seq_len == 0, hl_ref[...], rolled)
    xx = h_prev - h
    xr, xw, xk, xv, xa, xg = [h + xx * mu_ref[j:j + 1] for j in range(6)]

    r = _dot(xr, wrkv_ref[0])
    k = _dot(xk, wrkv_ref[1])
    v = _dot(xv, wrkv_ref[2])
    z = w0_ref[...] + _dot(jnp.tanh(_dot(xw, w1_ref[...])), w2_ref[...])
    lw = -_sigmoid(z) * jnp.exp(F32(-0.5))
    a = _sigmoid(a0_ref[...] + _dot(_dot(xa, a1_ref[...]), a2_ref[...]))
    gt = _dot(_sigmoid(_dot(xg, g1_ref[...])), g2_ref[...])
    if has_vfirst:
        mix = _sigmoid(v0_ref[...] + _dot(_dot(xv, v1_ref[...]), v2_ref[...]))
        v = v + (vf_ref[...] - v) * mix
    kk = k * kka_ref[...]
    ones = _head_ones(min(d, 2 * LANES))
    kk = kk * lax.rsqrt(jnp.maximum(_head_sum(kk * kk, ones), 1e-24))
    k = k * (1.0 + (a - 1.0) * kaa_ref[...])

    r_ref[...] = r
    lw_ref[...] = lw
    k_ref[...] = k
    v_ref[...] = v
    kk_ref[...] = kk
    a_ref[...] = a
    gt_ref[...] = gt


def _rwkv_proj(x, m4, gn, hl, seq_len, tm, p, v_first):
    n, d = x.shape
    tiles_per_seq = max(seq_len // tm, 1)
    has_vfirst = v_first is not None
    row = lambda a: a.reshape(1, d)
    const = lambda a: pl.BlockSpec(a.shape, lambda i: (0,) * a.ndim)
    tok = pl.BlockSpec((tm, d), lambda i: (i, 0))
    if hl.ndim == 3:
        hl_spec = pl.BlockSpec((None, 1, d), lambda i: (i // tiles_per_seq, 0, 0))
    else:
        hl_spec = tok
    args = [x, m4, gn, hl, p['mu'], p['w_rkv'], row(p['w0']), p['w1'], p['w2'], row(p['a0']), p['a1'], p['a2'],
            p['g1'], p['g2'], row(p['kk']), row(p['ka'])]
    specs = [tok, _mod_spec(m4, tm, tiles_per_seq), const(gn), hl_spec] + [const(a) for a in args[4:]]
    if has_vfirst:
        extra = [row(p['v0']), p['v1'], p['v2']]
        args += extra + [v_first]
        specs += [const(a) for a in extra] + [tok]
    return pl.pallas_call(
        functools.partial(_rwkv_proj_kernel, has_vfirst, seq_len),
        grid=(n // tm,),
        in_specs=specs,
        out_specs=[tok] * 8,
        out_shape=[jax.ShapeDtypeStruct((n, d), F32)] * 8,
        scratch_shapes=[pltpu.VMEM((1, d), F32)],
        compiler_params=_cparams("arbitrary"),
        name="rwkv_proj",
    )(*args)


def _wkv_kernel(chunk, r_ref, lw_ref, k_ref, v_ref, kk_ref, a_ref, s0_ref, lnw_ref, lnb_ref, rk_ref,
                o_ref, sout_ref, s_scr):
    c = chunk
    n2 = PAIR * c
    tb, width = r_ref.shape
    n_pairs = width // LANES
    t_idx = pl.program_id(2)

    @pl.when(t_idx == 0)
    def _():
        s_scr[...] = s0_ref[...]

    lane = lax.broadcasted_iota(jnp.int32, (c, LANES), 1)
    head_lane = [lane // HEAD_DIM == h for h in range(PAIR)]
    ri = lax.broadcasted_iota(jnp.int32, (c, c), 0)
    ci = lax.broadcasted_iota(jnp.int32, (c, c), 1)
    tri = jnp.where(ci <= ri, 1.0, 0.0).astype(BF16)
    r2 = lax.broadcasted_iota(jnp.int32, (n2, n2), 0)
    c2 = lax.broadcasted_iota(jnp.int32, (n2, n2), 1)
    strict = c2 < r2
    incl = c2 <= r2
    eye = jnp.where(c2 == r2, 1.0, 0.0)
    level_masks = []
    b = 1
    while b < c:
        level_masks.append((r2 // (2 * b) == c2 // (2 * b)) & ((r2 // b) % 2 == 1) & ((c2 // b) % 2 == 0))
        b *= 2
    ones = _head_ones(LANES)

    def stack(x):
        return jnp.concatenate([jnp.where(m, x, 0.0) for m in head_lane], axis=0)

    def one_chunk(ic, carry):
        rows = pl.ds(pl.multiple_of(ic * c, c), c)
        for p in range(n_pairs):
            ls = slice(p * LANES, (p + 1) * LANES)
            rc, lwc, kc, vc, kkc, ac = [ref[rows, ls] for ref in (r_ref, lw_ref, k_ref, v_ref, kk_ref, a_ref)]
            cum = _dot3(tri, lwc)
            g = jnp.exp(cum)
            gi = jnp.exp(-cum)
            g_end = g[c - 1:c, :]
            akk = ac * kkc
            a2 = stack(-kkc * jnp.exp(cum - lwc))
            b2 = stack(akk * gi)
            k2 = stack(kc * gi)
            q2 = stack(rc * g)
            v2 = stack(vc)
            bt2 = stack(akk * gi * g_end)
            kt2 = stack(kc * gi * g_end)

            l_ab = jnp.where(strict, _dot_nt(a2, b2), 0.0)
            l_ak = jnp.where(strict, _dot_nt(a2, k2), 0.0)
            l_rb = jnp.where(incl, _dot_nt(q2, b2), 0.0)
            l_rk = jnp.where(incl, _dot_nt(q2, k2), 0.0)
            tinv = eye
            for mask in level_masks:
                tinv = tinv + _dot(tinv, _dot(jnp.where(mask, l_ab, 0.0), tinv))

            s = s_scr[p]
            u2 = _dot(tinv, _dot_nt(a2, s) + _dot(l_ak, v2))
            o2 = _dot_nt(q2, s) + _dot(l_rb, u2) + _dot(l_rk, v2)
            s_scr[p] = s * g_end + _dot_tn(u2, bt2) + _dot_tn(v2, kt2)

            o = o2[0:c]
            for h in range(1, PAIR):
                o = o + o2[h * c:(h + 1) * c]
            mean = _head_sum(o, ones) * (1.0 / HEAD_DIM)
            cen = o - mean
            var = _head_sum(cen * cen, ones) * (1.0 / HEAD_DIM)
            o = cen * lax.rsqrt(var + GN_EPS) * lnw_ref[:, ls] + lnb_ref[:, ls]
            o_ref[rows, ls] = o + _head_sum(rc * kc * rk_ref[:, ls], ones) * vc
        return carry

    lax.fori_loop(0, tb // c, one_chunk, 0)

    @pl.when(t_idx == pl.num_programs(2) - 1)
    def _():
        sout_ref[...] = s_scr[...]


def _wkv(r, lw, k, v, kk, a, s0, lnw, lnb, rk, seq_len, chunk, tb, n_pairs):
    n, d = r.shape
    n_seq = n // seq_len
    width = n_pairs * LANES
    nt = seq_len // tb
    tok = pl.BlockSpec((tb, width), lambda b, g, t: (b * nt + t, g))
    vec = pl.BlockSpec((1, width), lambda b, g, t: (0, g))
    st = pl.BlockSpec((None, n_pairs, LANES, LANES), lambda b, g, t: (b, g, 0, 0))
    return pl.pallas_call(
        functools.partial(_wkv_kernel, chunk),
        grid=(n_seq, d // width, nt),
        in_specs=[tok] * 6 + [st, vec, vec, vec],
        out_specs=[tok, st],
        out_shape=[jax.ShapeDtypeStruct((n, d), F32), jax.ShapeDtypeStruct(s0.shape, F32)],
        scratch_shapes=[pltpu.VMEM((n_pairs, LANES, LANES), F32)],
        compiler_params=_cparams("parallel", "parallel", "arbitrary"),
        name="wkv",
    )(r, lw, k, v, kk, a, s0, lnw.reshape(1, d), lnb.reshape(1, d), rk.reshape(1, d))


def _pair_state(s):
    n_seq, h, n, _ = s.shape
    s = s.reshape(n_seq, h // PAIR, PAIR, n, n)
    eye = jnp.eye(PAIR, dtype=s.dtype)
    out = s[:, :, :, :, None, :] * eye[None, None, :, None, :, None]
    return out.reshape(n_seq, h // PAIR, PAIR * n, PAIR * n)


def _unpair_state(sp):
    n_seq, hp, n2, _ = sp.shape
    n = n2 // PAIR
    s = sp.reshape(n_seq, hp, PAIR, n, PAIR, n)
    return jnp.stack([s[:, :, a, :, a, :] for a in range(PAIR)], axis=2).reshape(n_seq, hp * PAIR, n, n)


def _out_kernel(has_gate, res_w, *refs):
    if has_gate:
        a_ref, gt_ref, x_ref, m_ref, g_ref, w_ref, o_ref = refs
        a = a_ref[...] * gt_ref[...]
    else:
        a_ref, x_ref, m_ref, g_ref, w_ref, o_ref = refs
        a = a_ref[...]
    y = _dot(a, w_ref[...])
    o_ref[...] = x_ref[...] + res_w * m_ref[2] * _rms(y, g_ref[1:2])


def _out_proj(a, gate, x, m4, gn, w, tm, tiles_per_seq, res_w=1.0):
    n, d = x.shape
    tok = pl.BlockSpec((tm, d), lambda i: (i, 0))
    args = [a] + ([gate] if gate is not None else []) + [x, m4, gn, w]
    specs = [tok] * (len(args) - 3) + [_mod_spec(m4, tm, tiles_per_seq), pl.BlockSpec((2, d), lambda i: (0, 0)),
                                       pl.BlockSpec(w.shape, lambda i: (0, 0))]
    return pl.pallas_call(
        functools.partial(_out_kernel, gate is not None, res_w),
        grid=(n // tm,),
        in_specs=specs,
        out_specs=tok,
        out_shape=jax.ShapeDtypeStruct((n, d), F32),
        compiler_params=_cparams("parallel"),
        name="out_proj",
    )(*args)


def _kv_kernel(x_ref, g_ref, wkv_ref, wf_ref, bf_ref, k_ref, v_ref, kb_ref, vb_ref, lf_ref):
    d = x_ref.shape[1]
    s = _rms(x_ref[...], g_ref[...])
    kv = _dot(s, wkv_ref[...])
    k = kv[:, :d]
    v = kv[:, d:]
    k_ref[...] = k
    v_ref[...] = v
    kb_ref[...] = k.astype(BF16)
    vb_ref[...] = v.astype(BF16)
    z = _dot(s, wf_ref[...]) + bf_ref[...]
    lf_ref[...] = jnp.minimum(z, 0.0) - jnp.log(1.0 + jnp.exp(-jnp.abs(z)))


def _shared_kv(x, g_kv, w_kv, w_f, b_f, tm):
    n, d = x.shape
    h = w_f.shape[1]
    tok = pl.BlockSpec((tm, d), lambda i: (i, 0))
    return pl.pallas_call(
        _kv_kernel,
        grid=(n // tm,),
        in_specs=[tok, pl.BlockSpec((1, d), lambda i: (0, 0)), pl.BlockSpec(w_kv.shape, lambda i: (0, 0)),
                  pl.BlockSpec(w_f.shape, lambda i: (0, 0)), pl.BlockSpec((1, h), lambda i: (0, 0))],
        out_specs=[tok, tok, tok, tok, pl.BlockSpec((tm, h), lambda i: (i, 0))],
        out_shape=[jax.ShapeDtypeStruct((n, d), F32)] * 2 + [jax.ShapeDtypeStruct((n, d), BF16)] * 2
        + [jax.ShapeDtypeStruct((n, h), F32)],
        compiler_params=_cparams("parallel"),
        name="shared_kv",
    )(x, g_kv.reshape(1, d), w_kv, w_f, b_f.reshape(1, h))


def _cumsum_kernel(has_new, pt_ref, lf_ref, *rest):
    if has_new:
        new_ref, ck_ref, cknew_ref, carry_scr = rest
    else:
        ck_ref, carry_scr = rest
    del pt_ref
    j = pl.program_id(1)
    page = lf_ref.shape[0]

    @pl.when(j == 0)
    def _():
        carry_scr[...] = jnp.zeros_like(carry_scr)

    ri = lax.broadcasted_iota(jnp.int32, (page, page), 0)
    ci = lax.broadcasted_iota(jnp.int32, (page, page), 1)
    tri = jnp.where(ci <= ri, 1.0, 0.0).astype(BF16)
    ck = carry_scr[...] + _dot3(tri, lf_ref[...])
    ck_ref[...] = ck
    carry_scr[...] = ck[page - 1:page, :]

    if has_new:
        @pl.when(j == pl.num_programs(1) - 1)
        def _():
            t = new_ref.shape[0]
            cknew_ref[...] = ck[page - 1:page, :] + _dot3(tri[:t, :t], new_ref[...])


def _cumsum_pages(page_table, lf_pages, lf_new):
    n_seq, n_pages = page_table.shape
    _, page, h = lf_pages.shape
    has_new = lf_new is not None
    in_specs = [pl.BlockSpec((None, page, h), lambda b, j, pt: (pt[b, j], 0, 0))]
    out_specs = [pl.BlockSpec((None, page, h), lambda b, j, pt: (b, j, 0))]
    out_shape = [jax.ShapeDtypeStruct((n_seq, n_pages * page, h), F32)]
    args = [lf_pages]
    if has_new:
        t = lf_new.shape[1]
        in_specs.append(pl.BlockSpec((None, t, h), lambda b, j, pt: (b, 0, 0)))
        out_specs.append(pl.BlockSpec((None, t, h), lambda b, j, pt: (b, 0, 0)))
        out_shape.append(jax.ShapeDtypeStruct((n_seq, t, h), F32))
        args.append(lf_new)
    return pl.pallas_call(
        functools.partial(_cumsum_kernel, has_new),
        grid_spec=pltpu.PrefetchScalarGridSpec(
            num_scalar_prefetch=1, grid=(n_seq, n_pages), in_specs=in_specs, out_specs=out_specs,
            scratch_shapes=[pltpu.VMEM((1, h), F32)]),
        out_shape=out_shape,
        compiler_params=_cparams("parallel", "arbitrary"),
        name="cumsum",
    )(page_table, *args)


def _q_kernel(x_ref, m_ref, g_ref, w_ref, q_ref):
    h = _rms(x_ref[...], g_ref[0:1]) * (1.0 + m_ref[1]) + m_ref[0]
    q_ref[...] = (_dot(h, w_ref[...]) * (HEAD_DIM ** -0.5)).astype(BF16)


def _q_proj(x, m4, gn, w, tm, tiles_per_seq):
    n, d = x.shape
    tok = pl.BlockSpec((tm, d), lambda i: (i, 0))
    return pl.pallas_call(
        _q_kernel,
        grid=(n // tm,),
        in_specs=[tok, _mod_spec(m4, tm, tiles_per_seq), pl.BlockSpec((2, d), lambda i: (0, 0)),
                  pl.BlockSpec(w.shape, lambda i: (0, 0))],
        out_specs=tok,
        out_shape=jax.ShapeDtypeStruct((n, d), BF16),
        compiler_params=_cparams("parallel"),
        name="q_proj",
    )(x, m4, gn, w)


def _flash_kernel(qi_ref, kj_ref, q_ref, k_ref, v_ref, cq_ref, ck_ref, o_ref, m_scr, l_scr, acc_scr):
    n = pl.program_id(2)
    qi = qi_ref[n]
    kj = kj_ref[n]
    tq = q_ref.shape[0]
    tk = k_ref.shape[0]

    @pl.when(kj == 0)
    def _():
        m_scr[...] = jnp.full_like(m_scr, NEG_INF)
        l_scr[...] = jnp.zeros_like(l_scr)
        acc_scr[...] = jnp.zeros_like(acc_scr)

    lane = lax.broadcasted_iota(jnp.int32, (tq, LANES), 1)
    q = q_ref[...]
    k = k_ref[...]
    v = v_ref[...]

    def update(masked):
        if masked:
            valid = (lax.broadcasted_iota(jnp.int32, (tq, tk), 1)
                     <= lax.broadcasted_iota(jnp.int32, (tq, tk), 0))
        for h in range(PAIR):
            qh = jnp.where(lane // HEAD_DIM == h, q, jnp.zeros_like(q))
            s = lax.dot_general(qh, k, _NT, preferred_element_type=F32)
            s = s + (cq_ref[:, h:h + 1] - ck_ref[h:h + 1, :])
            if masked:
                s = jnp.where(valid, s, NEG_INF)
            m_old = m_scr[h]
            m_new = jnp.maximum(m_old, jnp.max(s, axis=1, keepdims=True))
            alpha = jnp.exp(m_old - m_new)
            p = jnp.exp(s - m_new)
            l_scr[h] = alpha * l_scr[h] + jnp.sum(p, axis=1, keepdims=True)
            acc_scr[h] = alpha * acc_scr[h] + jnp.dot(p.astype(BF16), v, preferred_element_type=F32)
            m_scr[h] = m_new

    @pl.when(kj < qi)
    def _():
        update(False)

    @pl.when(kj == qi)
    def _():
        update(True)
        out = acc_scr[0] / l_scr[0]
        for h in range(1, PAIR):
            out = jnp.where(lane // HEAD_DIM == h, acc_scr[h] / l_scr[h], out)
        o_ref[...] = out


def _flash_prompt(q, kb, vb, ck, n_seq, tq):
    n, d = q.shape
    t = n // n_seq
    h = ck.shape[2]
    hp = h // PAIR
    nq = t // tq
    ck_pairs = ck.reshape(n_seq, t, hp, PAIR)
    cq_cols = ck_pairs.transpose(0, 2, 1, 3)
    ck_rows = ck_pairs.transpose(0, 2, 3, 1)
    qi = jnp.asarray([i for i in range(nq) for _ in range(i + 1)], jnp.int32)
    kj = jnp.asarray([j for i in range(nq) for j in range(i + 1)], jnp.int32)
    return pl.pallas_call(
        _flash_kernel,
        grid_spec=pltpu.PrefetchScalarGridSpec(
            num_scalar_prefetch=2,
            grid=(n_seq, hp, int(qi.shape[0])),
            in_specs=[
                pl.BlockSpec((tq, LANES), lambda b, p, s, qi, kj: (b * nq + qi[s], p)),
                pl.BlockSpec((tq, LANES), lambda b, p, s, qi, kj: (b * nq + kj[s], p)),
                pl.BlockSpec((tq, LANES), lambda b, p, s, qi, kj: (b * nq + kj[s], p)),
                pl.BlockSpec((None, None, tq, PAIR), lambda b, p, s, qi, kj: (b, p, qi[s], 0)),
                pl.BlockSpec((None, None, PAIR, tq), lambda b, p, s, qi, kj: (b, p, 0, kj[s])),
            ],
            out_specs=pl.BlockSpec((tq, LANES), lambda b, p, s, qi, kj: (b * nq + qi[s], p)),
            scratch_shapes=[pltpu.VMEM((PAIR, tq, 1), F32), pltpu.VMEM((PAIR, tq, 1), F32),
                            pltpu.VMEM((PAIR, tq, LANES), F32)]),
        out_shape=jax.ShapeDtypeStruct((n, d), F32),
        compiler_params=_cparams("parallel", "parallel", "arbitrary"),
        name="fox_prompt",
    )(qi, kj, q, kb, vb, cq_cols, ck_rows)


def _paged_kernel(n_grp, pt_ref, q_ref, cq_ref, ckp_ref, ckn_ref, kn_ref, vn_ref, *rest):
    k_refs = rest[:n_grp]
    v_refs = rest[n_grp:2 * n_grp]
    o_ref, qbd_scr, m_scr, l_scr, acc_scr = rest[2 * n_grp:]
    del pt_ref
    g = pl.program_id(1)
    t, d = q_ref.shape
    n_heads = d // HEAD_DIM
    rows = n_heads * t
    page = k_refs[0].shape[0]

    row_head = lax.broadcasted_iota(jnp.int32, (rows, d), 0) // t
    lane_head = lax.broadcasted_iota(jnp.int32, (rows, d), 1) // HEAD_DIM
    expand = jnp.where(lax.broadcasted_iota(jnp.int32, (rows, n_heads), 0) // t
                       == lax.broadcasted_iota(jnp.int32, (rows, n_heads), 1), 1.0, 0.0).astype(BF16)

    @pl.when(g == 0)
    def _():
        q = q_ref[...].astype(F32)
        q_rows = jnp.concatenate([q] * n_heads, axis=0)
        qbd_scr[...] = jnp.where(row_head == lane_head, q_rows, 0.0).astype(BF16)
        m_scr[...] = jnp.full_like(m_scr, NEG_INF)
        l_scr[...] = jnp.zeros_like(l_scr)
        acc_scr[...] = jnp.zeros_like(acc_scr)

    qbd = qbd_scr[...]
    cq = cq_ref[...]

    def attend(s, vs):
        m_old = m_scr[...]
        m_new = jnp.maximum(m_old, jnp.max(s, axis=1, keepdims=True))
        alpha = jnp.exp(m_old - m_new)
        p = jnp.exp(s - m_new)
        l_scr[...] = alpha * l_scr[...] + jnp.sum(p, axis=1, keepdims=True)
        pv = None
        for i, vpage in enumerate(vs):
            part = _dot(p[:, i * page:(i + 1) * page], vpage)
            pv = part if pv is None else pv + part
        acc_scr[...] = alpha * acc_scr[...] + pv
        m_scr[...] = m_new

    s = jnp.concatenate([_dot_nt(qbd, kr[...]) for kr in k_refs], axis=1)
    s = s + (cq - _dot3(expand, ckp_ref[...]))
    attend(s, [vr[...] for vr in v_refs])

    @pl.when(g == pl.num_programs(1) - 1)
    def _():
        pad = jnp.zeros((page - t, d), F32)
        k_new = jnp.concatenate([kn_ref[...], pad], axis=0)
        v_new = jnp.concatenate([vn_ref[...], pad], axis=0)
        s_new = _dot_nt(qbd, k_new) + (cq - _dot3(expand, ckn_ref[...]))
        key = lax.broadcasted_iota(jnp.int32, (rows, page), 1)
        query = lax.broadcasted_iota(jnp.int32, (rows, page), 0) % t
        attend(jnp.where((key <= query) & (key < t), s_new, NEG_INF), [v_new])
        out = acc_scr[...] / l_scr[...]
        out = jnp.where(row_head == lane_head, out, 0.0).reshape(n_heads, t, d)
        o_ref[...] = jnp.sum(out, axis=0)


def _paged_attention(q, cache_k, cache_v, page_table, ck_past, ck_new, k_new, v_new, n_grp):
    n_seq, t, d = q.shape
    n_pages = page_table.shape[1]
    page = cache_k.shape[1]
    h = d // HEAD_DIM
    rows = h * t
    assert n_pages % n_grp == 0
    cq_col = ck_new.transpose(0, 2, 1).reshape(n_seq, rows, 1)
    ckp_rows = ck_past.transpose(0, 2, 1)
    ckn_rows = jnp.pad(ck_new.transpose(0, 2, 1), ((0, 0), (0, 0), (0, page - t)))
    per_seq = lambda shape: pl.BlockSpec((None,) + shape, lambda b, g, pt: (b, 0, 0))

    def page_spec(i):
        return pl.BlockSpec((None, page, d), lambda b, g, pt: (pt[b, g * n_grp + i], 0, 0))

    in_specs = [per_seq((t, d)), per_seq((rows, 1)),
                pl.BlockSpec((None, h, n_grp * page), lambda b, g, pt: (b, 0, g)),
                per_seq((h, page)), per_seq((t, d)), per_seq((t, d))]
    in_specs += [page_spec(i) for i in range(n_grp)] * 2
    return pl.pallas_call(
        functools.partial(_paged_kernel, n_grp),
        grid_spec=pltpu.PrefetchScalarGridSpec(
            num_scalar_prefetch=1, grid=(n_seq, n_pages // n_grp), in_specs=in_specs,
            out_specs=per_seq((t, d)),
            scratch_shapes=[pltpu.VMEM((rows, d), BF16), pltpu.VMEM((rows, 1), F32), pltpu.VMEM((rows, 1), F32),
                            pltpu.VMEM((rows, d), F32)]),
        out_shape=jax.ShapeDtypeStruct((n_seq, t, d), F32),
        compiler_params=_cparams("parallel", "arbitrary"),
        name="fox_sample",
    )(page_table, q, cq_col, ckp_rows, ckn_rows, k_new, v_new, *([cache_k] * n_grp), *([cache_v] * n_grp))


def _largest_tile(n, cap):
    t = min(n, cap)
    while n % t:
        t //= 2
    return t


def _trunk(x3, mods, wkv0, shift0, past, w):
    n_seq, t, d = x3.shape
    n = n_seq * t
    depth = mods.shape[0]
    n_a = wkv0.shape[0]
    h = d // HEAD_DIM
    x = x3.reshape(n, d)
    per_seq = t >= 256
    if per_seq:
        tm_ffn, tm_mix = _largest_tile(t, 1024), _largest_tile(t, 256)
        chunk, tb = 64, _largest_tile(t, 256)
    else:
        tm_ffn = tm_mix = n
        chunk = tb = t

    def mod4(l, sub):
        m = mods[l, :, sub]
        if per_seq:
            return m.reshape(n_seq, 3, 1, d)
        return jnp.repeat(m, t, axis=0).transpose(1, 0, 2).reshape(1, 3, n, d)

    def tiles(tm):
        return max(t // tm, 1)

    v_first = None
    wkv_new, shift_new = [], []
    kv = None
    for l in range(depth):
        if l == n_a:
            k_new, v_new, kb, vb, logf = _shared_kv(x, w['g_kv'], w['w_kv'], w['w_f'], w['b_f'], tm_mix)
            if past is None:
                ident = jnp.arange(n // LANES, dtype=jnp.int32).reshape(n_seq, t // LANES)
                ck, = _cumsum_pages(ident, logf.reshape(n // LANES, LANES, h), None)
                kv = (kb, vb, ck)
            else:
                cache_k, cache_v, cache_logf, page_table = past
                ck_past, ck_new = _cumsum_pages(page_table, cache_logf, logf.reshape(n_seq, t, h))
                kv = (cache_k, cache_v, page_table, ck_past, ck_new, k_new.reshape(n_seq, t, d),
                      v_new.reshape(n_seq, t, d))
        gn = w['g_norm'][l]
        x = _ffn(x, mod4(l, 0), gn[0], w['w_ffn_in'][l, 0], w['w_ffn_out'][l, 0], tm_ffn, tiles(tm_ffn))
        m4 = mod4(l, 1)
        if l < n_a:
            p = {key: w[key + '_a'][l] for key in ('mu', 'w_rkv', 'w0', 'w1', 'w2', 'a0', 'a1', 'a2', 'g1', 'g2',
                                                   'kk', 'ka')}
            if l > 0:
                p.update({key: w[key + '_a'][l - 1] for key in ('v0', 'v1', 'v2')})
            hl = shift0[l].reshape(n_seq, 1, d) if per_seq else jnp.repeat(shift0[l], t, axis=0)
            r, lw, k, v, kk, a, gt, hmod = _rwkv_proj(x, m4, gn[1], hl, t, tm_mix, p, v_first if l > 0 else None)
            if l == 0:
                v_first = v
            o, s_new = _wkv(r, lw, k, v, kk, a, _pair_state(wkv0[l]), w['lnx_w_a'][l], w['lnx_b_a'][l],
                            w['rk_a'][l], t, chunk, tb, 2 if per_seq else h // PAIR)
            wkv_new.append(_unpair_state(s_new))
            shift_new.append(hmod.reshape(n_seq, t, d)[:, -1])
            x = _out_proj(o, gt, x, m4, gn[1], w['w_o_a'][l], tm_mix, tiles(tm_mix))
        else:
            j = l - n_a
            q = _q_proj(x, m4, gn[1], w['w_q_b'][j], tm_mix, tiles(tm_mix))
            if past is None:
                o = _flash_prompt(q, kv[0], kv[1], kv[2], n_seq, _largest_tile(t, 512))
            else:
                cache_k, cache_v, page_table, ck_past, ck_new, k3, v3 = kv
                o = _paged_attention(q.reshape(n_seq, t, d), cache_k, cache_v, page_table, ck_past, ck_new, k3, v3,
                                     _largest_tile(page_table.shape[1], 4)).reshape(n, d)
            x = _out_proj(o, None, x, m4, gn[1], w['w_o_b'][j], tm_mix, tiles(tm_mix))
        x = _ffn(x, mod4(l, 2), gn[2], w['w_ffn_in'][l, 1], w['w_ffn_out'][l, 1], tm_ffn, tiles(tm_ffn))
    return (x.reshape(n_seq, t, d), jnp.stack(wkv_new), jnp.stack(shift_new), k_new.reshape(n_seq, t, h, HEAD_DIM),
            v_new.reshape(n_seq, t, h, HEAD_DIM), logf.reshape(n_seq, t, h))


def kernel(x_prompt, x_sample, state_wkv, state_shift, cache_k, cache_v, cache_logf, page_table, c_prompt, c_sample,
           w_mod, b_mod, g_norm, w_ffn_in, w_ffn_out, mu_a, w_rkv_a, w_o_a, w0_a, w1_a, w2_a, a0_a, a1_a, a2_a,
           v0_a, v1_a, v2_a, g1_a, g2_a, kk_a, ka_a, rk_a, lnx_w_a, lnx_b_a, g_kv, w_kv, w_f, b_f, w_q_b, w_o_b):
    bp, _, d = x_prompt.shape
    bs = x_sample.shape[0]
    depth = w_mod.shape[0]
    n_a = state_wkv.shape[0]
    n_pool, page = cache_k.shape[:2]
    cast = lambda a: a.astype(BF16)
    w = dict(g_norm=g_norm, w_ffn_in=cast(w_ffn_in), w_ffn_out=cast(w_ffn_out), mu_a=mu_a, w_rkv_a=cast(w_rkv_a),
             w_o_a=cast(w_o_a), w0_a=w0_a, w1_a=cast(w1_a), w2_a=cast(w2_a), a0_a=a0_a, a1_a=cast(a1_a),
             a2_a=cast(a2_a), v0_a=v0_a, v1_a=cast(v1_a), v2_a=cast(v2_a), g1_a=cast(g1_a), g2_a=cast(g2_a),
             kk_a=kk_a, ka_a=ka_a, rk_a=rk_a.reshape(n_a, d), lnx_w_a=lnx_w_a, lnx_b_a=lnx_b_a, g_kv=g_kv,
             w_kv=cast(w_kv), w_f=cast(w_f), b_f=b_f, w_q_b=cast(w_q_b), w_o_b=cast(w_o_b))

    mods = _mods(jnp.concatenate([c_prompt, c_sample], axis=0), w_mod, b_mod)
    mods = mods.reshape(depth, bp + bs, 3, 3, d)

    n_heads = d // HEAD_DIM
    zeros_wkv = jnp.zeros((n_a, bp, n_heads, HEAD_DIM, HEAD_DIM), F32)
    zeros_shift = jnp.zeros((n_a, bp, d), F32)
    y_p, wkv_p, shift_p, k_p, v_p, logf_p = _trunk(x_prompt, mods[:, :bp], zeros_wkv, zeros_shift, None, w)
    past = (cache_k.reshape(n_pool, page, d), cache_v.reshape(n_pool, page, d), cache_logf, page_table)
    y_s, wkv_s, shift_s, k_s, v_s, logf_s = _trunk(x_sample, mods[:, bp:], state_wkv, state_shift, past, w)
    return (y_p, y_s, wkv_p, shift_p, k_p, v_p, logf_p, wkv_s, shift_s, k_s, v_s, logf_s)
```

```python
import functools

import jax
import jax.numpy as jnp
import numpy as np
from jax import lax
from jax.experimental import pallas as pl
from jax.experimental.pallas import tpu as pltpu

F32 = jnp.float32
BF16 = jnp.bfloat16

NORM_EPS = 1e-6
GN_EPS = 64e-5
MACARON_W = 0.5
NEG_INF = -1e30
HEAD_DIM = 64
LANES = 128
PAIR = LANES // HEAD_DIM
VMEM_LIMIT = 56 * 1024 * 1024

LOG2E = 1.4426950408889634
N_BIAS = 3

_NT = (((1,), (1,)), ((), ()))
_TN = (((0,), (0,)), ((), ()))


def _cparams(*sem):
    return pltpu.CompilerParams(dimension_semantics=sem, vmem_limit_bytes=VMEM_LIMIT)


def _dot(a, b):
    return jnp.dot(a.astype(BF16), b.astype(BF16), preferred_element_type=F32)


def _dot_nt(a, b):
    return lax.dot_general(a.astype(BF16), b.astype(BF16), _NT, preferred_element_type=F32)


def _dot_tn(a, b):
    return lax.dot_general(a.astype(BF16), b.astype(BF16), _TN, preferred_element_type=F32)


def _split3(x):
    hi = x.astype(BF16)
    r1 = x - hi.astype(F32)
    mid = r1.astype(BF16)
    lo = (r1 - mid.astype(F32)).astype(BF16)
    return hi, mid, lo


def _dot3(sel, x):
    hi, mid, lo = _split3(x)
    out = jnp.dot(sel, hi, preferred_element_type=F32)
    out = out + jnp.dot(sel, mid, preferred_element_type=F32)
    return out + jnp.dot(sel, lo, preferred_element_type=F32)


def _dot3_rhs(x, sel):
    hi, mid, lo = _split3(x)
    out = jnp.dot(hi, sel, preferred_element_type=F32)
    out = out + jnp.dot(mid, sel, preferred_element_type=F32)
    return out + jnp.dot(lo, sel, preferred_element_type=F32)


def _rms(x, g):
    return x * lax.rsqrt(jnp.mean(x * x, axis=-1, keepdims=True) + NORM_EPS) * g


def _sigmoid(x):
    return 1.0 / (1.0 + jnp.exp(-x))


def _silu(x):
    return x * _sigmoid(x)


def _head_ones(n):
    r = lax.broadcasted_iota(jnp.int32, (n, n), 0) // HEAD_DIM
    c = lax.broadcasted_iota(jnp.int32, (n, n), 1) // HEAD_DIM
    return jnp.where(r == c, 1.0, 0.0).astype(BF16)


def _head_sum(x, ones):
    n = ones.shape[0]
    hi = x.astype(BF16)
    lo = (x - hi.astype(F32)).astype(BF16)
    cols = []
    for i in range(0, x.shape[1], n):
        s = jnp.dot(hi[:, i:i + n], ones, preferred_element_type=F32)
        cols.append(s + jnp.dot(lo[:, i:i + n], ones, preferred_element_type=F32))
    return cols[0] if len(cols) == 1 else jnp.concatenate(cols, axis=1)


def _mods_kernel(c_ref, w_ref, b_ref, o_ref):
    o_ref[...] = _dot(_silu(c_ref[...]), w_ref[...]) + b_ref[...]


def _mods(c_all, w_mod, b_mod):
    depth, d, n9 = w_mod.shape
    nb = c_all.shape[0]
    tn = d
    return pl.pallas_call(
        _mods_kernel,
        grid=(depth, n9 // tn),
        in_specs=[
            pl.BlockSpec((nb, d), lambda l, j: (0, 0)),
            pl.BlockSpec((None, d, tn), lambda l, j: (l, 0, j)),
            pl.BlockSpec((None, 1, tn), lambda l, j: (l, 0, j)),
        ],
        out_specs=pl.BlockSpec((None, nb, tn), lambda l, j: (l, 0, j)),
        out_shape=jax.ShapeDtypeStruct((depth, nb, n9), F32),
        compiler_params=_cparams("arbitrary", "arbitrary"),
        name="mods",
    )(c_all, w_mod, b_mod.reshape(depth, 1, n9))


def _mod_spec(m4, tm, tiles_per_seq):
    _, _, rows, d = m4.shape
    if rows == 1:
        return pl.BlockSpec((None, 3, 1, d), lambda i, *_: (i // tiles_per_seq, 0, 0, 0))
    return pl.BlockSpec((None, 3, tm, d), lambda i, *_: (0, 0, i, 0))


def _ffn_kernel(x_ref, m_ref, g_ref, wg_ref, wu_ref, wo_ref, o_ref, h_scr, acc_scr):
    j = pl.program_id(1)

    @pl.when(j == 0)
    def _():
        h = _rms(x_ref[...], g_ref[0:1]) * (1.0 + m_ref[1]) + m_ref[0]
        h_scr[...] = h.astype(BF16)
        acc_scr[...] = jnp.zeros_like(acc_scr)

    h = h_scr[...]
    gate = jnp.dot(h, wg_ref[...], preferred_element_type=F32)
    up = jnp.dot(h, wu_ref[...], preferred_element_type=F32)
    acc_scr[...] += _dot(_silu(gate) * up, wo_ref[...])

    @pl.when(j == pl.num_programs(1) - 1)
    def _():
        o_ref[...] = x_ref[...] + MACARON_W * m_ref[2] * _rms(acc_scr[...], g_ref[1:2])


def _ffn(x, m4, gn, w_in, w_out, tm, tiles_per_seq):
    n, d = x.shape
    f = w_out.shape[0]
    tf = 256 if f % 256 == 0 else LANES
    nf = f // tf
    return pl.pallas_call(
        _ffn_kernel,
        grid=(n // tm, nf),
        in_specs=[
            pl.BlockSpec((tm, d), lambda i, j: (i, 0)),
            _mod_spec(m4, tm, tiles_per_seq),
            pl.BlockSpec((2, d), lambda i, j: (0, 0)),
            pl.BlockSpec((d, tf), lambda i, j: (0, j)),
            pl.BlockSpec((d, tf), lambda i, j: (0, j + nf)),
            pl.BlockSpec((tf, d), lambda i, j: (j, 0)),
        ],
        out_specs=pl.BlockSpec((tm, d), lambda i, j: (i, 0)),
        out_shape=jax.ShapeDtypeStruct((n, d), F32),
        scratch_shapes=[pltpu.VMEM((tm, d), BF16), pltpu.VMEM((tm, d), F32)],
        compiler_params=_cparams("parallel", "arbitrary"),
        name="ffn",
    )(x, m4, gn, w_in, w_in, w_out)


def _rwkv_proj_kernel(has_vfirst, seq_len, x_ref, m_ref, g_ref, hl_ref, mu_ref, wrkv_ref, w0_ref, w1_ref,
                      w2_ref, a0_ref, a1_ref, a2_ref, g1_ref, g2_ref, kka_ref, kaa_ref, *rest):
    if has_vfirst:
        v0_ref, v1_ref, v2_ref, vf_ref = rest[:4]
        rest = rest[4:]
    r_ref, lw_ref, k_ref, v_ref, kk_ref, a_ref, gt_ref, h_ref, carry_scr = rest
    tm, d = x_ref.shape
    i = pl.program_id(0)

    h = _rms(x_ref[...], g_ref[0:1]) * (1.0 + m_ref[1]) + m_ref[0]
    h_ref[...] = h
    row = lax.broadcasted_iota(jnp.int32, (tm, d), 0)
    rolled = pltpu.roll(h, 1, 0)
    if hl_ref.shape[0] == 1:
        tiles_per_seq = seq_len // tm
        first = jnp.where(i % tiles_per_seq == 0, hl_ref[...], carry_scr[...])
        h_prev = jnp.where(row == 0, first, rolled)
        carry_scr[...] = h[tm - 1:tm, :]
    else:
        h_prev = jnp.where(row % seq_len == 0, hl_ref[...], rolled)
    xx = h_prev - h
    xr, xw, xk, xv, xa, xg = [h + xx * mu_ref[j:j + 1] for j in range(6)]

    r = _dot(xr, wrkv_ref[0])
    k = _dot(xk, wrkv_ref[1])
    v = _dot(xv, wrkv_ref[2])
    z = w0_ref[...] + _dot(jnp.tanh(_dot(xw, w1_ref[...])), w2_ref[...])
    lw = -_sigmoid(z) * jnp.exp(F32(-0.5))
    a = _sigmoid(a0_ref[...] + _dot(_dot(xa, a1_ref[...]), a2_ref[...]))
    gt = _dot(_sigmoid(_dot(xg, g1_ref[...])), g2_ref[...])
    if has_vfirst:
        mix = _sigmoid(v0_ref[...] + _dot(_dot(xv, v1_ref[...]), v2_ref[...]))
        v = v + (vf_ref[...] - v) * mix
    kk = k * kka_ref[...]
    ones = _head_ones(min(d, 2 * LANES))
    kk = kk * lax.rsqrt(jnp.maximum(_head_sum(kk * kk, ones), 1e-24))
    k = k * (1.0 + (a - 1.0) * kaa_ref[...])

    r_ref[...] = r
    lw_ref[...] = lw
    k_ref[...] = k
    v_ref[...] = v
    kk_ref[...] = kk
    a_ref[...] = a
    gt_ref[...] = gt


def _rwkv_proj(x, m4, gn, hl, seq_len, tm, p, v_first):
    n, d = x.shape
    tiles_per_seq = max(seq_len // tm, 1)
    has_vfirst = v_first is not None
    row = lambda a: a.reshape(1, d)
    const = lambda a: pl.BlockSpec(a.shape, lambda i: (0,) * a.ndim)
    tok = pl.BlockSpec((tm, d), lambda i: (i, 0))
    if hl.ndim == 3:
        hl_spec = pl.BlockSpec((None, 1, d), lambda i: (i // tiles_per_seq, 0, 0))
    else:
        hl_spec = tok
    args = [x, m4, gn, hl, p['mu'], p['w_rkv'], row(p['w0']), p['w1'], p['w2'], row(p['a0']), p['a1'], p['a2'],
            p['g1'], p['g2'], row(p['kk']), row(p['ka'])]
    specs = [tok, _mod_spec(m4, tm, tiles_per_seq), const(gn), hl_spec] + [const(a) for a in args[4:]]
    if has_vfirst:
        extra = [row(p['v0']), p['v1'], p['v2']]
        args += extra + [v_first]
        specs += [const(a) for a in extra] + [tok]
    return pl.pallas_call(
        functools.partial(_rwkv_proj_kernel, has_vfirst, seq_len),
        grid=(n // tm,),
        in_specs=specs,
        out_specs=[tok] * 8,
        out_shape=[jax.ShapeDtypeStruct((n, d), F32)] * 8,
        scratch_shapes=[pltpu.VMEM((1, d), F32)],
        compiler_params=_cparams("arbitrary"),
        name="rwkv_proj",
    )(*args)


def _wkv_kernel(chunk, r_ref, lw_ref, k_ref, v_ref, kk_ref, a_ref, s0_ref, lnw_ref, lnb_ref, rk_ref,
                o_ref, sout_ref, s_scr):
    c = chunk
    n2 = PAIR * c
    tb, width = r_ref.shape
    n_pairs = width // LANES
    t_idx = pl.program_id(2)

    @pl.when(t_idx == 0)
    def _():
        s_scr[...] = s0_ref[...]

    lane = lax.broadcasted_iota(jnp.int32, (c, LANES), 1)
    head_lane = [lane // HEAD_DIM == h for h in range(PAIR)]
    ri = lax.broadcasted_iota(jnp.int32, (c, c), 0)
    ci = lax.broadcasted_iota(jnp.int32, (c, c), 1)
    tri = jnp.where(ci <= ri, 1.0, 0.0).astype(BF16)
    r2 = lax.broadcasted_iota(jnp.int32, (n2, n2), 0)
    c2 = lax.broadcasted_iota(jnp.int32, (n2, n2), 1)
    strict = c2 < r2
    incl = c2 <= r2
    eye = jnp.where(c2 == r2, 1.0, 0.0)
    level_masks = []
    b = 1
    while b < c:
        level_masks.append((r2 // (2 * b) == c2 // (2 * b)) & ((r2 // b) % 2 == 1) & ((c2 // b) % 2 == 0))
        b *= 2
    ones = _head_ones(LANES)

    def stack(x):
        return jnp.concatenate([jnp.where(m, x, 0.0) for m in head_lane], axis=0)

    def one_chunk(ic, carry):
        rows = pl.ds(pl.multiple_of(ic * c, c), c)
        for p in range(n_pairs):
            ls = slice(p * LANES, (p + 1) * LANES)
            rc, lwc, kc, vc, kkc, ac = [ref[rows, ls] for ref in (r_ref, lw_ref, k_ref, v_ref, kk_ref, a_ref)]
            cum = _dot3(tri, lwc)
            g = jnp.exp(cum)
            gi = jnp.exp(-cum)
            g_end = g[c - 1:c, :]
            akk = ac * kkc
            a2 = stack(-kkc * jnp.exp(cum - lwc))
            b2 = stack(akk * gi)
            k2 = stack(kc * gi)
            q2 = stack(rc * g)
            v2 = stack(vc)
            bt2 = stack(akk * gi * g_end)
            kt2 = stack(kc * gi * g_end)

            l_ab = jnp.where(strict, _dot_nt(a2, b2), 0.0)
            l_ak = jnp.where(strict, _dot_nt(a2, k2), 0.0)
            l_rb = jnp.where(incl, _dot_nt(q2, b2), 0.0)
            l_rk = jnp.where(incl, _dot_nt(q2, k2), 0.0)
            tinv = eye
            for mask in level_masks:
                tinv = tinv + _dot(tinv, _dot(jnp.where(mask, l_ab, 0.0), tinv))

            s = s_scr[p]
            u2 = _dot(tinv, _dot_nt(a2, s) + _dot(l_ak, v2))
            o2 = _dot_nt(q2, s) + _dot(l_rb, u2) + _dot(l_rk, v2)
            s_scr[p] = s * g_end + _dot_tn(u2, bt2) + _dot_tn(v2, kt2)

            o = o2[0:c]
            for h in range(1, PAIR):
                o = o + o2[h * c:(h + 1) * c]
            mean = _head_sum(o, ones) * (1.0 / HEAD_DIM)
            cen = o - mean
            var = _head_sum(cen * cen, ones) * (1.0 / HEAD_DIM)
            o = cen * lax.rsqrt(var + GN_EPS) * lnw_ref[:, ls] + lnb_ref[:, ls]
            o_ref[rows, ls] = o + _head_sum(rc * kc * rk_ref[:, ls], ones) * vc
        return carry

    lax.fori_loop(0, tb // c, one_chunk, 0)

    @pl.when(t_idx == pl.num_programs(2) - 1)
    def _():
        sout_ref[...] = s_scr[...]


def _wkv(r, lw, k, v, kk, a, s0, lnw, lnb, rk, seq_len, chunk, tb, n_pairs):
    n, d = r.shape
    n_seq = n // seq_len
    width = n_pairs * LANES
    nt = seq_len // tb
    tok = pl.BlockSpec((tb, width), lambda b, g, t: (b * nt + t, g))
    vec = pl.BlockSpec((1, width), lambda b, g, t: (0, g))
    st = pl.BlockSpec((None, n_pairs, LANES, LANES), lambda b, g, t: (b, g, 0, 0))
    return pl.pallas_call(
        functools.partial(_wkv_kernel, chunk),
        grid=(n_seq, d // width, nt),
        in_specs=[tok] * 6 + [st, vec, vec, vec],
        out_specs=[tok, st],
        out_shape=[jax.ShapeDtypeStruct((n, d), F32), jax.ShapeDtypeStruct(s0.shape, F32)],
        scratch_shapes=[pltpu.VMEM((n_pairs, LANES, LANES), F32)],
        compiler_params=_cparams("parallel", "parallel", "arbitrary"),
        name="wkv",
    )(r, lw, k, v, kk, a, s0, lnw.reshape(1, d), lnb.reshape(1, d), rk.reshape(1, d))


def _pair_state(s):
    n_seq, h, n, _ = s.shape
    s = s.reshape(n_seq, h // PAIR, PAIR, n, n)
    eye = jnp.eye(PAIR, dtype=s.dtype)
    out = s[:, :, :, :, None, :] * eye[None, None, :, None, :, None]
    return out.reshape(n_seq, h // PAIR, PAIR * n, PAIR * n)


def _unpair_state(sp):
    n_seq, hp, n2, _ = sp.shape
    n = n2 // PAIR
    s = sp.reshape(n_seq, hp, PAIR, n, PAIR, n)
    return jnp.stack([s[:, :, a, :, a, :] for a in range(PAIR)], axis=2).reshape(n_seq, hp * PAIR, n, n)


def _out_kernel(has_gate, res_w, *refs):
    if has_gate:
        a_ref, gt_ref, x_ref, m_ref, g_ref, w_ref, o_ref = refs
        a = a_ref[...] * gt_ref[...]
    else:
        a_ref, x_ref, m_ref, g_ref, w_ref, o_ref = refs
        a = a_ref[...]
    y = _dot(a, w_ref[...])
    o_ref[...] = x_ref[...] + res_w * m_ref[2] * _rms(y, g_ref[1:2])


def _out_proj(a, gate, x, m4, gn, w, tm, tiles_per_seq, res_w=1.0):
    n, d = x.shape
    tok = pl.BlockSpec((tm, d), lambda i: (i, 0))
    args = [a] + ([gate] if gate is not None else []) + [x, m4, gn, w]
    specs = [tok] * (len(args) - 3) + [_mod_spec(m4, tm, tiles_per_seq), pl.BlockSpec((2, d), lambda i: (0, 0)),
                                       pl.BlockSpec(w.shape, lambda i: (0, 0))]
    return pl.pallas_call(
        functools.partial(_out_kernel, gate is not None, res_w),
        grid=(n // tm,),
        in_specs=specs,
        out_specs=tok,
        out_shape=jax.ShapeDtypeStruct((n, d), F32),
        compiler_params=_cparams("parallel"),
        name="out_proj",
    )(*args)


def _kv_kernel(x_ref, g_ref, wkv_ref, wf_ref, bf_ref, k_ref, v_ref, kb_ref, vb_ref, lf_ref):
    d = x_ref.shape[1]
    s = _rms(x_ref[...], g_ref[...])
    kv = _dot(s, wkv_ref[...])
    k = kv[:, :d]
    v = kv[:, d:]
    k_ref[...] = k
    v_ref[...] = v
    kb_ref[...] = k.astype(BF16)
    vb_ref[...] = v.astype(BF16)
    z = _dot(s, wf_ref[...]) + bf_ref[...]
    lf_ref[...] = jnp.minimum(z, 0.0) - jnp.log(1.0 + jnp.exp(-jnp.abs(z)))


def _shared_kv(x, g_kv, w_kv, w_f, b_f, tm):
    n, d = x.shape
    h = w_f.shape[1]
    tok = pl.BlockSpec((tm, d), lambda i: (i, 0))
    return pl.pallas_call(
        _kv_kernel,
        grid=(n // tm,),
        in_specs=[tok, pl.BlockSpec((1, d), lambda i: (0, 0)), pl.BlockSpec(w_kv.shape, lambda i: (0, 0)),
                  pl.BlockSpec(w_f.shape, lambda i: (0, 0)), pl.BlockSpec((1, h), lambda i: (0, 0))],
        out_specs=[tok, tok, tok, tok, pl.BlockSpec((tm, h), lambda i: (i, 0))],
        out_shape=[jax.ShapeDtypeStruct((n, d), F32)] * 2 + [jax.ShapeDtypeStruct((n, d), BF16)] * 2
        + [jax.ShapeDtypeStruct((n, h), F32)],
        compiler_params=_cparams("parallel"),
        name="shared_kv",
    )(x, g_kv.reshape(1, d), w_kv, w_f, b_f.reshape(1, h))


def _cumsum_kernel(has_new, has_parts, pt_ref, lf_ref, *rest):
    if has_new:
        new_ref, ck_ref, cknew_ref, carry_scr = rest
    elif has_parts:
        ck_ref, parts_ref, carry_scr = rest
    else:
        ck_ref, carry_scr = rest
    del pt_ref
    j = pl.program_id(1)
    page, h = lf_ref.shape

    @pl.when(j == 0)
    def _():
        carry_scr[...] = jnp.zeros_like(carry_scr)

    ri = lax.broadcasted_iota(jnp.int32, (page, page), 0)
    ci = lax.broadcasted_iota(jnp.int32, (page, page), 1)
    tri = jnp.where(ci <= ri, 1.0, 0.0).astype(BF16)
    ck = carry_scr[...] + _dot3(tri, lf_ref[...])
    ck_ref[...] = ck
    carry_scr[...] = ck[page - 1:page, :]

    if has_parts:
        src_h = lax.broadcasted_iota(jnp.int32, (h, LANES), 0)
        dst = lax.broadcasted_iota(jnp.int32, (h, LANES), 1)
        out = jnp.where(lax.broadcasted_iota(jnp.int32, (page, LANES), 1) // h == N_BIAS, 1.0, 0.0)
        for x, part in enumerate(_split3(ck * LOG2E)):
            out = out + jnp.dot(part, jnp.where(dst == x * h + src_h, 1.0, 0.0).astype(BF16),
                                preferred_element_type=F32)
        parts_ref[...] = out.astype(BF16)

    if has_new:
        @pl.when(j == pl.num_programs(1) - 1)
        def _():
            t = new_ref.shape[0]
            cknew_ref[...] = ck[page - 1:page, :] + _dot3(tri[:t, :t], new_ref[...])


def _cumsum_pages(page_table, lf_pages, lf_new, with_parts=False):
    n_seq, n_pages = page_table.shape
    _, page, h = lf_pages.shape
    has_new = lf_new is not None
    in_specs = [pl.BlockSpec((None, page, h), lambda b, j, pt: (pt[b, j], 0, 0))]
    out_specs = [pl.BlockSpec((None, page, h), lambda b, j, pt: (b, j, 0))]
    out_shape = [jax.ShapeDtypeStruct((n_seq, n_pages * page, h), F32)]
    args = [lf_pages]
    if has_new:
        t = lf_new.shape[1]
        in_specs.append(pl.BlockSpec((None, t, h), lambda b, j, pt: (b, 0, 0)))
        out_specs.append(pl.BlockSpec((None, t, h), lambda b, j, pt: (b, 0, 0)))
        out_shape.append(jax.ShapeDtypeStruct((n_seq, t, h), F32))
        args.append(lf_new)
    elif with_parts:
        out_specs.append(pl.BlockSpec((None, page, LANES), lambda b, j, pt: (b, j, 0)))
        out_shape.append(jax.ShapeDtypeStruct((n_seq, n_pages * page, LANES), BF16))
    return pl.pallas_call(
        functools.partial(_cumsum_kernel, has_new, with_parts and not has_new),
        grid_spec=pltpu.PrefetchScalarGridSpec(
            num_scalar_prefetch=1, grid=(n_seq, n_pages), in_specs=in_specs, out_specs=out_specs,
            scratch_shapes=[pltpu.VMEM((1, h), F32)]),
        out_shape=out_shape,
        compiler_params=_cparams("parallel", "arbitrary"),
        name="cumsum",
    )(page_table, *args)


def _q_kernel(x_ref, m_ref, g_ref, w_ref, q_ref):
    h = _rms(x_ref[...], g_ref[0:1]) * (1.0 + m_ref[1]) + m_ref[0]
    q_ref[...] = (_dot(h, w_ref[...]) * (HEAD_DIM ** -0.5)).astype(BF16)


def _q_proj(x, m4, gn, w, tm, tiles_per_seq):
    n, d = x.shape
    tok = pl.BlockSpec((tm, d), lambda i: (i, 0))
    return pl.pallas_call(
        _q_kernel,
        grid=(n // tm,),
        in_specs=[tok, _mod_spec(m4, tm, tiles_per_seq), pl.BlockSpec((2, d), lambda i: (0, 0)),
                  pl.BlockSpec(w.shape, lambda i: (0, 0))],
        out_specs=tok,
        out_shape=jax.ShapeDtypeStruct((n, d), BF16),
        compiler_params=_cparams("parallel"),
        name="q_proj",
    )(x, m4, gn, w)


def _pair_to_tiles():
    m = np.zeros((LANES, PAIR * LANES), np.float32)
    r = np.arange(LANES)
    m[r, (r // HEAD_DIM) * LANES + r % HEAD_DIM] = 1.0
    return jnp.asarray(m, BF16)


def _bias_place(h, for_keys):
    m = np.zeros((LANES, h * LANES), np.float32)
    for hh in range(h):
        for x in range(N_BIAS):
            if for_keys:
                m[N_BIAS * h + hh, hh * LANES + HEAD_DIM + x] = 1.0
                m[x * h + hh, hh * LANES + HEAD_DIM + N_BIAS + x] = -1.0
            else:
                m[x * h + hh, hh * LANES + HEAD_DIM + x] = 1.0
                m[N_BIAS * h + hh, hh * LANES + HEAD_DIM + N_BIAS + x] = 1.0
    return jnp.asarray(m, BF16)


def _to_head_tiles(x, place, extra):
    cols = []
    for p in range(x.shape[1] // LANES):
        sl = slice(p * PAIR * LANES, (p + 1) * PAIR * LANES)
        cols.append(jnp.dot(x[:, p * LANES:(p + 1) * LANES], place, preferred_element_type=F32) + extra[:, sl])
    return jnp.concatenate(cols, axis=1)


def _q_aug_kernel(x_ref, m_ref, g_ref, w_ref, parts_ref, place_ref, bias_ref, q_ref):
    h = _rms(x_ref[...], g_ref[0:1]) * (1.0 + m_ref[1]) + m_ref[0]
    q = (_dot(h, w_ref[...]) * (HEAD_DIM ** -0.5 * LOG2E)).astype(BF16)
    bias = jnp.dot(parts_ref[...], bias_ref[...], preferred_element_type=F32)
    q_ref[...] = _to_head_tiles(q, place_ref[...], bias).astype(BF16)


def _q_aug(x, m4, gn, w, parts, tm, tiles_per_seq):
    n, d = x.shape
    tok = pl.BlockSpec((tm, d), lambda i: (i, 0))
    place, bias = _pair_to_tiles(), _bias_place(d // HEAD_DIM, False)
    const = lambda a: pl.BlockSpec(a.shape, lambda i: (0, 0))
    return pl.pallas_call(
        _q_aug_kernel,
        grid=(n // tm,),
        in_specs=[tok, _mod_spec(m4, tm, tiles_per_seq), pl.BlockSpec((2, d), lambda i: (0, 0)), const(w),
                  pl.BlockSpec((tm, LANES), lambda i: (i, 0)), const(place), const(bias)],
        out_specs=pl.BlockSpec((tm, PAIR * d), lambda i: (i, 0)),
        out_shape=jax.ShapeDtypeStruct((n, PAIR * d), BF16),
        compiler_params=_cparams("parallel"),
        name="q_aug",
    )(x, m4, gn, w, parts, place, bias)


def _kv_aug_kernel(k_ref, v_ref, parts_ref, place_ref, bias_ref, ka_ref, va_ref):
    tm, d2 = ka_ref.shape
    bias = jnp.dot(parts_ref[...], bias_ref[...], preferred_element_type=F32)
    ka_ref[...] = _to_head_tiles(k_ref[...], place_ref[...], bias).astype(BF16)
    ones = jnp.where(lax.broadcasted_iota(jnp.int32, (1, d2), 1) % LANES >= HEAD_DIM, 1.0, 0.0)
    va_ref[...] = _to_head_tiles(v_ref[...], place_ref[...], ones).astype(BF16)


def _kv_aug(kb, vb, parts, tm):
    n, d = kb.shape
    tok = pl.BlockSpec((tm, d), lambda i: (i, 0))
    wide = pl.BlockSpec((tm, PAIR * d), lambda i: (i, 0))
    place, bias = _pair_to_tiles(), _bias_place(d // HEAD_DIM, True)
    const = lambda a: pl.BlockSpec(a.shape, lambda i: (0, 0))
    return pl.pallas_call(
        _kv_aug_kernel,
        grid=(n // tm,),
        in_specs=[tok, tok, pl.BlockSpec((tm, LANES), lambda i: (i, 0)), const(place), const(bias)],
        out_specs=[wide, wide],
        out_shape=[jax.ShapeDtypeStruct((n, PAIR * d), BF16)] * 2,
        compiler_params=_cparams("parallel"),
        name="kv_aug",
    )(kb, vb, parts, place, bias)


def _flash_kernel(qi_ref, kj_ref, q_ref, k_ref, v_ref, o_ref, m_scr, acc_scr):
    n = pl.program_id(2)
    qi = qi_ref[n]
    kj = kj_ref[n]
    tq = q_ref.shape[0]
    tk = k_ref.shape[0]
    n_heads = q_ref.shape[1] // LANES

    @pl.when(kj == 0)
    def _():
        m_scr[...] = jnp.full_like(m_scr, NEG_INF)
        acc_scr[...] = jnp.zeros_like(acc_scr)

    def update(masked):
        if masked:
            valid = (lax.broadcasted_iota(jnp.int32, (tq, tk), 1)
                     <= lax.broadcasted_iota(jnp.int32, (tq, tk), 0))
        for h in range(n_heads):
            sl = slice(h * LANES, (h + 1) * LANES)
            s = lax.dot_general(q_ref[:, sl], k_ref[:, sl], _NT, preferred_element_type=F32)
            if masked:
                s = jnp.where(valid, s, NEG_INF)
            cols = [s[:, c * LANES:(c + 1) * LANES] for c in range(tk // LANES)]
            m_cur = cols[0]
            for col in cols[1:]:
                m_cur = jnp.maximum(m_cur, col)
            m_old = m_scr[h]
            m_new = jnp.maximum(m_old, jnp.max(m_cur, axis=1, keepdims=True))
            p = jnp.concatenate([jnp.exp2(col - m_new) for col in cols], axis=1).astype(BF16)
            acc_scr[h] = jnp.exp2(m_old - m_new) * acc_scr[h] + jnp.dot(p, v_ref[:, sl],
                                                                        preferred_element_type=F32)
            m_scr[h] = m_new

    @pl.when(kj < qi)
    def _():
        update(False)

    @pl.when(kj == qi)
    def _():
        update(True)
        lane = lax.broadcasted_iota(jnp.int32, (tq, LANES), 1)
        outs = []
        for h in range(n_heads):
            acc = acc_scr[h]
            outs.append(acc / pltpu.roll(acc, HEAD_DIM, 1))
        for p in range(n_heads // PAIR):
            o_ref[:, p * LANES:(p + 1) * LANES] = jnp.where(
                lane < HEAD_DIM, outs[PAIR * p], pltpu.roll(outs[PAIR * p + 1], HEAD_DIM, 1))


def _flash_prompt(qa, ka, va, n_seq, tq, heads_per_step):
    n, d2 = qa.shape
    t = n // n_seq
    nq = t // tq
    width = heads_per_step * LANES
    qi = jnp.asarray([i for i in range(nq) for _ in range(i + 1)], jnp.int32)
    kj = jnp.asarray([j for i in range(nq) for j in range(i + 1)], jnp.int32)
    return pl.pallas_call(
        _flash_kernel,
        grid_spec=pltpu.PrefetchScalarGridSpec(
            num_scalar_prefetch=2,
            grid=(n_seq, d2 // width, int(qi.shape[0])),
            in_specs=[
                pl.BlockSpec((tq, width), lambda b, p, s, qi, kj: (b * nq + qi[s], p)),
                pl.BlockSpec((tq, width), lambda b, p, s, qi, kj: (b * nq + kj[s], p)),
                pl.BlockSpec((tq, width), lambda b, p, s, qi, kj: (b * nq + kj[s], p)),
            ],
            out_specs=pl.BlockSpec((tq, width // PAIR), lambda b, p, s, qi, kj: (b * nq + qi[s], p)),
            scratch_shapes=[pltpu.VMEM((heads_per_step, tq, LANES), F32),
                            pltpu.VMEM((heads_per_step, tq, LANES), F32)]),
        out_shape=jax.ShapeDtypeStruct((n, d2 // PAIR), F32),
        compiler_params=_cparams("parallel", "parallel", "arbitrary"),
        name="fox_prompt",
    )(qi, kj, qa, ka, va)


def _paged_kernel(n_grp, pt_ref, q_ref, cq_ref, ckp_ref, ckn_ref, kn_ref, vn_ref, *rest):
    k_refs = rest[:n_grp]
    v_refs = rest[n_grp:2 * n_grp]
    o_ref, qbd_scr, m_scr, l_scr, acc_scr = rest[2 * n_grp:]
    del pt_ref
    g = pl.program_id(1)
    t, d = q_ref.shape
    n_heads = d // HEAD_DIM
    rows = n_heads * t
    page = k_refs[0].shape[0]

    row_head = lax.broadcasted_iota(jnp.int32, (rows, d), 0) // t
    lane_head = lax.broadcasted_iota(jnp.int32, (rows, d), 1) // HEAD_DIM
    expand = jnp.where(lax.broadcasted_iota(jnp.int32, (rows, n_heads), 0) // t
                       == lax.broadcasted_iota(jnp.int32, (rows, n_heads), 1), 1.0, 0.0).astype(BF16)

    @pl.when(g == 0)
    def _():
        q = q_ref[...].astype(F32)
        q_rows = jnp.concatenate([q] * n_heads, axis=0)
        qbd_scr[...] = jnp.where(row_head == lane_head, q_rows, 0.0).astype(BF16)
        m_scr[...] = jnp.full_like(m_scr, NEG_INF)
        l_scr[...] = jnp.zeros_like(l_scr)
        acc_scr[...] = jnp.zeros_like(acc_scr)

    qbd = qbd_scr[...]
    cq = cq_ref[...]

    def attend(s, vs):
        m_old = m_scr[...]
        m_new = jnp.maximum(m_old, jnp.max(s, axis=1, keepdims=True))
        alpha = jnp.exp(m_old - m_new)
        p = jnp.exp(s - m_new)
        l_scr[...] = alpha * l_scr[...] + jnp.sum(p, axis=1, keepdims=True)
        pv = None
        for i, vpage in enumerate(vs):
            part = _dot(p[:, i * page:(i + 1) * page], vpage)
            pv = part if pv is None else pv + part
        acc_scr[...] = alpha * acc_scr[...] + pv
        m_scr[...] = m_new

    s = jnp.concatenate([_dot_nt(qbd, kr[...]) for kr in k_refs], axis=1)
    s = s + (cq - _dot3(expand, ckp_ref[...]))
    attend(s, [vr[...] for vr in v_refs])

    @pl.when(g == pl.num_programs(1) - 1)
    def _():
        pad = jnp.zeros((page - t, d), F32)
        k_new = jnp.concatenate([kn_ref[...], pad], axis=0)
        v_new = jnp.concatenate([vn_ref[...], pad], axis=0)
        s_new = _dot_nt(qbd, k_new) + (cq - _dot3(expand, ckn_ref[...]))
        key = lax.broadcasted_iota(jnp.int32, (rows, page), 1)
        query = lax.broadcasted_iota(jnp.int32, (rows, page), 0) % t
        attend(jnp.where((key <= query) & (key < t), s_new, NEG_INF), [v_new])
        out = acc_scr[...] / l_scr[...]
        out = jnp.where(row_head == lane_head, out, 0.0).reshape(n_heads, t, d)
        o_ref[...] = jnp.sum(out, axis=0)


def _paged_attention(q, cache_k, cache_v, page_table, ck_past, ck_new, k_new, v_new, n_grp):
    n_seq, t, d = q.shape
    n_pages = page_table.shape[1]
    page = cache_k.shape[1]
    h = d // HEAD_DIM
    rows = h * t
    assert n_pages % n_grp == 0
    cq_col = ck_new.transpose(0, 2, 1).reshape(n_seq, rows, 1)
    ckp_rows = ck_past.transpose(0, 2, 1)
    ckn_rows = jnp.pad(ck_new.transpose(0, 2, 1), ((0, 0), (0, 0), (0, page - t)))
    per_seq = lambda shape: pl.BlockSpec((None,) + shape, lambda b, g, pt: (b, 0, 0))

    def page_spec(i):
        return pl.BlockSpec((None, page, d), lambda b, g, pt: (pt[b, g * n_grp + i], 0, 0))

    in_specs = [per_seq((t, d)), per_seq((rows, 1)),
                pl.BlockSpec((None, h, n_grp * page), lambda b, g, pt: (b, 0, g)),
                per_seq((h, page)), per_seq((t, d)), per_seq((t, d))]
    in_specs += [page_spec(i) for i in range(n_grp)] * 2
    return pl.pallas_call(
        functools.partial(_paged_kernel, n_grp),
        grid_spec=pltpu.PrefetchScalarGridSpec(
            num_scalar_prefetch=1, grid=(n_seq, n_pages // n_grp), in_specs=in_specs,
            out_specs=per_seq((t, d)),
            scratch_shapes=[pltpu.VMEM((rows, d), BF16), pltpu.VMEM((rows, 1), F32), pltpu.VMEM((rows, 1), F32),
                            pltpu.VMEM((rows, d), F32)]),
        out_shape=jax.ShapeDtypeStruct((n_seq, t, d), F32),
        compiler_params=_cparams("parallel", "arbitrary"),
        name="fox_sample",
    )(page_table, q, cq_col, ckp_rows, ckn_rows, k_new, v_new, *([cache_k] * n_grp), *([cache_v] * n_grp))


def _largest_tile(n, cap):
    t = min(n, cap)
    while n % t:
        t //= 2
    return t


def _trunk(x3, mods, wkv0, shift0, past, w):
    n_seq, t, d = x3.shape
    n = n_seq * t
    depth = mods.shape[0]
    n_a = wkv0.shape[0]
    h = d // HEAD_DIM
    x = x3.reshape(n, d)
    per_seq = t >= 256
    if per_seq:
        tm_ffn, tm_mix = _largest_tile(t, 1024), _largest_tile(t, 256)
        chunk, tb = 64, _largest_tile(t, 256)
    else:
        tm_ffn = tm_mix = n
        chunk = tb = t

    def mod4(l, sub):
        m = mods[l, :, sub]
        if per_seq:
            return m.reshape(n_seq, 3, 1, d)
        return jnp.repeat(m, t, axis=0).transpose(1, 0, 2).reshape(1, 3, n, d)

    def tiles(tm):
        return max(t // tm, 1)

    v_first = None
    wkv_new, shift_new = [], []
    kv = None
    for l in range(depth):
        if l == n_a:
            k_new, v_new, kb, vb, logf = _shared_kv(x, w['g_kv'], w['w_kv'], w['w_f'], w['b_f'], tm_mix)
            if past is None:
                ident = jnp.arange(n // LANES, dtype=jnp.int32).reshape(n_seq, t // LANES)
                _, parts = _cumsum_pages(ident, logf.reshape(n // LANES, LANES, h), None, with_parts=True)
                parts = parts.reshape(n, LANES)
                kv = list(_kv_aug(kb, vb, parts, tm_mix)) + [parts]
            else:
                cache_k, cache_v, cache_logf, page_table = past
                ck_past, ck_new = _cumsum_pages(page_table, cache_logf, logf.reshape(n_seq, t, h))
                kv = (cache_k, cache_v, page_table, ck_past, ck_new, k_new.reshape(n_seq, t, d),
                      v_new.reshape(n_seq, t, d))
        gn = w['g_norm'][l]
        x = _ffn(x, mod4(l, 0), gn[0], w['w_ffn_in'][l, 0], w['w_ffn_out'][l, 0], tm_ffn, tiles(tm_ffn))
        m4 = mod4(l, 1)
        if l < n_a:
            p = {key: w[key + '_a'][l] for key in ('mu', 'w_rkv', 'w0', 'w1', 'w2', 'a0', 'a1', 'a2', 'g1', 'g2',
                                                   'kk', 'ka')}
            if l > 0:
                p.update({key: w[key + '_a'][l - 1] for key in ('v0', 'v1', 'v2')})
            hl = shift0[l].reshape(n_seq, 1, d) if per_seq else jnp.repeat(shift0[l], t, axis=0)
            r, lw, k, v, kk, a, gt, hmod = _rwkv_proj(x, m4, gn[1], hl, t, tm_mix, p, v_first if l > 0 else None)
            if l == 0:
                v_first = v
            o, s_new = _wkv(r, lw, k, v, kk, a, _pair_state(wkv0[l]), w['lnx_w_a'][l], w['lnx_b_a'][l],
                            w['rk_a'][l], t, chunk, tb, 2 if per_seq else h // PAIR)
            wkv_new.append(_unpair_state(s_new))
            shift_new.append(hmod.reshape(n_seq, t, d)[:, -1])
            x = _out_proj(o, gt, x, m4, gn[1], w['w_o_a'][l], tm_mix, tiles(tm_mix))
        else:
            j = l - n_a
            if past is None:
                qa = _q_aug(x, m4, gn[1], w['w_q_b'][j], kv[2], tm_mix, tiles(tm_mix))
                o = _flash_prompt(qa, kv[0], kv[1], n_seq, _largest_tile(t, 512), 2)
            else:
                q = _q_proj(x, m4, gn[1], w['w_q_b'][j], tm_mix, tiles(tm_mix))
                cache_k, cache_v, page_table, ck_past, ck_new, k3, v3 = kv
                o = _paged_attention(q.reshape(n_seq, t, d), cache_k, cache_v, page_table, ck_past, ck_new, k3, v3,
                                     _largest_tile(page_table.shape[1], 4)).reshape(n, d)
            x = _out_proj(o, None, x, m4, gn[1], w['w_o_b'][j], tm_mix, tiles(tm_mix))
        x = _ffn(x, mod4(l, 2), gn[2], w['w_ffn_in'][l, 1], w['w_ffn_out'][l, 1], tm_ffn, tiles(tm_ffn))
    return (x.reshape(n_seq, t, d), jnp.stack(wkv_new), jnp.stack(shift_new), k_new.reshape(n_seq, t, h, HEAD_DIM),
            v_new.reshape(n_seq, t, h, HEAD_DIM), logf.reshape(n_seq, t, h))


def kernel(x_prompt, x_sample, state_wkv, state_shift, cache_k, cache_v, cache_logf, page_table, c_prompt, c_sample,
           w_mod, b_mod, g_norm, w_ffn_in, w_ffn_out, mu_a, w_rkv_a, w_o_a, w0_a, w1_a, w2_a, a0_a, a1_a, a2_a,
           v0_a, v1_a, v2_a, g1_a, g2_a, kk_a, ka_a, rk_a, lnx_w_a, lnx_b_a, g_kv, w_kv, w_f, b_f, w_q_b, w_o_b):
    bp, _, d = x_prompt.shape
    bs = x_sample.shape[0]
    depth = w_mod.shape[0]
    n_a = state_wkv.shape[0]
    n_pool, page = cache_k.shape[:2]
    cast = lambda a: a.astype(BF16)
    w = dict(g_norm=g_norm, w_ffn_in=cast(w_ffn_in), w_ffn_out=cast(w_ffn_out), mu_a=mu_a, w_rkv_a=cast(w_rkv_a),
             w_o_a=cast(w_o_a), w0_a=w0_a, w1_a=cast(w1_a), w2_a=cast(w2_a), a0_a=a0_a, a1_a=cast(a1_a),
             a2_a=cast(a2_a), v0_a=v0_a, v1_a=cast(v1_a), v2_a=cast(v2_a), g1_a=cast(g1_a), g2_a=cast(g2_a),
             kk_a=kk_a, ka_a=ka_a, rk_a=rk_a.reshape(n_a, d), lnx_w_a=lnx_w_a, lnx_b_a=lnx_b_a, g_kv=g_kv,
             w_kv=cast(w_kv), w_f=cast(w_f), b_f=b_f, w_q_b=cast(w_q_b), w_o_b=cast(w_o_b))

    mods = _mods(jnp.concatenate([c_prompt, c_sample], axis=0), w_mod, b_mod)
    mods = mods.reshape(depth, bp + bs, 3, 3, d)

    n_heads = d // HEAD_DIM
    zeros_wkv = jnp.zeros((n_a, bp, n_heads, HEAD_DIM, HEAD_DIM), F32)
    zeros_shift = jnp.zeros((n_a, bp, d), F32)
    y_p, wkv_p, shift_p, k_p, v_p, logf_p = _trunk(x_prompt, mods[:, :bp], zeros_wkv, zeros_shift, None, w)
    past = (cache_k.reshape(n_pool, page, d), cache_v.reshape(n_pool, page, d), cache_logf, page_table)
    y_s, wkv_s, shift_s, k_s, v_s, logf_s = _trunk(x_sample, mods[:, bp:], state_wkv, state_shift, past, w)
    return (y_p, y_s, wkv_p, shift_p, k_p, v_p, logf_p, wkv_s, shift_s, k_s, v_s, logf_s)
```

```python
import functools

import jax
import jax.numpy as jnp
import numpy as np
from jax import lax
from jax.experimental import pallas as pl
from jax.experimental.pallas import tpu as pltpu

F32 = jnp.float32
BF16 = jnp.bfloat16

NORM_EPS = 1e-6
GN_EPS = 64e-5
MACARON_W = 0.5
NEG_INF = -1e30
HEAD_DIM = 64
LANES = 128
PAIR = LANES // HEAD_DIM
VMEM_LIMIT = 56 * 1024 * 1024

LOG2E = 1.4426950408889634
N_BIAS = 3

_NT = (((1,), (1,)), ((), ()))
_TN = (((0,), (0,)), ((), ()))


def _cparams(*sem):
    return pltpu.CompilerParams(dimension_semantics=sem, vmem_limit_bytes=VMEM_LIMIT)


def _dot(a, b):
    return jnp.dot(a.astype(BF16), b.astype(BF16), preferred_element_type=F32)


def _dot_nt(a, b):
    return lax.dot_general(a.astype(BF16), b.astype(BF16), _NT, preferred_element_type=F32)


def _dot_tn(a, b):
    return lax.dot_general(a.astype(BF16), b.astype(BF16), _TN, preferred_element_type=F32)


def _split3(x):
    hi = x.astype(BF16)
    r1 = x - hi.astype(F32)
    mid = r1.astype(BF16)
    lo = (r1 - mid.astype(F32)).astype(BF16)
    return hi, mid, lo


def _dot3(sel, x):
    hi, mid, lo = _split3(x)
    out = jnp.dot(sel, hi, preferred_element_type=F32)
    out = out + jnp.dot(sel, mid, preferred_element_type=F32)
    return out + jnp.dot(sel, lo, preferred_element_type=F32)


def _dot3_rhs(x, sel):
    hi, mid, lo = _split3(x)
    out = jnp.dot(hi, sel, preferred_element_type=F32)
    out = out + jnp.dot(mid, sel, preferred_element_type=F32)
    return out + jnp.dot(lo, sel, preferred_element_type=F32)


def _rms(x, g):
    return x * lax.rsqrt(jnp.mean(x * x, axis=-1, keepdims=True) + NORM_EPS) * g


def _sigmoid(x):
    return 1.0 / (1.0 + jnp.exp(-x))


def _silu(x):
    return x * _sigmoid(x)


def _head_ones(n):
    r = lax.broadcasted_iota(jnp.int32, (n, n), 0) // HEAD_DIM
    c = lax.broadcasted_iota(jnp.int32, (n, n), 1) // HEAD_DIM
    return jnp.where(r == c, 1.0, 0.0).astype(BF16)


def _head_sum(x, ones):
    n = ones.shape[0]
    hi = x.astype(BF16)
    lo = (x - hi.astype(F32)).astype(BF16)
    cols = []
    for i in range(0, x.shape[1], n):
        s = jnp.dot(hi[:, i:i + n], ones, preferred_element_type=F32)
        cols.append(s + jnp.dot(lo[:, i:i + n], ones, preferred_element_type=F32))
    return cols[0] if len(cols) == 1 else jnp.concatenate(cols, axis=1)


def _mods_kernel(c_ref, w_ref, b_ref, o_ref):
    o_ref[...] = _dot(_silu(c_ref[...]), w_ref[...]) + b_ref[...]


def _mods(c_all, w_mod, b_mod):
    depth, d, n9 = w_mod.shape
    nb = c_all.shape[0]
    tn = d
    return pl.pallas_call(
        _mods_kernel,
        grid=(depth, n9 // tn),
        in_specs=[
            pl.BlockSpec((nb, d), lambda l, j: (0, 0)),
            pl.BlockSpec((None, d, tn), lambda l, j: (l, 0, j)),
            pl.BlockSpec((None, 1, tn), lambda l, j: (l, 0, j)),
        ],
        out_specs=pl.BlockSpec((None, nb, tn), lambda l, j: (l, 0, j)),
        out_shape=jax.ShapeDtypeStruct((depth, nb, n9), F32),
        compiler_params=_cparams("arbitrary", "arbitrary"),
        name="mods",
    )(c_all, w_mod, b_mod.reshape(depth, 1, n9))


def _mod_spec(m4, tm, tiles_per_seq):
    _, _, rows, d = m4.shape
    if rows == 1:
        return pl.BlockSpec((None, 3, 1, d), lambda i, *_: (i // tiles_per_seq, 0, 0, 0))
    return pl.BlockSpec((None, 3, tm, d), lambda i, *_: (0, 0, i, 0))


def _ffn_kernel(x_ref, m_ref, g_ref, wg_ref, wu_ref, wo_ref, o_ref, h_scr, acc_scr):
    j = pl.program_id(1)

    @pl.when(j == 0)
    def _():
        h = _rms(x_ref[...], g_ref[0:1]) * (1.0 + m_ref[1]) + m_ref[0]
        h_scr[...] = h.astype(BF16)
        acc_scr[...] = jnp.zeros_like(acc_scr)

    h = h_scr[...]
    gate = jnp.dot(h, wg_ref[...], preferred_element_type=F32)
    up = jnp.dot(h, wu_ref[...], preferred_element_type=F32)
    acc_scr[...] += _dot(_silu(gate) * up, wo_ref[...])

    @pl.when(j == pl.num_programs(1) - 1)
    def _():
        o_ref[...] = x_ref[...] + MACARON_W * m_ref[2] * _rms(acc_scr[...], g_ref[1:2])


def _ffn(x, m4, gn, w_in, w_out, tm, tiles_per_seq):
    n, d = x.shape
    f = w_out.shape[0]
    tf = 256 if f % 256 == 0 else LANES
    nf = f // tf
    return pl.pallas_call(
        _ffn_kernel,
        grid=(n // tm, nf),
        in_specs=[
            pl.BlockSpec((tm, d), lambda i, j: (i, 0)),
            _mod_spec(m4, tm, tiles_per_seq),
            pl.BlockSpec((2, d), lambda i, j: (0, 0)),
            pl.BlockSpec((d, tf), lambda i, j: (0, j)),
            pl.BlockSpec((d, tf), lambda i, j: (0, j + nf)),
            pl.BlockSpec((tf, d), lambda i, j: (j, 0)),
        ],
        out_specs=pl.BlockSpec((tm, d), lambda i, j: (i, 0)),
        out_shape=jax.ShapeDtypeStruct((n, d), F32),
        scratch_shapes=[pltpu.VMEM((tm, d), BF16), pltpu.VMEM((tm, d), F32)],
        compiler_params=_cparams("parallel", "arbitrary"),
        name="ffn",
    )(x, m4, gn, w_in, w_in, w_out)


def _rwkv_proj_kernel(has_vfirst, seq_len, x_ref, m_ref, g_ref, hl_ref, mu_ref, wrkv_ref, w0_ref, w1_ref,
                      w2_ref, a0_ref, a1_ref, a2_ref, g1_ref, g2_ref, kka_ref, kaa_ref, *rest):
    if has_vfirst:
        v0_ref, v1_ref, v2_ref, vf_ref = rest[:4]
        rest = rest[4:]
    r_ref, lw_ref, k_ref, v_ref, kk_ref, a_ref, gt_ref, h_ref, carry_scr = rest
    tm, d = x_ref.shape
    i = pl.program_id(0)

    h = _rms(x_ref[...], g_ref[0:1]) * (1.0 + m_ref[1]) + m_ref[0]
    h_ref[...] = h
    row = lax.broadcasted_iota(jnp.int32, (tm, d), 0)
    rolled = pltpu.roll(h, 1, 0)
    if hl_ref.shape[0] == 1:
        tiles_per_seq = seq_len // tm
        first = jnp.where(i % tiles_per_seq == 0, hl_ref[...], carry_scr[...])
        h_prev = jnp.where(row == 0, first, rolled)
        carry_scr[...] = h[tm - 1:tm, :]
    else:
        h_prev = jnp.where(row % seq_len == 0, hl_ref[...], rolled)
    xx = h_prev - h
    xr, xw, xk, xv, xa, xg = [h + xx * mu_ref[j:j + 1] for j in range(6)]

    r = _dot(xr, wrkv_ref[0])
    k = _dot(xk, wrkv_ref[1])
    v = _dot(xv, wrkv_ref[2])
    z = w0_ref[...] + _dot(jnp.tanh(_dot(xw, w1_ref[...])), w2_ref[...])
    lw = -_sigmoid(z) * jnp.exp(F32(-0.5))
    a = _sigmoid(a0_ref[...] + _dot(_dot(xa, a1_ref[...]), a2_ref[...]))
    gt = _dot(_sigmoid(_dot(xg, g1_ref[...])), g2_ref[...])
    if has_vfirst:
        mix = _sigmoid(v0_ref[...] + _dot(_dot(xv, v1_ref[...]), v2_ref[...]))
        v = v + (vf_ref[...] - v) * mix
    kk = k * kka_ref[...]
    ones = _head_ones(min(d, 2 * LANES))
    kk = kk * lax.rsqrt(jnp.maximum(_head_sum(kk * kk, ones), 1e-24))
    k = k * (1.0 + (a - 1.0) * kaa_ref[...])

    r_ref[...] = r
    lw_ref[...] = lw
    k_ref[...] = k
    v_ref[...] = v
    kk_ref[...] = kk
    a_ref[...] = a
    gt_ref[...] = gt


def _rwkv_proj(x, m4, gn, hl, seq_len, tm, p, v_first):
    n, d = x.shape
    tiles_per_seq = max(seq_len // tm, 1)
    has_vfirst = v_first is not None
    row = lambda a: a.reshape(1, d)
    const = lambda a: pl.BlockSpec(a.shape, lambda i: (0,) * a.ndim)
    tok = pl.BlockSpec((tm, d), lambda i: (i, 0))
    if hl.ndim == 3:
        hl_spec = pl.BlockSpec((None, 1, d), lambda i: (i // tiles_per_seq, 0, 0))
    else:
        hl_spec = tok
    args = [x, m4, gn, hl, p['mu'], p['w_rkv'], row(p['w0']), p['w1'], p['w2'], row(p['a0']), p['a1'], p['a2'],
            p['g1'], p['g2'], row(p['kk']), row(p['ka'])]
    specs = [tok, _mod_spec(m4, tm, tiles_per_seq), const(gn), hl_spec] + [const(a) for a in args[4:]]
    if has_vfirst:
        extra = [row(p['v0']), p['v1'], p['v2']]
        args += extra + [v_first]
        specs += [const(a) for a in extra] + [tok]
    return pl.pallas_call(
        functools.partial(_rwkv_proj_kernel, has_vfirst, seq_len),
        grid=(n // tm,),
        in_specs=specs,
        out_specs=[tok] * 8,
        out_shape=[jax.ShapeDtypeStruct((n, d), F32)] * 8,
        scratch_shapes=[pltpu.VMEM((1, d), F32)],
        compiler_params=_cparams("arbitrary"),
        name="rwkv_proj",
    )(*args)


LONG_CHUNK = 64


def _long_masks():
    n2 = PAIR * LONG_CHUNK
    r = np.arange(n2)[:, None]
    c = np.arange(n2)[None, :]
    masks = [c < r, c <= r, c == r]
    b = 1
    while b < LONG_CHUNK:
        masks.append((r // (2 * b) == c // (2 * b)) & ((r // b) % 2 == 1) & ((c // b) % 2 == 0))
        b *= 2
    return jnp.asarray(np.stack(masks).astype(np.float32))


def _wkv_long_kernel(r_ref, lw_ref, k_ref, v_ref, kk_ref, a_ref, s0_ref, lnw_ref, lnb_ref, rk_ref, mask_ref,
                     o_ref, sout_ref, st_scr, wq_scr, bk_scr, lkk_scr, v2_scr, kt_scr, bt_scr, lrb_scr, lvt_scr,
                     x_scr, gc_scr, lab_scr, tinv_scr, ol_scr, ds_scr, ut_scr):
    c = LONG_CHUNK
    n2 = PAIR * c
    tb, d = r_ref.shape
    n_pairs = d // LANES
    t_idx = pl.program_id(1)

    @pl.when(t_idx == 0)
    def _():
        st_scr[...] = s0_ref[...]

    rows_in = min(tb, c)
    n_chunks = max(tb // c, 1)
    items = [(ic, p) for ic in range(n_chunks) for p in range(n_pairs)]
    dotf = lambda x, y: jnp.dot(x, y, preferred_element_type=F32)
    lane = lax.broadcasted_iota(jnp.int32, (c, LANES), 1)
    head_lane = [lane // HEAD_DIM == h for h in range(PAIR)]
    tri = jnp.where(lax.broadcasted_iota(jnp.int32, (c, c), 1) <= lax.broadcasted_iota(jnp.int32, (c, c), 0),
                    1.0, 0.0).astype(BF16)

    def stack(x):
        return jnp.concatenate([jnp.where(m, x, 0.0) for m in head_lane], axis=0)

    for i, (ic, p) in enumerate(items):
        rows, ls = slice(ic * c, ic * c + rows_in), slice(p * LANES, (p + 1) * LANES)
        rc, lwc, kc, vc, kkc, ac = [ref[rows, ls] for ref in (r_ref, lw_ref, k_ref, v_ref, kk_ref, a_ref)]
        if rows_in < c:
            pad = jnp.zeros((c - rows_in, LANES), F32)
            rc, lwc, kc, vc, kkc, ac = [jnp.concatenate([x, pad], axis=0) for x in (rc, lwc, kc, vc, kkc, ac)]
        cum = _dot3(tri, lwc)
        g = jnp.exp(cum)
        gi = jnp.exp(-cum)
        g_end = g[c - 1:c, :]
        bb = ac * kkc * gi
        kb = kc * gi
        wq_scr[i, :n2] = stack(-kkc * jnp.exp(cum - lwc)).astype(BF16)
        wq_scr[i, n2:] = stack(rc * g).astype(BF16)
        bk_scr[i, :n2] = stack(bb).astype(BF16)
        bk_scr[i, n2:] = stack(kb).astype(BF16)
        v2_scr[i] = stack(vc).astype(BF16)
        kt_scr[i] = stack(kb * g_end).astype(BF16)
        bt_scr[i] = stack(bb * g_end).T.astype(BF16)
        gc_scr[i] = jnp.broadcast_to(g_end, (LANES, LANES)).T

    for i in range(len(items)):
        scores = lax.dot_general(wq_scr[i], bk_scr[i], _NT, preferred_element_type=F32)
        l_ab = scores[:n2, :n2] * mask_ref[0]
        lab_scr[i] = l_ab
        tinv_scr[i] = mask_ref[2] + l_ab * mask_ref[3]
        lkk_scr[i, :n2] = (scores[:n2, n2:] * mask_ref[0]).astype(BF16)
        lkk_scr[i, n2:] = (scores[n2:, n2:] * mask_ref[1]).astype(BF16)
        lrb_scr[i] = (scores[n2:, :n2] * mask_ref[1]).astype(BF16)

    for i in range(len(items)):
        lv = dotf(lkk_scr[i], v2_scr[i])
        lvt_scr[i] = lv[:n2].astype(BF16)
        ol_scr[i] = lv[n2:]
        ds_scr[i] = lax.dot_general(kt_scr[i], v2_scr[i], _TN, preferred_element_type=F32)

    for level in range(4, mask_ref.shape[0]):
        for i in range(len(items)):
            x_scr[i] = dotf((lab_scr[i] * mask_ref[level]).astype(BF16), tinv_scr[i].astype(BF16)).astype(BF16)
        for i in range(len(items)):
            tinv = tinv_scr[i]
            tinv_scr[i] = tinv + dotf(tinv.astype(BF16), x_scr[i])

    for i in range(len(items)):
        tinv = tinv_scr[i].astype(BF16)
        ut_scr[i] = dotf(tinv, lvt_scr[i])
        wq_scr[i, :n2] = dotf(tinv, wq_scr[i, :n2]).astype(BF16)

    def advance(ic, carry):
        rows = pl.ds(pl.multiple_of(ic * c, c), rows_in)
        for p in range(n_pairs):
            i = ic * n_pairs + p
            st = st_scr[p]
            ws = dotf(wq_scr[i], st.astype(BF16))
            u2 = (ws[:n2] + ut_scr[i]).astype(BF16)
            o2 = ws[n2:] + dotf(lrb_scr[i], u2) + ol_scr[i]
            st_scr[p] = gc_scr[i] * st + dotf(bt_scr[i], u2) + ds_scr[i]
            o = o2[0:c]
            for h in range(1, PAIR):
                o = o + o2[h * c:(h + 1) * c]
            o_ref[rows, p * LANES:(p + 1) * LANES] = o[:rows_in]
        return carry

    lax.fori_loop(0, n_chunks, advance, 0)

    ones = _head_ones(min(d, 2 * LANES))
    o = o_ref[...]
    cen = o - _head_sum(o, ones) * (1.0 / HEAD_DIM)
    var = _head_sum(cen * cen, ones) * (1.0 / HEAD_DIM)
    bonus = _head_sum(r_ref[...] * k_ref[...] * rk_ref[...], ones) * v_ref[...]
    o_ref[...] = cen * lax.rsqrt(var + GN_EPS) * lnw_ref[...] + lnb_ref[...] + bonus

    @pl.when(t_idx == pl.num_programs(1) - 1)
    def _():
        sout_ref[...] = st_scr[...]


def _wkv_long(r, lw, k, v, kk, a, s0t, lnw, lnb, rk, seq_len, tb):
    n, d = r.shape
    n_seq = n // seq_len
    n_pairs = d // LANES
    nt = seq_len // tb
    n_items = max(tb // LONG_CHUNK, 1) * n_pairs
    n2 = PAIR * LONG_CHUNK
    assert n2 == LANES
    masks = _long_masks()
    tok = pl.BlockSpec((tb, d), lambda b, t: (b * nt + t, 0))
    vec = pl.BlockSpec((1, d), lambda b, t: (0, 0))
    st = pl.BlockSpec((None, n_pairs, LANES, LANES), lambda b, t: (b, 0, 0, 0))
    per = lambda rows, dt: pltpu.VMEM((n_items, rows, LANES), dt)
    return pl.pallas_call(
        _wkv_long_kernel,
        grid=(n_seq, nt),
        in_specs=[tok] * 6 + [st, vec, vec, vec, pl.BlockSpec(masks.shape, lambda b, t: (0, 0, 0))],
        out_specs=[tok, st],
        out_shape=[jax.ShapeDtypeStruct((n, d), F32), jax.ShapeDtypeStruct(s0t.shape, F32)],
        scratch_shapes=[pltpu.VMEM((n_pairs, LANES, LANES), F32)]
        + [per(2 * n2, BF16)] * 3 + [per(n2, BF16)] * 6 + [per(n2, F32)] * 6,
        compiler_params=_cparams("parallel", "arbitrary"),
        name="wkv_long",
    )(r, lw, k, v, kk, a, s0t, lnw.reshape(1, d), lnb.reshape(1, d), rk.reshape(1, d), masks)


def _pair_state(s):
    n_seq, h, n, _ = s.shape
    s = s.reshape(n_seq, h // PAIR, PAIR, n, n)
    eye = jnp.eye(PAIR, dtype=s.dtype)
    out = s[:, :, :, :, None, :] * eye[None, None, :, None, :, None]
    return out.reshape(n_seq, h // PAIR, PAIR * n, PAIR * n)


def _unpair_state(sp):
    n_seq, hp, n2, _ = sp.shape
    n = n2 // PAIR
    s = sp.reshape(n_seq, hp, PAIR, n, PAIR, n)
    return jnp.stack([s[:, :, a, :, a, :] for a in range(PAIR)], axis=2).reshape(n_seq, hp * PAIR, n, n)


def _out_kernel(has_gate, res_w, *refs):
    if has_gate:
        a_ref, gt_ref, x_ref, m_ref, g_ref, w_ref, o_ref = refs
        a = a_ref[...] * gt_ref[...]
    else:
        a_ref, x_ref, m_ref, g_ref, w_ref, o_ref = refs
        a = a_ref[...]
    y = _dot(a, w_ref[...])
    o_ref[...] = x_ref[...] + res_w * m_ref[2] * _rms(y, g_ref[1:2])


def _out_proj(a, gate, x, m4, gn, w, tm, tiles_per_seq, res_w=1.0):
    n, d = x.shape
    tok = pl.BlockSpec((tm, d), lambda i: (i, 0))
    args = [a] + ([gate] if gate is not None else []) + [x, m4, gn, w]
    specs = [tok] * (len(args) - 3) + [_mod_spec(m4, tm, tiles_per_seq), pl.BlockSpec((2, d), lambda i: (0, 0)),
                                       pl.BlockSpec(w.shape, lambda i: (0, 0))]
    return pl.pallas_call(
        functools.partial(_out_kernel, gate is not None, res_w),
        grid=(n // tm,),
        in_specs=specs,
        out_specs=tok,
        out_shape=jax.ShapeDtypeStruct((n, d), F32),
        compiler_params=_cparams("parallel"),
        name="out_proj",
    )(*args)


def _kv_kernel(x_ref, g_ref, wkv_ref, wf_ref, bf_ref, k_ref, v_ref, kb_ref, vb_ref, lf_ref):
    d = x_ref.shape[1]
    s = _rms(x_ref[...], g_ref[...])
    kv = _dot(s, wkv_ref[...])
    k = kv[:, :d]
    v = kv[:, d:]
    k_ref[...] = k
    v_ref[...] = v
    kb_ref[...] = k.astype(BF16)
    vb_ref[...] = v.astype(BF16)
    z = _dot(s, wf_ref[...]) + bf_ref[...]
    lf_ref[...] = jnp.minimum(z, 0.0) - jnp.log(1.0 + jnp.exp(-jnp.abs(z)))


def _shared_kv(x, g_kv, w_kv, w_f, b_f, tm):
    n, d = x.shape
    h = w_f.shape[1]
    tok = pl.BlockSpec((tm, d), lambda i: (i, 0))
    return pl.pallas_call(
        _kv_kernel,
        grid=(n // tm,),
        in_specs=[tok, pl.BlockSpec((1, d), lambda i: (0, 0)), pl.BlockSpec(w_kv.shape, lambda i: (0, 0)),
                  pl.BlockSpec(w_f.shape, lambda i: (0, 0)), pl.BlockSpec((1, h), lambda i: (0, 0))],
        out_specs=[tok, tok, tok, tok, pl.BlockSpec((tm, h), lambda i: (i, 0))],
        out_shape=[jax.ShapeDtypeStruct((n, d), F32)] * 2 + [jax.ShapeDtypeStruct((n, d), BF16)] * 2
        + [jax.ShapeDtypeStruct((n, h), F32)],
        compiler_params=_cparams("parallel"),
        name="shared_kv",
    )(x, g_kv.reshape(1, d), w_kv, w_f, b_f.reshape(1, h))


def _cumsum_kernel(n_grp, has_new, has_parts, pt_ref, *refs):
    lf_refs = refs[:n_grp]
    rest = refs[n_grp:]
    if has_new:
        new_ref, ck_ref, cknew_ref, carry_scr = rest
    elif has_parts:
        ck_ref, parts_ref, carry_scr = rest
    else:
        ck_ref, carry_scr = rest
    del pt_ref
    j = pl.program_id(1)
    page, h = lf_refs[0].shape

    @pl.when(j == 0)
    def _():
        carry_scr[...] = jnp.zeros_like(carry_scr)

    ri = lax.broadcasted_iota(jnp.int32, (page, page), 0)
    ci = lax.broadcasted_iota(jnp.int32, (page, page), 1)
    tri = jnp.where(ci <= ri, 1.0, 0.0).astype(BF16)
    local = [_dot3(tri, lf[...]) for lf in lf_refs]
    base = carry_scr[...]
    for i in range(n_grp):
        ck = base + local[i]
        ck_ref[i * page:(i + 1) * page, :] = ck
        base = ck[page - 1:page, :]

        if has_parts:
            src_h = lax.broadcasted_iota(jnp.int32, (h, LANES), 0)
            dst = lax.broadcasted_iota(jnp.int32, (h, LANES), 1)
            out = jnp.where(lax.broadcasted_iota(jnp.int32, (page, LANES), 1) // h == N_BIAS, 1.0, 0.0)
            for x, part in enumerate(_split3(ck * LOG2E)):
                out = out + jnp.dot(part, jnp.where(dst == x * h + src_h, 1.0, 0.0).astype(BF16),
                                    preferred_element_type=F32)
            parts_ref[i * page:(i + 1) * page, :] = out.astype(BF16)
    carry_scr[...] = base

    if has_new:
        @pl.when(j == pl.num_programs(1) - 1)
        def _():
            t = new_ref.shape[0]
            cknew_ref[...] = base + _dot3(tri[:t, :t], new_ref[...])


def _cumsum_pages(page_table, lf_pages, lf_new, with_parts=False):
    n_seq, n_pages = page_table.shape
    _, page, h = lf_pages.shape
    has_new = lf_new is not None
    n_grp = _largest_tile(n_pages, 8)

    def page_spec(i):
        return pl.BlockSpec((None, page, h), lambda b, j, pt: (pt[b, j * n_grp + i], 0, 0))

    in_specs = [page_spec(i) for i in range(n_grp)]
    out_specs = [pl.BlockSpec((None, n_grp * page, h), lambda b, j, pt: (b, j, 0))]
    out_shape = [jax.ShapeDtypeStruct((n_seq, n_pages * page, h), F32)]
    args = [lf_pages] * n_grp
    if has_new:
        t = lf_new.shape[1]
        in_specs.append(pl.BlockSpec((None, t, h), lambda b, j, pt: (b, 0, 0)))
        out_specs.append(pl.BlockSpec((None, t, h), lambda b, j, pt: (b, 0, 0)))
        out_shape.append(jax.ShapeDtypeStruct((n_seq, t, h), F32))
        args.append(lf_new)
    elif with_parts:
        out_specs.append(pl.BlockSpec((None, n_grp * page, LANES), lambda b, j, pt: (b, j, 0)))
        out_shape.append(jax.ShapeDtypeStruct((n_seq, n_pages * page, LANES), BF16))
    return pl.pallas_call(
        functools.partial(_cumsum_kernel, n_grp, has_new, with_parts and not has_new),
        grid_spec=pltpu.PrefetchScalarGridSpec(
            num_scalar_prefetch=1, grid=(n_seq, n_pages // n_grp), in_specs=in_specs, out_specs=out_specs,
            scratch_shapes=[pltpu.VMEM((1, h), F32)]),
        out_shape=out_shape,
        compiler_params=_cparams("parallel", "arbitrary"),
        name="cumsum",
    )(page_table, *args)


def _q_kernel(x_ref, m_ref, g_ref, w_ref, q_ref):
    h = _rms(x_ref[...], g_ref[0:1]) * (1.0 + m_ref[1]) + m_ref[0]
    q_ref[...] = (_dot(h, w_ref[...]) * (HEAD_DIM ** -0.5)).astype(BF16)


def _q_proj(x, m4, gn, w, tm, tiles_per_seq):
    n, d = x.shape
    tok = pl.BlockSpec((tm, d), lambda i: (i, 0))
    return pl.pallas_call(
        _q_kernel,
        grid=(n // tm,),
        in_specs=[tok, _mod_spec(m4, tm, tiles_per_seq), pl.BlockSpec((2, d), lambda i: (0, 0)),
                  pl.BlockSpec(w.shape, lambda i: (0, 0))],
        out_specs=tok,
        out_shape=jax.ShapeDtypeStruct((n, d), BF16),
        compiler_params=_cparams("parallel"),
        name="q_proj",
    )(x, m4, gn, w)


def _pair_to_tiles():
    m = np.zeros((LANES, PAIR * LANES), np.float32)
    r = np.arange(LANES)
    m[r, (r // HEAD_DIM) * LANES + r % HEAD_DIM] = 1.0
    return jnp.asarray(m, BF16)


def _bias_place(h, for_keys):
    m = np.zeros((LANES, h * LANES), np.float32)
    for hh in range(h):
        for x in range(N_BIAS):
            if for_keys:
                m[N_BIAS * h + hh, hh * LANES + HEAD_DIM + x] = 1.0
                m[x * h + hh, hh * LANES + HEAD_DIM + N_BIAS + x] = -1.0
            else:
                m[x * h + hh, hh * LANES + HEAD_DIM + x] = 1.0
                m[N_BIAS * h + hh, hh * LANES + HEAD_DIM + N_BIAS + x] = 1.0
    return jnp.asarray(m, BF16)


def _to_head_tiles(x, place, extra):
    cols = []
    for p in range(x.shape[1] // LANES):
        sl = slice(p * PAIR * LANES, (p + 1) * PAIR * LANES)
        cols.append(jnp.dot(x[:, p * LANES:(p + 1) * LANES], place, preferred_element_type=F32) + extra[:, sl])
    return jnp.concatenate(cols, axis=1)


def _q_aug_kernel(x_ref, m_ref, g_ref, w_ref, parts_ref, place_ref, bias_ref, q_ref):
    h = _rms(x_ref[...], g_ref[0:1]) * (1.0 + m_ref[1]) + m_ref[0]
    q = (_dot(h, w_ref[...]) * (HEAD_DIM ** -0.5 * LOG2E)).astype(BF16)
    bias = jnp.dot(parts_ref[...], bias_ref[...], preferred_element_type=F32)
    q_ref[...] = _to_head_tiles(q, place_ref[...], bias).astype(BF16)


def _q_aug(x, m4, gn, w, parts, tm, tiles_per_seq):
    n, d = x.shape
    tok = pl.BlockSpec((tm, d), lambda i: (i, 0))
    place, bias = _pair_to_tiles(), _bias_place(d // HEAD_DIM, False)
    const = lambda a: pl.BlockSpec(a.shape, lambda i: (0, 0))
    return pl.pallas_call(
        _q_aug_kernel,
        grid=(n // tm,),
        in_specs=[tok, _mod_spec(m4, tm, tiles_per_seq), pl.BlockSpec((2, d), lambda i: (0, 0)), const(w),
                  pl.BlockSpec((tm, LANES), lambda i: (i, 0)), const(place), const(bias)],
        out_specs=pl.BlockSpec((tm, PAIR * d), lambda i: (i, 0)),
        out_shape=jax.ShapeDtypeStruct((n, PAIR * d), BF16),
        compiler_params=_cparams("parallel"),
        name="q_aug",
    )(x, m4, gn, w, parts, place, bias)


def _kv_aug_kernel(k_ref, v_ref, parts_ref, place_ref, bias_ref, ka_ref, va_ref):
    tm, d2 = ka_ref.shape
    bias = jnp.dot(parts_ref[...], bias_ref[...], preferred_element_type=F32)
    ka_ref[...] = _to_head_tiles(k_ref[...], place_ref[...], bias).astype(BF16)
    ones = jnp.where(lax.broadcasted_iota(jnp.int32, (1, d2), 1) % LANES >= HEAD_DIM, 1.0, 0.0)
    va_ref[...] = _to_head_tiles(v_ref[...], place_ref[...], ones).astype(BF16)


def _kv_aug(kb, vb, parts, tm):
    n, d = kb.shape
    tok = pl.BlockSpec((tm, d), lambda i: (i, 0))
    wide = pl.BlockSpec((tm, PAIR * d), lambda i: (i, 0))
    place, bias = _pair_to_tiles(), _bias_place(d // HEAD_DIM, True)
    const = lambda a: pl.BlockSpec(a.shape, lambda i: (0, 0))
    return pl.pallas_call(
        _kv_aug_kernel,
        grid=(n // tm,),
        in_specs=[tok, tok, pl.BlockSpec((tm, LANES), lambda i: (i, 0)), const(place), const(bias)],
        out_specs=[wide, wide],
        out_shape=[jax.ShapeDtypeStruct((n, PAIR * d), BF16)] * 2,
        compiler_params=_cparams("parallel"),
        name="kv_aug",
    )(kb, vb, parts, place, bias)


def _flash_kernel(qi_ref, kj_ref, q_ref, k_ref, v_ref, o_ref, m_scr, acc_scr):
    n = pl.program_id(2)
    qi = qi_ref[n]
    kj = kj_ref[n]
    tq = q_ref.shape[0]
    tk = k_ref.shape[0]
    n_heads = q_ref.shape[1] // LANES

    @pl.when(kj == 0)
    def _():
        m_scr[...] = jnp.full_like(m_scr, NEG_INF)
        acc_scr[...] = jnp.zeros_like(acc_scr)

    def update(masked):
        if masked:
            valid = (lax.broadcasted_iota(jnp.int32, (tq, tk), 1)
                     <= lax.broadcasted_iota(jnp.int32, (tq, tk), 0))
        for h in range(n_heads):
            sl = slice(h * LANES, (h + 1) * LANES)
            s = lax.dot_general(q_ref[:, sl], k_ref[:, sl], _NT, preferred_element_type=F32)
            if masked:
                s = jnp.where(valid, s, NEG_INF)
            cols = [s[:, c * LANES:(c + 1) * LANES] for c in range(tk // LANES)]
            m_cur = cols[0]
            for col in cols[1:]:
                m_cur = jnp.maximum(m_cur, col)
            m_old = m_scr[h]
            m_new = jnp.maximum(m_old, jnp.max(m_cur, axis=1, keepdims=True))
            p = jnp.concatenate([jnp.exp2(col - m_new) for col in cols], axis=1).astype(BF16)
            acc_scr[h] = jnp.exp2(m_old - m_new) * acc_scr[h] + jnp.dot(p, v_ref[:, sl],
                                                                        preferred_element_type=F32)
            m_scr[h] = m_new

    @pl.when(kj < qi)
    def _():
        update(False)

    @pl.when(kj == qi)
    def _():
        update(True)
        lane = lax.broadcasted_iota(jnp.int32, (tq, LANES), 1)
        outs = []
        for h in range(n_heads):
            acc = acc_scr[h]
            outs.append(acc / pltpu.roll(acc, HEAD_DIM, 1))
        for p in range(n_heads // PAIR):
            o_ref[:, p * LANES:(p + 1) * LANES] = jnp.where(
                lane < HEAD_DIM, outs[PAIR * p], pltpu.roll(outs[PAIR * p + 1], HEAD_DIM, 1))


def _flash_prompt(qa, ka, va, n_seq, tq, heads_per_step):
    n, d2 = qa.shape
    t = n // n_seq
    nq = t // tq
    width = heads_per_step * LANES
    qi = jnp.asarray([i for i in range(nq) for _ in range(i + 1)], jnp.int32)
    kj = jnp.asarray([j for i in range(nq) for j in range(i + 1)], jnp.int32)
    return pl.pallas_call(
        _flash_kernel,
        grid_spec=pltpu.PrefetchScalarGridSpec(
            num_scalar_prefetch=2,
            grid=(n_seq, d2 // width, int(qi.shape[0])),
            in_specs=[
                pl.BlockSpec((tq, width), lambda b, p, s, qi, kj: (b * nq + qi[s], p)),
                pl.BlockSpec((tq, width), lambda b, p, s, qi, kj: (b * nq + kj[s], p)),
                pl.BlockSpec((tq, width), lambda b, p, s, qi, kj: (b * nq + kj[s], p)),
            ],
            out_specs=pl.BlockSpec((tq, width // PAIR), lambda b, p, s, qi, kj: (b * nq + qi[s], p)),
            scratch_shapes=[pltpu.VMEM((heads_per_step, tq, LANES), F32),
                            pltpu.VMEM((heads_per_step, tq, LANES), F32)]),
        out_shape=jax.ShapeDtypeStruct((n, d2 // PAIR), F32),
        compiler_params=_cparams("parallel", "parallel", "arbitrary"),
        name="fox_prompt",
    )(qi, kj, qa, ka, va)


def _paged_kernel(n_grp, gathered, pt_ref, q_ref, cq_ref, ckp_ref, ckn_ref, kn_ref, vn_ref, *rest):
    t, d = q_ref.shape
    n_heads = d // HEAD_DIM
    rows = n_heads * t
    if gathered:
        kd_ref, vd_ref, o_ref, qbd_scr, m_scr, l_scr, acc_scr = rest
        page = kd_ref.shape[0] // n_grp
    else:
        k_refs = rest[:n_grp]
        v_refs = rest[n_grp:2 * n_grp]
        o_ref, kd_ref, vd_ref, qbd_scr, m_scr, l_scr, acc_scr = rest[2 * n_grp:]
        page = k_refs[0].shape[0] // n_heads
    del pt_ref
    g = pl.program_id(1)

    row_head = lax.broadcasted_iota(jnp.int32, (rows, d), 0) // t
    lane_head = lax.broadcasted_iota(jnp.int32, (rows, d), 1) // HEAD_DIM
    expand = jnp.where(lax.broadcasted_iota(jnp.int32, (rows, n_heads), 0) // t
                       == lax.broadcasted_iota(jnp.int32, (rows, n_heads), 1), 1.0, 0.0).astype(BF16)

    @pl.when(g == 0)
    def _():
        q = q_ref[...].astype(F32)
        q_rows = jnp.concatenate([q] * n_heads, axis=0)
        qbd_scr[...] = jnp.where(row_head == lane_head, q_rows, 0.0).astype(BF16)
        m_scr[...] = jnp.full_like(m_scr, NEG_INF)
        l_scr[...] = jnp.zeros_like(l_scr)
        acc_scr[...] = jnp.zeros_like(acc_scr)

    qbd = qbd_scr[...]
    cq = cq_ref[...]

    def attend(s, vs):
        m_old = m_scr[...]
        m_new = jnp.maximum(m_old, jnp.max(s, axis=1, keepdims=True))
        alpha = jnp.exp(m_old - m_new)
        p = jnp.exp(s - m_new)
        l_scr[...] = alpha * l_scr[...] + jnp.sum(p, axis=1, keepdims=True)
        pv = None
        for i, vpage in enumerate(vs):
            part = _dot(p[:, i * page:(i + 1) * page], vpage)
            pv = part if pv is None else pv + part
        acc_scr[...] = alpha * acc_scr[...] + pv
        m_scr[...] = m_new

    def dense(ref):
        heads = [ref[pl.ds(hh, page, stride=n_heads), :] for hh in range(n_heads)]
        return jnp.concatenate(heads, axis=1).astype(BF16)

    if gathered:
        ks = [kd_ref[i * page:(i + 1) * page, :] for i in range(n_grp)]
        vs = [vd_ref[i * page:(i + 1) * page, :] for i in range(n_grp)]
    else:
        ks = [dense(kr) for kr in k_refs]
        vs = [dense(vr) for vr in v_refs]
        for i in range(n_grp):
            kd_ref[i * page:(i + 1) * page, :] = ks[i]
            vd_ref[i * page:(i + 1) * page, :] = vs[i]
    s = jnp.concatenate([_dot_nt(qbd, kp) for kp in ks], axis=1)
    s = s + (cq - _dot3(expand, ckp_ref[...]))
    attend(s, vs)

    @pl.when(g == pl.num_programs(1) - 1)
    def _():
        pad = jnp.zeros((page - t, d), F32)
        k_new = jnp.concatenate([kn_ref[...], pad], axis=0)
        v_new = jnp.concatenate([vn_ref[...], pad], axis=0)
        s_new = _dot_nt(qbd, k_new) + (cq - _dot3(expand, ckn_ref[...]))
        key = lax.broadcasted_iota(jnp.int32, (rows, page), 1)
        query = lax.broadcasted_iota(jnp.int32, (rows, page), 0) % t
        attend(jnp.where((key <= query) & (key < t), s_new, NEG_INF), [v_new])
        out = acc_scr[...] / l_scr[...]
        out = jnp.where(row_head == lane_head, out, 0.0).reshape(n_heads, t, d)
        o_ref[...] = jnp.sum(out, axis=0)


def _paged_attention(q, kv_src, page_table, page, ck_past, ck_new, k_new, v_new, n_grp):
    n_seq, t, d = q.shape
    n_pages = page_table.shape[1]
    h = d // HEAD_DIM
    rows = h * t
    assert n_pages % n_grp == 0
    gathered = kv_src[0].ndim == 3
    cq_col = ck_new.transpose(0, 2, 1).reshape(n_seq, rows, 1)
    ckp_rows = ck_past.transpose(0, 2, 1)
    ckn_rows = jnp.pad(ck_new.transpose(0, 2, 1), ((0, 0), (0, 0), (0, page - t)))
    per_seq = lambda shape: pl.BlockSpec((None,) + shape, lambda b, g, pt: (b, 0, 0))
    dense_spec = pl.BlockSpec((None, n_grp * page, d), lambda b, g, pt: (b, g, 0))

    def page_spec(i):
        return pl.BlockSpec((None, page * h, HEAD_DIM), lambda b, g, pt: (pt[b, g * n_grp + i], 0, 0))

    in_specs = [per_seq((t, d)), per_seq((rows, 1)),
                pl.BlockSpec((None, h, n_grp * page), lambda b, g, pt: (b, 0, g)),
                per_seq((h, page)), per_seq((t, d)), per_seq((t, d))]
    out_specs = [per_seq((t, d))]
    out_shape = [jax.ShapeDtypeStruct((n_seq, t, d), F32)]
    if gathered:
        in_specs += [dense_spec, dense_spec]
        kv_args = list(kv_src)
    else:
        caches = [c.reshape(c.shape[0], page * h, HEAD_DIM) for c in kv_src]
        in_specs += [page_spec(i) for i in range(n_grp)] * 2
        kv_args = [caches[0]] * n_grp + [caches[1]] * n_grp
        out_specs += [dense_spec, dense_spec]
        out_shape += [jax.ShapeDtypeStruct((n_seq, n_pages * page, d), BF16)] * 2
    return pl.pallas_call(
        functools.partial(_paged_kernel, n_grp, gathered),
        grid_spec=pltpu.PrefetchScalarGridSpec(
            num_scalar_prefetch=1, grid=(n_seq, n_pages // n_grp), in_specs=in_specs, out_specs=out_specs,
            scratch_shapes=[pltpu.VMEM((rows, d), BF16), pltpu.VMEM((rows, 1), F32), pltpu.VMEM((rows, 1), F32),
                            pltpu.VMEM((rows, d), F32)]),
        out_shape=out_shape,
        compiler_params=_cparams("parallel", "arbitrary"),
        name="fox_sample",
    )(page_table, q, cq_col, ckp_rows, ckn_rows, k_new, v_new, *kv_args)


def _largest_tile(n, cap):
    t = min(n, cap)
    while n % t:
        t //= 2
    return t


def _trunk(x3, mods, wkv0, shift0, past, w):
    n_seq, t, d = x3.shape
    n = n_seq * t
    depth = mods.shape[0]
    n_a = wkv0.shape[0]
    h = d // HEAD_DIM
    x = x3.reshape(n, d)
    per_seq = t >= 256
    if per_seq:
        tm_ffn, tm_mix = _largest_tile(t, 1024), _largest_tile(t, 256)
        tb = _largest_tile(t, 2 * LONG_CHUNK)
    else:
        tm_ffn = tm_mix = n
        tb = t

    def mod4(l, sub):
        m = mods[l, :, sub]
        if per_seq:
            return m.reshape(n_seq, 3, 1, d)
        return jnp.repeat(m, t, axis=0).transpose(1, 0, 2).reshape(1, 3, n, d)

    def tiles(tm):
        return max(t // tm, 1)

    v_first = None
    wkv_new, shift_new = [], []
    kv = None
    for l in range(depth):
        if l == n_a:
            k_new, v_new, kb, vb, logf = _shared_kv(x, w['g_kv'], w['w_kv'], w['w_f'], w['b_f'], tm_mix)
            if past is None:
                ident = jnp.arange(n // LANES, dtype=jnp.int32).reshape(n_seq, t // LANES)
                _, parts = _cumsum_pages(ident, logf.reshape(n // LANES, LANES, h), None, with_parts=True)
                parts = parts.reshape(n, LANES)
                kv = list(_kv_aug(kb, vb, parts, tm_mix)) + [parts]
            else:
                cache_k, cache_v, cache_logf, page_table = past
                ck_past, ck_new = _cumsum_pages(page_table, cache_logf, logf.reshape(n_seq, t, h))
                kv = [(cache_k, cache_v), page_table, cache_k.shape[1], ck_past, ck_new,
                      k_new.reshape(n_seq, t, d), v_new.reshape(n_seq, t, d)]
        gn = w['g_norm'][l]
        x = _ffn(x, mod4(l, 0), gn[0], w['w_ffn_in'][l, 0], w['w_ffn_out'][l, 0], tm_ffn, tiles(tm_ffn))
        m4 = mod4(l, 1)
        if l < n_a:
            p = {key: w[key + '_a'][l] for key in ('mu', 'w_rkv', 'w0', 'w1', 'w2', 'a0', 'a1', 'a2', 'g1', 'g2',
                                                   'kk', 'ka')}
            if l > 0:
                p.update({key: w[key + '_a'][l - 1] for key in ('v0', 'v1', 'v2')})
            hl = shift0[l].reshape(n_seq, 1, d) if per_seq else jnp.repeat(shift0[l], t, axis=0)
            r, lw, k, v, kk, a, gt, hmod = _rwkv_proj(x, m4, gn[1], hl, t, tm_mix, p, v_first if l > 0 else None)
            if l == 0:
                v_first = v
            o, s_new = _wkv_long(r, lw, k, v, kk, a, _pair_state(jnp.swapaxes(wkv0[l], -1, -2)),
                                 w['lnx_w_a'][l], w['lnx_b_a'][l], w['rk_a'][l], t, tb)
            wkv_new.append(jnp.swapaxes(_unpair_state(s_new), -1, -2))
            shift_new.append(hmod.reshape(n_seq, t, d)[:, -1])
            x = _out_proj(o, gt, x, m4, gn[1], w['w_o_a'][l], tm_mix, tiles(tm_mix))
        else:
            j = l - n_a
            if past is None:
                qa = _q_aug(x, m4, gn[1], w['w_q_b'][j], kv[2], tm_mix, tiles(tm_mix))
                o = _flash_prompt(qa, kv[0], kv[1], n_seq, _largest_tile(t, 512), 2)
            else:
                q = _q_proj(x, m4, gn[1], w['w_q_b'][j], tm_mix, tiles(tm_mix))
                res = _paged_attention(q.reshape(n_seq, t, d), *kv, _largest_tile(kv[1].shape[1], 4))
                o = res[0].reshape(n, d)
                if len(res) > 1:
                    kv[0] = tuple(res[1:])
            x = _out_proj(o, None, x, m4, gn[1], w['w_o_b'][j], tm_mix, tiles(tm_mix))
        x = _ffn(x, mod4(l, 2), gn[2], w['w_ffn_in'][l, 1], w['w_ffn_out'][l, 1], tm_ffn, tiles(tm_ffn))
    return (x.reshape(n_seq, t, d), jnp.stack(wkv_new), jnp.stack(shift_new), k_new.reshape(n_seq, t, h, HEAD_DIM),
            v_new.reshape(n_seq, t, h, HEAD_DIM), logf.reshape(n_seq, t, h))


def kernel(x_prompt, x_sample, state_wkv, state_shift, cache_k, cache_v, cache_logf, page_table, c_prompt, c_sample,
           w_mod, b_mod, g_norm, w_ffn_in, w_ffn_out, mu_a, w_rkv_a, w_o_a, w0_a, w1_a, w2_a, a0_a, a1_a, a2_a,
           v0_a, v1_a, v2_a, g1_a, g2_a, kk_a, ka_a, rk_a, lnx_w_a, lnx_b_a, g_kv, w_kv, w_f, b_f, w_q_b, w_o_b):
    bp, _, d = x_prompt.shape
    bs = x_sample.shape[0]
    depth = w_mod.shape[0]
    n_a = state_wkv.shape[0]
    n_pool, page = cache_k.shape[:2]
    cast = lambda a: a.astype(BF16)
    w = dict(g_norm=g_norm, w_ffn_in=cast(w_ffn_in), w_ffn_out=cast(w_ffn_out), mu_a=mu_a, w_rkv_a=cast(w_rkv_a),
             w_o_a=cast(w_o_a), w0_a=w0_a, w1_a=cast(w1_a), w2_a=cast(w2_a), a0_a=a0_a, a1_a=cast(a1_a),
             a2_a=cast(a2_a), v0_a=v0_a, v1_a=cast(v1_a), v2_a=cast(v2_a), g1_a=cast(g1_a), g2_a=cast(g2_a),
             kk_a=kk_a, ka_a=ka_a, rk_a=rk_a.reshape(n_a, d), lnx_w_a=lnx_w_a, lnx_b_a=lnx_b_a, g_kv=g_kv,
             w_kv=cast(w_kv), w_f=cast(w_f), b_f=b_f, w_q_b=cast(w_q_b), w_o_b=cast(w_o_b))

    mods = _mods(jnp.concatenate([c_prompt, c_sample], axis=0), w_mod, b_mod)
    mods = mods.reshape(depth, bp + bs, 3, 3, d)

    n_heads = d // HEAD_DIM
    zeros_wkv = jnp.zeros((n_a, bp, n_heads, HEAD_DIM, HEAD_DIM), F32)
    zeros_shift = jnp.zeros((n_a, bp, d), F32)
    y_p, wkv_p, shift_p, k_p, v_p, logf_p = _trunk(x_prompt, mods[:, :bp], zeros_wkv, zeros_shift, None, w)
    past = (cache_k, cache_v, cache_logf, page_table)
    y_s, wkv_s, shift_s, k_s, v_s, logf_s = _trunk(x_sample, mods[:, bp:], state_wkv, state_shift, past, w)
    return (y_p, y_s, wkv_p, shift_p, k_p, v_p, logf_p, wkv_s, shift_s, k_s, v_s, logf_s)
```

```python
import functools

import jax
import jax.numpy as jnp
import numpy as np
from jax import lax
from jax.experimental import pallas as pl
from jax.experimental.pallas import tpu as pltpu

F32 = jnp.float32
BF16 = jnp.bfloat16

NORM_EPS = 1e-6
GN_EPS = 64e-5
MACARON_W = 0.5
NEG_INF = -1e30
HEAD_DIM = 64
LANES = 128
PAIR = LANES // HEAD_DIM
VMEM_LIMIT = 56 * 1024 * 1024

LOG2E = 1.4426950408889634
N_BIAS = 3

_NT = (((1,), (1,)), ((), ()))
_TN = (((0,), (0,)), ((), ()))


def _cparams(*sem):
    return pltpu.CompilerParams(dimension_semantics=sem, vmem_limit_bytes=VMEM_LIMIT)


def _dot(a, b):
    return jnp.dot(a.astype(BF16), b.astype(BF16), preferred_element_type=F32)


def _dot_nt(a, b):
    return lax.dot_general(a.astype(BF16), b.astype(BF16), _NT, preferred_element_type=F32)


def _dot_tn(a, b):
    return lax.dot_general(a.astype(BF16), b.astype(BF16), _TN, preferred_element_type=F32)


def _split3(x):
    hi = x.astype(BF16)
    r1 = x - hi.astype(F32)
    mid = r1.astype(BF16)
    lo = (r1 - mid.astype(F32)).astype(BF16)
    return hi, mid, lo


def _dot3(sel, x):
    hi, mid, lo = _split3(x)
    out = jnp.dot(sel, hi, preferred_element_type=F32)
    out = out + jnp.dot(sel, mid, preferred_element_type=F32)
    return out + jnp.dot(sel, lo, preferred_element_type=F32)


def _dot3_rhs(x, sel):
    hi, mid, lo = _split3(x)
    out = jnp.dot(hi, sel, preferred_element_type=F32)
    out = out + jnp.dot(mid, sel, preferred_element_type=F32)
    return out + jnp.dot(lo, sel, preferred_element_type=F32)


def _rms(x, g):
    return x * lax.rsqrt(jnp.mean(x * x, axis=-1, keepdims=True) + NORM_EPS) * g


def _sigmoid(x):
    return 1.0 / (1.0 + jnp.exp(-x))


def _silu(x):
    return x * _sigmoid(x)


def _head_ones(n):
    r = lax.broadcasted_iota(jnp.int32, (n, n), 0) // HEAD_DIM
    c = lax.broadcasted_iota(jnp.int32, (n, n), 1) // HEAD_DIM
    return jnp.where(r == c, 1.0, 0.0).astype(BF16)


def _head_sum(x, ones):
    n = ones.shape[0]
    hi = x.astype(BF16)
    lo = (x - hi.astype(F32)).astype(BF16)
    cols = []
    for i in range(0, x.shape[1], n):
        s = jnp.dot(hi[:, i:i + n], ones, preferred_element_type=F32)
        cols.append(s + jnp.dot(lo[:, i:i + n], ones, preferred_element_type=F32))
    return cols[0] if len(cols) == 1 else jnp.concatenate(cols, axis=1)


def _mods_kernel(c_ref, w_ref, b_ref, o_ref):
    o_ref[...] = _dot(_silu(c_ref[...]), w_ref[...]) + b_ref[...]


def _mods(c_all, w_mod, b_mod):
    depth, d, n9 = w_mod.shape
    nb = c_all.shape[0]
    tn = d
    return pl.pallas_call(
        _mods_kernel,
        grid=(depth, n9 // tn),
        in_specs=[
            pl.BlockSpec((nb, d), lambda l, j: (0, 0)),
            pl.BlockSpec((None, d, tn), lambda l, j: (l, 0, j)),
            pl.BlockSpec((None, 1, tn), lambda l, j: (l, 0, j)),
        ],
        out_specs=pl.BlockSpec((None, nb, tn), lambda l, j: (l, 0, j)),
        out_shape=jax.ShapeDtypeStruct((depth, nb, n9), F32),
        compiler_params=_cparams("arbitrary", "arbitrary"),
        name="mods",
    )(c_all, w_mod, b_mod.reshape(depth, 1, n9))


def _mod_spec(m4, tm, tiles_per_seq):
    _, _, rows, d = m4.shape
    if rows == 1:
        return pl.BlockSpec((None, 3, 1, d), lambda i, *_: (i // tiles_per_seq, 0, 0, 0))
    return pl.BlockSpec((None, 3, tm, d), lambda i, *_: (0, 0, i, 0))


def _ffn_kernel(x_ref, m_ref, g_ref, wg_ref, wu_ref, wo_ref, o_ref, h_scr, acc_scr):
    j = pl.program_id(1)

    @pl.when(j == 0)
    def _():
        h = _rms(x_ref[...], g_ref[0:1]) * (1.0 + m_ref[1]) + m_ref[0]
        h_scr[...] = h.astype(BF16)
        acc_scr[...] = jnp.zeros_like(acc_scr)

    h = h_scr[...]
    gate = _dot(h, wg_ref[...])
    up = _dot(h, wu_ref[...])
    acc_scr[...] += _dot(_silu(gate) * up, wo_ref[...])

    @pl.when(j == pl.num_programs(1) - 1)
    def _():
        o_ref[...] = x_ref[...] + MACARON_W * m_ref[2] * _rms(acc_scr[...], g_ref[1:2])


def _ffn(x, m4, gn, w_in, w_out, tm, tiles_per_seq):
    n, d = x.shape
    f = w_out.shape[0]
    tf = 256 if f % 256 == 0 else LANES
    nf = f // tf
    return pl.pallas_call(
        _ffn_kernel,
        grid=(n // tm, nf),
        in_specs=[
            pl.BlockSpec((tm, d), lambda i, j: (i, 0)),
            _mod_spec(m4, tm, tiles_per_seq),
            pl.BlockSpec((2, d), lambda i, j: (0, 0)),
            pl.BlockSpec((d, tf), lambda i, j: (0, j)),
            pl.BlockSpec((d, tf), lambda i, j: (0, j + nf)),
            pl.BlockSpec((tf, d), lambda i, j: (j, 0)),
        ],
        out_specs=pl.BlockSpec((tm, d), lambda i, j: (i, 0)),
        out_shape=jax.ShapeDtypeStruct((n, d), F32),
        scratch_shapes=[pltpu.VMEM((tm, d), BF16), pltpu.VMEM((tm, d), F32)],
        compiler_params=_cparams("parallel", "arbitrary"),
        name="ffn",
    )(x, m4, gn, w_in, w_in, w_out)


def _rwkv_proj_kernel(has_vfirst, seq_len, x_ref, m_ref, g_ref, hl_ref, mu_ref, wrkv_ref, w0_ref, w1_ref,
                      w2_ref, a0_ref, a1_ref, a2_ref, g1_ref, g2_ref, kka_ref, kaa_ref, *rest):
    if has_vfirst:
        v0_ref, v1_ref, v2_ref, vf_ref = rest[:4]
        rest = rest[4:]
    r_ref, lw_ref, k_ref, v_ref, kk_ref, a_ref, gt_ref, h_ref, carry_scr = rest
    tm, d = x_ref.shape
    i = pl.program_id(0)

    h = _rms(x_ref[...], g_ref[0:1]) * (1.0 + m_ref[1]) + m_ref[0]
    h_ref[...] = h
    row = lax.broadcasted_iota(jnp.int32, (tm, d), 0)
    rolled = pltpu.roll(h, 1, 0)
    if hl_ref.shape[0] == 1:
        tiles_per_seq = seq_len // tm
        first = jnp.where(i % tiles_per_seq == 0, hl_ref[...], carry_scr[...])
        h_prev = jnp.where(row == 0, first, rolled)
        carry_scr[...] = h[tm - 1:tm, :]
    else:
        h_prev = jnp.where(row % seq_len == 0, hl_ref[...], rolled)
    xx = h_prev - h
    xr, xw, xk, xv, xa, xg = [h + xx * mu_ref[j:j + 1] for j in range(6)]

    r = _dot(xr, wrkv_ref[0])
    k = _dot(xk, wrkv_ref[1])
    v = _dot(xv, wrkv_ref[2])
    z = w0_ref[...] + _dot(jnp.tanh(_dot(xw, w1_ref[...])), w2_ref[...])
    lw = -_sigmoid(z) * jnp.exp(F32(-0.5))
    a = _sigmoid(a0_ref[...] + _dot(_dot(xa, a1_ref[...]), a2_ref[...]))
    gt = _dot(_sigmoid(_dot(xg, g1_ref[...])), g2_ref[...])
    if has_vfirst:
        mix = _sigmoid(v0_ref[...] + _dot(_dot(xv, v1_ref[...]), v2_ref[...]))
        v = v + (vf_ref[...] - v) * mix
    kk = k * kka_ref[...]
    ones = _head_ones(min(d, 2 * LANES))
    kk = kk * lax.rsqrt(jnp.maximum(_head_sum(kk * kk, ones), 1e-24))
    k = k * (1.0 + (a - 1.0) * kaa_ref[...])

    r_ref[...] = r
    lw_ref[...] = lw
    k_ref[...] = k
    v_ref[...] = v
    kk_ref[...] = kk
    a_ref[...] = a
    gt_ref[...] = gt


def _rwkv_proj(x, m4, gn, hl, seq_len, tm, p, v_first):
    n, d = x.shape
    tiles_per_seq = max(seq_len // tm, 1)
    has_vfirst = v_first is not None
    row = lambda a: a.reshape(1, d)
    const = lambda a: pl.BlockSpec(a.shape, lambda i: (0,) * a.ndim)
    tok = pl.BlockSpec((tm, d), lambda i: (i, 0))
    if hl.ndim == 3:
        hl_spec = pl.BlockSpec((None, 1, d), lambda i: (i // tiles_per_seq, 0, 0))
    else:
        hl_spec = tok
    args = [x, m4, gn, hl, p['mu'], p['w_rkv'], row(p['w0']), p['w1'], p['w2'], row(p['a0']), p['a1'], p['a2'],
            p['g1'], p['g2'], row(p['kk']), row(p['ka'])]
    specs = [tok, _mod_spec(m4, tm, tiles_per_seq), const(gn), hl_spec] + [const(a) for a in args[4:]]
    if has_vfirst:
        extra = [row(p['v0']), p['v1'], p['v2']]
        args += extra + [v_first]
        specs += [const(a) for a in extra] + [tok]
    return pl.pallas_call(
        functools.partial(_rwkv_proj_kernel, has_vfirst, seq_len),
        grid=(n // tm,),
        in_specs=specs,
        out_specs=[tok] * 8,
        out_shape=[jax.ShapeDtypeStruct((n, d), F32)] * 8,
        scratch_shapes=[pltpu.VMEM((1, d), F32)],
        compiler_params=_cparams("arbitrary"),
        name="rwkv_proj",
    )(*args)


LONG_CHUNK = 64


def _long_masks():
    n2 = PAIR * LONG_CHUNK
    r = np.arange(n2)[:, None]
    c = np.arange(n2)[None, :]
    masks = [c < r, c <= r, c == r]
    b = 1
    while b < LONG_CHUNK:
        masks.append((r // (2 * b) == c // (2 * b)) & ((r // b) % 2 == 1) & ((c // b) % 2 == 0))
        b *= 2
    return jnp.asarray(np.stack(masks).astype(np.float32))


def _wkv_long_kernel(r_ref, lw_ref, k_ref, v_ref, kk_ref, a_ref, s0_ref, lnw_ref, lnb_ref, rk_ref, mask_ref,
                     o_ref, sout_ref, st_scr, wq_scr, bk_scr, lkk_scr, v2_scr, kt_scr, bt_scr, lrb_scr, lvt_scr,
                     x_scr, gc_scr, lab_scr, tinv_scr, ol_scr, ds_scr, ut_scr):
    c = LONG_CHUNK
    n2 = PAIR * c
    tb, d = r_ref.shape
    n_pairs = d // LANES
    t_idx = pl.program_id(1)

    @pl.when(t_idx == 0)
    def _():
        st_scr[...] = s0_ref[...]

    rows_in = min(tb, c)
    n_chunks = max(tb // c, 1)
    items = [(ic, p) for ic in range(n_chunks) for p in range(n_pairs)]
    dotf = lambda x, y: jnp.dot(x, y, preferred_element_type=F32)
    lane = lax.broadcasted_iota(jnp.int32, (c, LANES), 1)
    head_lane = [lane // HEAD_DIM == h for h in range(PAIR)]
    tri = jnp.where(lax.broadcasted_iota(jnp.int32, (c, c), 1) <= lax.broadcasted_iota(jnp.int32, (c, c), 0),
                    1.0, 0.0).astype(BF16)

    def stack(x):
        return jnp.concatenate([jnp.where(m, x, 0.0) for m in head_lane], axis=0)

    for i, (ic, p) in enumerate(items):
        rows, ls = slice(ic * c, ic * c + rows_in), slice(p * LANES, (p + 1) * LANES)
        rc, lwc, kc, vc, kkc, ac = [ref[rows, ls] for ref in (r_ref, lw_ref, k_ref, v_ref, kk_ref, a_ref)]
        if rows_in < c:
            pad = jnp.zeros((c - rows_in, LANES), F32)
            rc, lwc, kc, vc, kkc, ac = [jnp.concatenate([x, pad], axis=0) for x in (rc, lwc, kc, vc, kkc, ac)]
        cum = _dot3(tri, lwc)
        g = jnp.exp(cum)
        gi = jnp.exp(-cum)
        g_end = g[c - 1:c, :]
        bb = ac * kkc * gi
        kb = kc * gi
        wq_scr[i, :n2] = stack(-kkc * jnp.exp(cum - lwc)).astype(BF16)
        wq_scr[i, n2:] = stack(rc * g).astype(BF16)
        bk_scr[i, :n2] = stack(bb).astype(BF16)
        bk_scr[i, n2:] = stack(kb).astype(BF16)
        v2_scr[i] = stack(vc).astype(BF16)
        kt_scr[i] = stack(kb * g_end).astype(BF16)
        bt_scr[i] = stack(bb * g_end).T.astype(BF16)
        gc_scr[i] = jnp.broadcast_to(g_end, (LANES, LANES)).T

    for i in range(len(items)):
        scores = lax.dot_general(wq_scr[i], bk_scr[i], _NT, preferred_element_type=F32)
        l_ab = scores[:n2, :n2] * mask_ref[0]
        lab_scr[i] = l_ab
        tinv_scr[i] = mask_ref[2] + l_ab * mask_ref[3]
        lkk_scr[i, :n2] = (scores[:n2, n2:] * mask_ref[0]).astype(BF16)
        lkk_scr[i, n2:] = (scores[n2:, n2:] * mask_ref[1]).astype(BF16)
        lrb_scr[i] = (scores[n2:, :n2] * mask_ref[1]).astype(BF16)

    for i in range(len(items)):
        lv = dotf(lkk_scr[i], v2_scr[i])
        lvt_scr[i] = lv[:n2].astype(BF16)
        ol_scr[i] = lv[n2:]
        ds_scr[i] = lax.dot_general(kt_scr[i], v2_scr[i], _TN, preferred_element_type=F32)

    for level in range(4, mask_ref.shape[0]):
        for i in range(len(items)):
            x_scr[i] = dotf((lab_scr[i] * mask_ref[level]).astype(BF16), tinv_scr[i].astype(BF16)).astype(BF16)
        for i in range(len(items)):
            tinv = tinv_scr[i]
            tinv_scr[i] = tinv + dotf(tinv.astype(BF16), x_scr[i])

    for i in range(len(items)):
        tinv = tinv_scr[i].astype(BF16)
        ut_scr[i] = dotf(tinv, lvt_scr[i])
        wq_scr[i, :n2] = dotf(tinv, wq_scr[i, :n2]).astype(BF16)

    def advance(ic, carry):
        rows = pl.ds(pl.multiple_of(ic * c, c), rows_in)
        for p in range(n_pairs):
            i = ic * n_pairs + p
            st = st_scr[p]
            ws = dotf(wq_scr[i], st.astype(BF16))
            u2 = (ws[:n2] + ut_scr[i]).astype(BF16)
            o2 = ws[n2:] + dotf(lrb_scr[i], u2) + ol_scr[i]
            st_scr[p] = gc_scr[i] * st + dotf(bt_scr[i], u2) + ds_scr[i]
            o = o2[0:c]
            for h in range(1, PAIR):
                o = o + o2[h * c:(h + 1) * c]
            o_ref[rows, p * LANES:(p + 1) * LANES] = o[:rows_in]
        return carry

    lax.fori_loop(0, n_chunks, advance, 0)

    ones = _head_ones(min(d, 2 * LANES))
    o = o_ref[...]
    cen = o - _head_sum(o, ones) * (1.0 / HEAD_DIM)
    var = _head_sum(cen * cen, ones) * (1.0 / HEAD_DIM)
    bonus = _head_sum(r_ref[...] * k_ref[...] * rk_ref[...], ones) * v_ref[...]
    o_ref[...] = cen * lax.rsqrt(var + GN_EPS) * lnw_ref[...] + lnb_ref[...] + bonus

    @pl.when(t_idx == pl.num_programs(1) - 1)
    def _():
        sout_ref[...] = st_scr[...]


def _wkv_long(r, lw, k, v, kk, a, s0t, lnw, lnb, rk, seq_len, tb):
    n, d = r.shape
    n_seq = n // seq_len
    n_pairs = d // LANES
    nt = seq_len // tb
    n_items = max(tb // LONG_CHUNK, 1) * n_pairs
    n2 = PAIR * LONG_CHUNK
    assert n2 == LANES
    masks = _long_masks()
    tok = pl.BlockSpec((tb, d), lambda b, t: (b * nt + t, 0))
    vec = pl.BlockSpec((1, d), lambda b, t: (0, 0))
    st = pl.BlockSpec((None, n_pairs, LANES, LANES), lambda b, t: (b, 0, 0, 0))
    per = lambda rows, dt: pltpu.VMEM((n_items, rows, LANES), dt)
    return pl.pallas_call(
        _wkv_long_kernel,
        grid=(n_seq, nt),
        in_specs=[tok] * 6 + [st, vec, vec, vec, pl.BlockSpec(masks.shape, lambda b, t: (0, 0, 0))],
        out_specs=[tok, st],
        out_shape=[jax.ShapeDtypeStruct((n, d), F32), jax.ShapeDtypeStruct(s0t.shape, F32)],
        scratch_shapes=[pltpu.VMEM((n_pairs, LANES, LANES), F32)]
        + [per(2 * n2, BF16)] * 3 + [per(n2, BF16)] * 6 + [per(n2, F32)] * 6,
        compiler_params=_cparams("parallel", "arbitrary"),
        name="wkv_long",
    )(r, lw, k, v, kk, a, s0t, lnw.reshape(1, d), lnb.reshape(1, d), rk.reshape(1, d), masks)


def _pair_state(s):
    n_seq, h, n, _ = s.shape
    s = s.reshape(n_seq, h // PAIR, PAIR, n, n)
    eye = jnp.eye(PAIR, dtype=s.dtype)
    out = s[:, :, :, :, None, :] * eye[None, None, :, None, :, None]
    return out.reshape(n_seq, h // PAIR, PAIR * n, PAIR * n)


def _unpair_state(sp):
    n_seq, hp, n2, _ = sp.shape
    n = n2 // PAIR
    s = sp.reshape(n_seq, hp, PAIR, n, PAIR, n)
    return jnp.stack([s[:, :, a, :, a, :] for a in range(PAIR)], axis=2).reshape(n_seq, hp * PAIR, n, n)


def _out_kernel(has_gate, res_w, *refs):
    if has_gate:
        a_ref, gt_ref, x_ref, m_ref, g_ref, w_ref, o_ref = refs
        a = a_ref[...] * gt_ref[...]
    else:
        a_ref, x_ref, m_ref, g_ref, w_ref, o_ref = refs
        a = a_ref[...]
    y = _dot(a, w_ref[...])
    o_ref[...] = x_ref[...] + res_w * m_ref[2] * _rms(y, g_ref[1:2])


def _out_proj(a, gate, x, m4, gn, w, tm, tiles_per_seq, res_w=1.0):
    n, d = x.shape
    tok = pl.BlockSpec((tm, d), lambda i: (i, 0))
    args = [a] + ([gate] if gate is not None else []) + [x, m4, gn, w]
    specs = [tok] * (len(args) - 3) + [_mod_spec(m4, tm, tiles_per_seq), pl.BlockSpec((2, d), lambda i: (0, 0)),
                                       pl.BlockSpec(w.shape, lambda i: (0, 0))]
    return pl.pallas_call(
        functools.partial(_out_kernel, gate is not None, res_w),
        grid=(n // tm,),
        in_specs=specs,
        out_specs=tok,
        out_shape=jax.ShapeDtypeStruct((n, d), F32),
        compiler_params=_cparams("parallel"),
        name="out_proj",
    )(*args)


def _kv_kernel(x_ref, g_ref, wkv_ref, wf_ref, bf_ref, k_ref, v_ref, kb_ref, vb_ref, lf_ref):
    d = x_ref.shape[1]
    s = _rms(x_ref[...], g_ref[...])
    kv = _dot(s, wkv_ref[...])
    k = kv[:, :d]
    v = kv[:, d:]
    k_ref[...] = k
    v_ref[...] = v
    kb_ref[...] = k.astype(BF16)
    vb_ref[...] = v.astype(BF16)
    z = _dot(s, wf_ref[...]) + bf_ref[...]
    lf_ref[...] = jnp.minimum(z, 0.0) - jnp.log(1.0 + jnp.exp(-jnp.abs(z)))


def _shared_kv(x, g_kv, w_kv, w_f, b_f, tm):
    n, d = x.shape
    h = w_f.shape[1]
    tok = pl.BlockSpec((tm, d), lambda i: (i, 0))
    return pl.pallas_call(
        _kv_kernel,
        grid=(n // tm,),
        in_specs=[tok, pl.BlockSpec((1, d), lambda i: (0, 0)), pl.BlockSpec(w_kv.shape, lambda i: (0, 0)),
                  pl.BlockSpec(w_f.shape, lambda i: (0, 0)), pl.BlockSpec((1, h), lambda i: (0, 0))],
        out_specs=[tok, tok, tok, tok, pl.BlockSpec((tm, h), lambda i: (i, 0))],
        out_shape=[jax.ShapeDtypeStruct((n, d), F32)] * 2 + [jax.ShapeDtypeStruct((n, d), BF16)] * 2
        + [jax.ShapeDtypeStruct((n, h), F32)],
        compiler_params=_cparams("parallel"),
        name="shared_kv",
    )(x, g_kv.reshape(1, d), w_kv, w_f, b_f.reshape(1, h))


def _cumsum_kernel(n_grp, has_new, has_parts, pt_ref, *refs):
    lf_refs = refs[:n_grp]
    rest = refs[n_grp:]
    if has_new:
        new_ref, ck_ref, cknew_ref, carry_scr = rest
    elif has_parts:
        ck_ref, parts_ref, carry_scr = rest
    else:
        ck_ref, carry_scr = rest
    del pt_ref
    j = pl.program_id(1)
    page, h = lf_refs[0].shape

    @pl.when(j == 0)
    def _():
        carry_scr[...] = jnp.zeros_like(carry_scr)

    ri = lax.broadcasted_iota(jnp.int32, (page, page), 0)
    ci = lax.broadcasted_iota(jnp.int32, (page, page), 1)
    tri = jnp.where(ci <= ri, 1.0, 0.0).astype(BF16)
    local = [_dot3(tri, lf[...]) for lf in lf_refs]
    base = carry_scr[...]
    for i in range(n_grp):
        ck = base + local[i]
        ck_ref[i * page:(i + 1) * page, :] = ck
        base = ck[page - 1:page, :]

        if has_parts:
            src_h = lax.broadcasted_iota(jnp.int32, (h, LANES), 0)
            dst = lax.broadcasted_iota(jnp.int32, (h, LANES), 1)
            out = jnp.where(lax.broadcasted_iota(jnp.int32, (page, LANES), 1) // h == N_BIAS, 1.0, 0.0)
            for x, part in enumerate(_split3(ck * LOG2E)):
                out = out + jnp.dot(part, jnp.where(dst == x * h + src_h, 1.0, 0.0).astype(BF16),
                                    preferred_element_type=F32)
            parts_ref[i * page:(i + 1) * page, :] = out.astype(BF16)
    carry_scr[...] = base

    if has_new:
        @pl.when(j == pl.num_programs(1) - 1)
        def _():
            t = new_ref.shape[0]
            cknew_ref[...] = base + _dot3(tri[:t, :t], new_ref[...])


def _cumsum_pages(page_table, lf_pages, lf_new, with_parts=False):
    n_seq, n_pages = page_table.shape
    _, page, h = lf_pages.shape
    has_new = lf_new is not None
    n_grp = _largest_tile(n_pages, 8)

    def page_spec(i):
        return pl.BlockSpec((None, page, h), lambda b, j, pt: (pt[b, j * n_grp + i], 0, 0))

    in_specs = [page_spec(i) for i in range(n_grp)]
    out_specs = [pl.BlockSpec((None, n_grp * page, h), lambda b, j, pt: (b, j, 0))]
    out_shape = [jax.ShapeDtypeStruct((n_seq, n_pages * page, h), F32)]
    args = [lf_pages] * n_grp
    if has_new:
        t = lf_new.shape[1]
        in_specs.append(pl.BlockSpec((None, t, h), lambda b, j, pt: (b, 0, 0)))
        out_specs.append(pl.BlockSpec((None, t, h), lambda b, j, pt: (b, 0, 0)))
        out_shape.append(jax.ShapeDtypeStruct((n_seq, t, h), F32))
        args.append(lf_new)
    elif with_parts:
        out_specs.append(pl.BlockSpec((None, n_grp * page, LANES), lambda b, j, pt: (b, j, 0)))
        out_shape.append(jax.ShapeDtypeStruct((n_seq, n_pages * page, LANES), BF16))
    return pl.pallas_call(
        functools.partial(_cumsum_kernel, n_grp, has_new, with_parts and not has_new),
        grid_spec=pltpu.PrefetchScalarGridSpec(
            num_scalar_prefetch=1, grid=(n_seq, n_pages // n_grp), in_specs=in_specs, out_specs=out_specs,
            scratch_shapes=[pltpu.VMEM((1, h), F32)]),
        out_shape=out_shape,
        compiler_params=_cparams("parallel", "arbitrary"),
        name="cumsum",
    )(page_table, *args)


def _q_kernel(x_ref, m_ref, g_ref, w_ref, q_ref):
    h = _rms(x_ref[...], g_ref[0:1]) * (1.0 + m_ref[1]) + m_ref[0]
    q_ref[...] = (_dot(h, w_ref[...]) * (HEAD_DIM ** -0.5)).astype(BF16)


def _q_proj(x, m4, gn, w, tm, tiles_per_seq):
    n, d = x.shape
    tok = pl.BlockSpec((tm, d), lambda i: (i, 0))
    return pl.pallas_call(
        _q_kernel,
        grid=(n // tm,),
        in_specs=[tok, _mod_spec(m4, tm, tiles_per_seq), pl.BlockSpec((2, d), lambda i: (0, 0)),
                  pl.BlockSpec(w.shape, lambda i: (0, 0))],
        out_specs=tok,
        out_shape=jax.ShapeDtypeStruct((n, d), BF16),
        compiler_params=_cparams("parallel"),
        name="q_proj",
    )(x, m4, gn, w)


def _pair_to_tiles():
    m = np.zeros((LANES, PAIR * LANES), np.float32)
    r = np.arange(LANES)
    m[r, (r // HEAD_DIM) * LANES + r % HEAD_DIM] = 1.0
    return jnp.asarray(m, BF16)


def _bias_place(h, for_keys):
    m = np.zeros((LANES, h * LANES), np.float32)
    for hh in range(h):
        for x in range(N_BIAS):
            if for_keys:
                m[N_BIAS * h + hh, hh * LANES + HEAD_DIM + x] = 1.0
                m[x * h + hh, hh * LANES + HEAD_DIM + N_BIAS + x] = -1.0
            else:
                m[x * h + hh, hh * LANES + HEAD_DIM + x] = 1.0
                m[N_BIAS * h + hh, hh * LANES + HEAD_DIM + N_BIAS + x] = 1.0
    return jnp.asarray(m, BF16)


def _to_head_tiles(x, place, extra):
    cols = []
    for p in range(x.shape[1] // LANES):
        sl = slice(p * PAIR * LANES, (p + 1) * PAIR * LANES)
        cols.append(jnp.dot(x[:, p * LANES:(p + 1) * LANES], place, preferred_element_type=F32) + extra[:, sl])
    return jnp.concatenate(cols, axis=1)


def _q_aug_kernel(x_ref, m_ref, g_ref, w_ref, parts_ref, place_ref, bias_ref, q_ref):
    h = _rms(x_ref[...], g_ref[0:1]) * (1.0 + m_ref[1]) + m_ref[0]
    q = (_dot(h, w_ref[...]) * (HEAD_DIM ** -0.5 * LOG2E)).astype(BF16)
    bias = jnp.dot(parts_ref[...], bias_ref[...], preferred_element_type=F32)
    q_ref[...] = _to_head_tiles(q, place_ref[...], bias).astype(BF16)


def _q_aug(x, m4, gn, w, parts, tm, tiles_per_seq):
    n, d = x.shape
    tok = pl.BlockSpec((tm, d), lambda i: (i, 0))
    place, bias = _pair_to_tiles(), _bias_place(d // HEAD_DIM, False)
    const = lambda a: pl.BlockSpec(a.shape, lambda i: (0, 0))
    return pl.pallas_call(
        _q_aug_kernel,
        grid=(n // tm,),
        in_specs=[tok, _mod_spec(m4, tm, tiles_per_seq), pl.BlockSpec((2, d), lambda i: (0, 0)), const(w),
                  pl.BlockSpec((tm, LANES), lambda i: (i, 0)), const(place), const(bias)],
        out_specs=pl.BlockSpec((tm, PAIR * d), lambda i: (i, 0)),
        out_shape=jax.ShapeDtypeStruct((n, PAIR * d), BF16),
        compiler_params=_cparams("parallel"),
        name="q_aug",
    )(x, m4, gn, w, parts, place, bias)


def _kv_aug_kernel(k_ref, v_ref, parts_ref, place_ref, bias_ref, ka_ref, va_ref):
    tm, d2 = ka_ref.shape
    bias = jnp.dot(parts_ref[...], bias_ref[...], preferred_element_type=F32)
    ka_ref[...] = _to_head_tiles(k_ref[...], place_ref[...], bias).astype(BF16)
    ones = jnp.where(lax.broadcasted_iota(jnp.int32, (1, d2), 1) % LANES >= HEAD_DIM, 1.0, 0.0)
    va_ref[...] = _to_head_tiles(v_ref[...], place_ref[...], ones).astype(BF16)


def _kv_aug(kb, vb, parts, tm):
    n, d = kb.shape
    tok = pl.BlockSpec((tm, d), lambda i: (i, 0))
    wide = pl.BlockSpec((tm, PAIR * d), lambda i: (i, 0))
    place, bias = _pair_to_tiles(), _bias_place(d // HEAD_DIM, True)
    const = lambda a: pl.BlockSpec(a.shape, lambda i: (0, 0))
    return pl.pallas_call(
        _kv_aug_kernel,
        grid=(n // tm,),
        in_specs=[tok, tok, pl.BlockSpec((tm, LANES), lambda i: (i, 0)), const(place), const(bias)],
        out_specs=[wide, wide],
        out_shape=[jax.ShapeDtypeStruct((n, PAIR * d), BF16)] * 2,
        compiler_params=_cparams("parallel"),
        name="kv_aug",
    )(kb, vb, parts, place, bias)


def _flash_kernel(qi_ref, kj_ref, q_ref, k_ref, v_ref, o_ref, m_scr, acc_scr):
    n = pl.program_id(2)
    qi = qi_ref[n]
    kj = kj_ref[n]
    tq = q_ref.shape[0]
    tk = k_ref.shape[0]
    n_heads = q_ref.shape[1] // LANES

    @pl.when(kj == 0)
    def _():
        m_scr[...] = jnp.full_like(m_scr, NEG_INF)
        acc_scr[...] = jnp.zeros_like(acc_scr)

    def update(masked):
        if masked:
            valid = (lax.broadcasted_iota(jnp.int32, (tq, tk), 1)
                     <= lax.broadcasted_iota(jnp.int32, (tq, tk), 0))
        for h in range(n_heads):
            sl = slice(h * LANES, (h + 1) * LANES)
            s = lax.dot_general(q_ref[:, sl], k_ref[:, sl], _NT, preferred_element_type=F32)
            if masked:
                s = jnp.where(valid, s, NEG_INF)
            cols = [s[:, c * LANES:(c + 1) * LANES] for c in range(tk // LANES)]
            m_cur = cols[0]
            for col in cols[1:]:
                m_cur = jnp.maximum(m_cur, col)
            m_old = m_scr[h]
            m_new = jnp.maximum(m_old, jnp.max(m_cur, axis=1, keepdims=True))
            p = jnp.concatenate([jnp.exp2(col - m_new) for col in cols], axis=1).astype(BF16)
            acc_scr[h] = jnp.exp2(m_old - m_new) * acc_scr[h] + jnp.dot(p, v_ref[:, sl],
                                                                        preferred_element_type=F32)
            m_scr[h] = m_new

    @pl.when(kj < qi)
    def _():
        update(False)

    @pl.when(kj == qi)
    def _():
        update(True)
        lane = lax.broadcasted_iota(jnp.int32, (tq, LANES), 1)
        outs = []
        for h in range(n_heads):
            acc = acc_scr[h]
            outs.append(acc / pltpu.roll(acc, HEAD_DIM, 1))
        for p in range(n_heads // PAIR):
            o_ref[:, p * LANES:(p + 1) * LANES] = jnp.where(
                lane < HEAD_DIM, outs[PAIR * p], pltpu.roll(outs[PAIR * p + 1], HEAD_DIM, 1))


def _flash_prompt(qa, ka, va, n_seq, tq, heads_per_step):
    n, d2 = qa.shape
    t = n // n_seq
    nq = t // tq
    width = heads_per_step * LANES
    qi = jnp.asarray([i for i in range(nq) for _ in range(i + 1)], jnp.int32)
    kj = jnp.asarray([j for i in range(nq) for j in range(i + 1)], jnp.int32)
    return pl.pallas_call(
        _flash_kernel,
        grid_spec=pltpu.PrefetchScalarGridSpec(
            num_scalar_prefetch=2,
            grid=(n_seq, d2 // width, int(qi.shape[0])),
            in_specs=[
                pl.BlockSpec((tq, width), lambda b, p, s, qi, kj: (b * nq + qi[s], p)),
                pl.BlockSpec((tq, width), lambda b, p, s, qi, kj: (b * nq + kj[s], p)),
                pl.BlockSpec((tq, width), lambda b, p, s, qi, kj: (b * nq + kj[s], p)),
            ],
            out_specs=pl.BlockSpec((tq, width // PAIR), lambda b, p, s, qi, kj: (b * nq + qi[s], p)),
            scratch_shapes=[pltpu.VMEM((heads_per_step, tq, LANES), F32),
                            pltpu.VMEM((heads_per_step, tq, LANES), F32)]),
        out_shape=jax.ShapeDtypeStruct((n, d2 // PAIR), F32),
        compiler_params=_cparams("parallel", "parallel", "arbitrary"),
        name="fox_prompt",
    )(qi, kj, qa, ka, va)


def _paged_kernel(n_grp, gathered, pt_ref, q_ref, cq_ref, ckp_ref, ckn_ref, kn_ref, vn_ref, *rest):
    t, d = q_ref.shape
    n_heads = d // HEAD_DIM
    rows = n_heads * t
    if gathered:
        kd_ref, vd_ref, o_ref, qbd_scr, m_scr, l_scr, acc_scr = rest
        page = kd_ref.shape[0] // n_grp
    else:
        k_refs = rest[:n_grp]
        v_refs = rest[n_grp:2 * n_grp]
        o_ref, kd_ref, vd_ref, qbd_scr, m_scr, l_scr, acc_scr = rest[2 * n_grp:]
        page = k_refs[0].shape[0]
    del pt_ref
    g = pl.program_id(1)

    row_head = lax.broadcasted_iota(jnp.int32, (rows, d), 0) // t
    lane_head = lax.broadcasted_iota(jnp.int32, (rows, d), 1) // HEAD_DIM
    expand = jnp.where(lax.broadcasted_iota(jnp.int32, (rows, n_heads), 0) // t
                       == lax.broadcasted_iota(jnp.int32, (rows, n_heads), 1), 1.0, 0.0).astype(BF16)

    @pl.when(g == 0)
    def _():
        q = q_ref[...].astype(F32)
        q_rows = jnp.concatenate([q] * n_heads, axis=0)
        qbd_scr[...] = jnp.where(row_head == lane_head, q_rows, 0.0).astype(BF16)
        m_scr[...] = jnp.full_like(m_scr, NEG_INF)
        l_scr[...] = jnp.zeros_like(l_scr)
        acc_scr[...] = jnp.zeros_like(acc_scr)

    qbd = qbd_scr[...]
    cq = cq_ref[...]

    def attend(s, vs):
        m_old = m_scr[...]
        m_new = jnp.maximum(m_old, jnp.max(s, axis=1, keepdims=True))
        alpha = jnp.exp(m_old - m_new)
        p = jnp.exp(s - m_new)
        l_scr[...] = alpha * l_scr[...] + jnp.sum(p, axis=1, keepdims=True)
        pv = None
        for i, vpage in enumerate(vs):
            part = _dot(p[:, i * page:(i + 1) * page], vpage)
            pv = part if pv is None else pv + part
        acc_scr[...] = alpha * acc_scr[...] + pv
        m_scr[...] = m_new

    def dense(ref):
        return pltpu.einshape("khd->k(hd)", ref[...]).astype(BF16)

    if gathered:
        ks = [kd_ref[i * page:(i + 1) * page, :] for i in range(n_grp)]
        vs = [vd_ref[i * page:(i + 1) * page, :] for i in range(n_grp)]
    else:
        ks = [dense(kr) for kr in k_refs]
        vs = [dense(vr) for vr in v_refs]
        for i in range(n_grp):
            kd_ref[i * page:(i + 1) * page, :] = ks[i]
            vd_ref[i * page:(i + 1) * page, :] = vs[i]
    s = jnp.concatenate([_dot_nt(qbd, kp) for kp in ks], axis=1)
    s = s + (cq - _dot3(expand, ckp_ref[...]))
    attend(s, vs)

    @pl.when(g == pl.num_programs(1) - 1)
    def _():
        pad = jnp.zeros((page - t, d), F32)
        k_new = jnp.concatenate([kn_ref[...], pad], axis=0)
        v_new = jnp.concatenate([vn_ref[...], pad], axis=0)
        s_new = _dot_nt(qbd, k_new) + (cq - _dot3(expand, ckn_ref[...]))
        key = lax.broadcasted_iota(jnp.int32, (rows, page), 1)
        query = lax.broadcasted_iota(jnp.int32, (rows, page), 0) % t
        attend(jnp.where((key <= query) & (key < t), s_new, NEG_INF), [v_new])
        out = acc_scr[...] / l_scr[...]
        out = jnp.where(row_head == lane_head, out, 0.0).reshape(n_heads, t, d)
        o_ref[...] = jnp.sum(out, axis=0)


def _paged_attention(q, kv_src, page_table, page, ck_past, ck_new, k_new, v_new, n_grp):
    n_seq, t, d = q.shape
    n_pages = page_table.shape[1]
    h = d // HEAD_DIM
    rows = h * t
    assert n_pages % n_grp == 0
    gathered = kv_src[0].ndim == 3
    cq_col = ck_new.transpose(0, 2, 1).reshape(n_seq, rows, 1)
    ckp_rows = ck_past.transpose(0, 2, 1)
    ckn_rows = jnp.pad(ck_new.transpose(0, 2, 1), ((0, 0), (0, 0), (0, page - t)))
    per_seq = lambda shape: pl.BlockSpec((None,) + shape, lambda b, g, pt: (b, 0, 0))
    dense_spec = pl.BlockSpec((None, n_grp * page, d), lambda b, g, pt: (b, g, 0))

    def page_spec(i):
        return pl.BlockSpec((None, page, h, HEAD_DIM), lambda b, g, pt: (pt[b, g * n_grp + i], 0, 0, 0))

    in_specs = [per_seq((t, d)), per_seq((rows, 1)),
                pl.BlockSpec((None, h, n_grp * page), lambda b, g, pt: (b, 0, g)),
                per_seq((h, page)), per_seq((t, d)), per_seq((t, d))]
    out_specs = [per_seq((t, d))]
    out_shape = [jax.ShapeDtypeStruct((n_seq, t, d), F32)]
    if gathered:
        in_specs += [dense_spec, dense_spec]
        kv_args = list(kv_src)
    else:
        in_specs += [page_spec(i) for i in range(n_grp)] * 2
        kv_args = [kv_src[0]] * n_grp + [kv_src[1]] * n_grp
        out_specs += [dense_spec, dense_spec]
        out_shape += [jax.ShapeDtypeStruct((n_seq, n_pages * page, d), BF16)] * 2
    return pl.pallas_call(
        functools.partial(_paged_kernel, n_grp, gathered),
        grid_spec=pltpu.PrefetchScalarGridSpec(
            num_scalar_prefetch=1, grid=(n_seq, n_pages // n_grp), in_specs=in_specs, out_specs=out_specs,
            scratch_shapes=[pltpu.VMEM((rows, d), BF16), pltpu.VMEM((rows, 1), F32), pltpu.VMEM((rows, 1), F32),
                            pltpu.VMEM((rows, d), F32)]),
        out_shape=out_shape,
        compiler_params=_cparams("parallel", "arbitrary"),
        name="fox_sample",
    )(page_table, q, cq_col, ckp_rows, ckn_rows, k_new, v_new, *kv_args)


def _largest_tile(n, cap):
    t = min(n, cap)
    while n % t:
        t //= 2
    return t


def _trunk(x3, mods, wkv0, shift0, past, w):
    n_seq, t, d = x3.shape
    n = n_seq * t
    depth = mods.shape[0]
    n_a = wkv0.shape[0]
    h = d // HEAD_DIM
    x = x3.reshape(n, d)
    per_seq = t >= 256
    if per_seq:
        tm_ffn, tm_mix = _largest_tile(t, 1024), _largest_tile(t, 256)
        tb = _largest_tile(t, 2 * LONG_CHUNK)
    else:
        tm_ffn = tm_mix = n
        tb = t

    def mod4(l, sub):
        m = mods[l, :, sub]
        if per_seq:
            return m.reshape(n_seq, 3, 1, d)
        return jnp.repeat(m, t, axis=0).transpose(1, 0, 2).reshape(1, 3, n, d)

    def tiles(tm):
        return max(t // tm, 1)

    v_first = None
    wkv_new, shift_new = [], []
    kv = None
    for l in range(depth):
        if l == n_a:
            k_new, v_new, kb, vb, logf = _shared_kv(x, w['g_kv'], w['w_kv'], w['w_f'], w['b_f'], tm_mix)
            if past is None:
                ident = jnp.arange(n // LANES, dtype=jnp.int32).reshape(n_seq, t // LANES)
                _, parts = _cumsum_pages(ident, logf.reshape(n // LANES, LANES, h), None, with_parts=True)
                parts = parts.reshape(n, LANES)
                kv = list(_kv_aug(kb, vb, parts, tm_mix)) + [parts]
            else:
                cache_k, cache_v, cache_logf, page_table = past
                ck_past, ck_new = _cumsum_pages(page_table, cache_logf, logf.reshape(n_seq, t, h))
                kv = [(cache_k, cache_v), page_table, cache_k.shape[1], ck_past, ck_new,
                      k_new.reshape(n_seq, t, d), v_new.reshape(n_seq, t, d)]
        gn = w['g_norm'][l]
        x = _ffn(x, mod4(l, 0), gn[0], w['w_ffn_in'][l, 0], w['w_ffn_out'][l, 0], tm_ffn, tiles(tm_ffn))
        m4 = mod4(l, 1)
        if l < n_a:
            p = {key: w[key + '_a'][l] for key in ('mu', 'w_rkv', 'w0', 'w1', 'w2', 'a0', 'a1', 'a2', 'g1', 'g2',
                                                   'kk', 'ka')}
            if l > 0:
                p.update({key: w[key + '_a'][l - 1] for key in ('v0', 'v1', 'v2')})
            hl = shift0[l].reshape(n_seq, 1, d) if per_seq else jnp.repeat(shift0[l], t, axis=0)
            r, lw, k, v, kk, a, gt, hmod = _rwkv_proj(x, m4, gn[1], hl, t, tm_mix, p, v_first if l > 0 else None)
            if l == 0:
                v_first = v
            o, s_new = _wkv_long(r, lw, k, v, kk, a, _pair_state(jnp.swapaxes(wkv0[l], -1, -2)),
                                 w['lnx_w_a'][l], w['lnx_b_a'][l], w['rk_a'][l], t, tb)
            wkv_new.append(jnp.swapaxes(_unpair_state(s_new), -1, -2))
            shift_new.append(hmod.reshape(n_seq, t, d)[:, -1])
            x = _out_proj(o, gt, x, m4, gn[1], w['w_o_a'][l], tm_mix, tiles(tm_mix))
        else:
            j = l - n_a
            if past is None:
                qa = _q_aug(x, m4, gn[1], w['w_q_b'][j], kv[2], tm_mix, tiles(tm_mix))
                o = _flash_prompt(qa, kv[0], kv[1], n_seq, _largest_tile(t, 512), 4)
            else:
                q = _q_proj(x, m4, gn[1], w['w_q_b'][j], tm_mix, tiles(tm_mix))
                res = _paged_attention(q.reshape(n_seq, t, d), *kv, _largest_tile(kv[1].shape[1], 4))
                o = res[0].reshape(n, d)
                if len(res) > 1:
                    kv[0] = tuple(res[1:])
            x = _out_proj(o, None, x, m4, gn[1], w['w_o_b'][j], tm_mix, tiles(tm_mix))
        x = _ffn(x, mod4(l, 2), gn[2], w['w_ffn_in'][l, 1], w['w_ffn_out'][l, 1], tm_ffn, tiles(tm_ffn))
    return (x.reshape(n_seq, t, d), jnp.stack(wkv_new), jnp.stack(shift_new), k_new.reshape(n_seq, t, h, HEAD_DIM),
            v_new.reshape(n_seq, t, h, HEAD_DIM), logf.reshape(n_seq, t, h))


def kernel(x_prompt, x_sample, state_wkv, state_shift, cache_k, cache_v, cache_logf, page_table, c_prompt, c_sample,
           w_mod, b_mod, g_norm, w_ffn_in, w_ffn_out, mu_a, w_rkv_a, w_o_a, w0_a, w1_a, w2_a, a0_a, a1_a, a2_a,
           v0_a, v1_a, v2_a, g1_a, g2_a, kk_a, ka_a, rk_a, lnx_w_a, lnx_b_a, g_kv, w_kv, w_f, b_f, w_q_b, w_o_b):
    bp, _, d = x_prompt.shape
    bs = x_sample.shape[0]
    depth = w_mod.shape[0]
    n_a = state_wkv.shape[0]
    n_pool, page = cache_k.shape[:2]
    cast = lambda a: a.astype(BF16)
    w = dict(g_norm=g_norm, w_ffn_in=w_ffn_in, w_ffn_out=w_ffn_out, mu_a=mu_a, w_rkv_a=cast(w_rkv_a),
             w_o_a=cast(w_o_a), w0_a=w0_a, w1_a=cast(w1_a), w2_a=cast(w2_a), a0_a=a0_a, a1_a=cast(a1_a),
             a2_a=cast(a2_a), v0_a=v0_a, v1_a=cast(v1_a), v2_a=cast(v2_a), g1_a=cast(g1_a), g2_a=cast(g2_a),
             kk_a=kk_a, ka_a=ka_a, rk_a=rk_a.reshape(n_a, d), lnx_w_a=lnx_w_a, lnx_b_a=lnx_b_a, g_kv=g_kv,
             w_kv=cast(w_kv), w_f=cast(w_f), b_f=b_f, w_q_b=cast(w_q_b), w_o_b=cast(w_o_b))

    mods = _mods(jnp.concatenate([c_prompt, c_sample], axis=0), w_mod, b_mod)
    mods = mods.reshape(depth, bp + bs, 3, 3, d)

    n_heads = d // HEAD_DIM
    zeros_wkv = jnp.zeros((n_a, bp, n_heads, HEAD_DIM, HEAD_DIM), F32)
    zeros_shift = jnp.zeros((n_a, bp, d), F32)
    y_p, wkv_p, shift_p, k_p, v_p, logf_p = _trunk(x_prompt, mods[:, :bp], zeros_wkv, zeros_shift, None, w)
    past = (cache_k, cache_v, cache_logf, page_table)
    y_s, wkv_s, shift_s, k_s, v_s, logf_s = _trunk(x_sample, mods[:, bp:], state_wkv, state_shift, past, w)
    return (y_p, y_s, wkv_p, shift_p, k_p, v_p, logf_p, wkv_s, shift_s, k_s, v_s, logf_s)
```

```python
import functools

import jax
import jax.numpy as jnp
import numpy as np
from jax import lax
from jax.experimental import pallas as pl
from jax.experimental.pallas import tpu as pltpu

F32 = jnp.float32
BF16 = jnp.bfloat16

NORM_EPS = 1e-6
GN_EPS = 64e-5
MACARON_W = 0.5
NEG_INF = -1e30
HEAD_DIM = 64
LANES = 128
PAIR = LANES // HEAD_DIM
VMEM_LIMIT = 56 * 1024 * 1024

LOG2E = 1.4426950408889634
N_BIAS = 3

_NT = (((1,), (1,)), ((), ()))
_TN = (((0,), (0,)), ((), ()))


def _cparams(*sem):
    return pltpu.CompilerParams(dimension_semantics=sem, vmem_limit_bytes=VMEM_LIMIT)


def _dot(a, b):
    return jnp.dot(a.astype(BF16), b.astype(BF16), preferred_element_type=F32)


def _dot_nt(a, b):
    return lax.dot_general(a.astype(BF16), b.astype(BF16), _NT, preferred_element_type=F32)


def _dot_tn(a, b):
    return lax.dot_general(a.astype(BF16), b.astype(BF16), _TN, preferred_element_type=F32)


def _split3(x):
    hi = x.astype(BF16)
    r1 = x - hi.astype(F32)
    mid = r1.astype(BF16)
    lo = (r1 - mid.astype(F32)).astype(BF16)
    return hi, mid, lo


def _dot3(sel, x):
    hi, mid, lo = _split3(x)
    out = jnp.dot(sel, hi, preferred_element_type=F32)
    out = out + jnp.dot(sel, mid, preferred_element_type=F32)
    return out + jnp.dot(sel, lo, preferred_element_type=F32)


def _dot3_rhs(x, sel):
    hi, mid, lo = _split3(x)
    out = jnp.dot(hi, sel, preferred_element_type=F32)
    out = out + jnp.dot(mid, sel, preferred_element_type=F32)
    return out + jnp.dot(lo, sel, preferred_element_type=F32)


def _rms(x, g):
    return x * lax.rsqrt(jnp.mean(x * x, axis=-1, keepdims=True) + NORM_EPS) * g


def _sigmoid(x):
    return 1.0 / (1.0 + jnp.exp(-x))


def _silu(x):
    return x * _sigmoid(x)


def _head_ones(n):
    r = lax.broadcasted_iota(jnp.int32, (n, n), 0) // HEAD_DIM
    c = lax.broadcasted_iota(jnp.int32, (n, n), 1) // HEAD_DIM
    return jnp.where(r == c, 1.0, 0.0).astype(BF16)


def _head_sum(x, ones):
    n = ones.shape[0]
    hi = x.astype(BF16)
    lo = (x - hi.astype(F32)).astype(BF16)
    cols = []
    for i in range(0, x.shape[1], n):
        s = jnp.dot(hi[:, i:i + n], ones, preferred_element_type=F32)
        cols.append(s + jnp.dot(lo[:, i:i + n], ones, preferred_element_type=F32))
    return cols[0] if len(cols) == 1 else jnp.concatenate(cols, axis=1)


def _mods_kernel(c_ref, w_ref, b_ref, o_ref):
    o_ref[...] = _dot(_silu(c_ref[...]), w_ref[...]) + b_ref[...]


def _mods(c_all, w_mod, b_mod):
    depth, d, n9 = w_mod.shape
    nb = c_all.shape[0]
    tn = d
    return pl.pallas_call(
        _mods_kernel,
        grid=(depth, n9 // tn),
        in_specs=[
            pl.BlockSpec((nb, d), lambda l, j: (0, 0)),
            pl.BlockSpec((None, d, tn), lambda l, j: (l, 0, j)),
            pl.BlockSpec((None, 1, tn), lambda l, j: (l, 0, j)),
        ],
        out_specs=pl.BlockSpec((None, nb, tn), lambda l, j: (l, 0, j)),
        out_shape=jax.ShapeDtypeStruct((depth, nb, n9), F32),
        compiler_params=_cparams("arbitrary", "arbitrary"),
        name="mods",
    )(c_all, w_mod, b_mod.reshape(depth, 1, n9))


def _mod_spec(m4, tm, tiles_per_seq):
    _, _, rows, d = m4.shape
    if rows == 1:
        return pl.BlockSpec((None, 3, 1, d), lambda i, *_: (i // tiles_per_seq, 0, 0, 0))
    return pl.BlockSpec((None, 3, tm, d), lambda i, *_: (0, 0, i, 0))


def _ffn_kernel(x_ref, m_ref, g_ref, wg_ref, wu_ref, wo_ref, o_ref, h_scr, acc_scr):
    j = pl.program_id(1)

    @pl.when(j == 0)
    def _():
        h = _rms(x_ref[...], g_ref[0:1]) * (1.0 + m_ref[1]) + m_ref[0]
        h_scr[...] = h.astype(BF16)
        acc_scr[...] = jnp.zeros_like(acc_scr)

    h = h_scr[...]
    gate = _dot(h, wg_ref[...])
    up = _dot(h, wu_ref[...])
    acc_scr[...] += _dot(_silu(gate) * up, wo_ref[...])

    @pl.when(j == pl.num_programs(1) - 1)
    def _():
        o_ref[...] = x_ref[...] + MACARON_W * m_ref[2] * _rms(acc_scr[...], g_ref[1:2])


def _ffn(x, m4, gn, w_in, w_out, tm, tiles_per_seq):
    n, d = x.shape
    f = w_out.shape[0]
    tf = 256 if f % 256 == 0 else LANES
    nf = f // tf
    return pl.pallas_call(
        _ffn_kernel,
        grid=(n // tm, nf),
        in_specs=[
            pl.BlockSpec((tm, d), lambda i, j: (i, 0)),
            _mod_spec(m4, tm, tiles_per_seq),
            pl.BlockSpec((2, d), lambda i, j: (0, 0)),
            pl.BlockSpec((d, tf), lambda i, j: (0, j)),
            pl.BlockSpec((d, tf), lambda i, j: (0, j + nf)),
            pl.BlockSpec((tf, d), lambda i, j: (j, 0)),
        ],
        out_specs=pl.BlockSpec((tm, d), lambda i, j: (i, 0)),
        out_shape=jax.ShapeDtypeStruct((n, d), F32),
        scratch_shapes=[pltpu.VMEM((tm, d), BF16), pltpu.VMEM((tm, d), F32)],
        compiler_params=_cparams("parallel", "arbitrary"),
        name="ffn",
    )(x, m4, gn, w_in, w_in, w_out)


def _rwkv_proj_kernel(has_vfirst, seq_len, x_ref, m_ref, g_ref, hl_ref, mu_ref, wrkv_ref, w0_ref, w1_ref,
                      w2_ref, a0_ref, a1_ref, a2_ref, g1_ref, g2_ref, kka_ref, kaa_ref, *rest):
    if has_vfirst:
        v0_ref, v1_ref, v2_ref, vf_ref = rest[:4]
        rest = rest[4:]
    r_ref, lw_ref, k_ref, v_ref, kk_ref, a_ref, gt_ref, h_ref, carry_scr = rest
    tm, d = x_ref.shape
    i = pl.program_id(0)

    h = _rms(x_ref[...], g_ref[0:1]) * (1.0 + m_ref[1]) + m_ref[0]
    h_ref[...] = h
    row = lax.broadcasted_iota(jnp.int32, (tm, d), 0)
    rolled = pltpu.roll(h, 1, 0)
    if hl_ref.shape[0] == 1:
        tiles_per_seq = seq_len // tm
        first = jnp.where(i % tiles_per_seq == 0, hl_ref[...], carry_scr[...])
        h_prev = jnp.where(row == 0, first, rolled)
        carry_scr[...] = h[tm - 1:tm, :]
    else:
        h_prev = jnp.where(row % seq_len == 0, hl_ref[...], rolled)
    xx = h_prev - h
    xr, xw, xk, xv, xa, xg = [h + xx * mu_ref[j:j + 1] for j in range(6)]

    r = _dot(xr, wrkv_ref[0])
    k = _dot(xk, wrkv_ref[1])
    v = _dot(xv, wrkv_ref[2])
    z = w0_ref[...] + _dot(jnp.tanh(_dot(xw, w1_ref[...])), w2_ref[...])
    lw = -_sigmoid(z) * jnp.exp(F32(-0.5))
    a = _sigmoid(a0_ref[...] + _dot(_dot(xa, a1_ref[...]), a2_ref[...]))
    gt = _dot(_sigmoid(_dot(xg, g1_ref[...])), g2_ref[...])
    if has_vfirst:
        mix = _sigmoid(v0_ref[...] + _dot(_dot(xv, v1_ref[...]), v2_ref[...]))
        v = v + (vf_ref[...] - v) * mix
    kk = k * kka_ref[...]
    ones = _head_ones(min(d, 2 * LANES))
    kk = kk * lax.rsqrt(jnp.maximum(_head_sum(kk * kk, ones), 1e-24))
    k = k * (1.0 + (a - 1.0) * kaa_ref[...])

    r_ref[...] = r
    lw_ref[...] = lw
    k_ref[...] = k
    v_ref[...] = v
    kk_ref[...] = kk
    a_ref[...] = a
    gt_ref[...] = gt


def _rwkv_proj(x, m4, gn, hl, seq_len, tm, p, v_first):
    n, d = x.shape
    tiles_per_seq = max(seq_len // tm, 1)
    has_vfirst = v_first is not None
    row = lambda a: a.reshape(1, d)
    const = lambda a: pl.BlockSpec(a.shape, lambda i: (0,) * a.ndim)
    tok = pl.BlockSpec((tm, d), lambda i: (i, 0))
    if hl.ndim == 3:
        hl_spec = pl.BlockSpec((None, 1, d), lambda i: (i // tiles_per_seq, 0, 0))
    else:
        hl_spec = tok
    args = [x, m4, gn, hl, p['mu'], p['w_rkv'], row(p['w0']), p['w1'], p['w2'], row(p['a0']), p['a1'], p['a2'],
            p['g1'], p['g2'], row(p['kk']), row(p['ka'])]
    specs = [tok, _mod_spec(m4, tm, tiles_per_seq), const(gn), hl_spec] + [const(a) for a in args[4:]]
    if has_vfirst:
        extra = [row(p['v0']), p['v1'], p['v2']]
        args += extra + [v_first]
        specs += [const(a) for a in extra] + [tok]
    return pl.pallas_call(
        functools.partial(_rwkv_proj_kernel, has_vfirst, seq_len),
        grid=(n // tm,),
        in_specs=specs,
        out_specs=[tok] * 8,
        out_shape=[jax.ShapeDtypeStruct((n, d), F32)] * 8,
        scratch_shapes=[pltpu.VMEM((1, d), F32)],
        compiler_params=_cparams("arbitrary"),
        name="rwkv_proj",
    )(*args)


LONG_CHUNK = 64


def _long_masks():
    n2 = PAIR * LONG_CHUNK
    r = np.arange(n2)[:, None]
    c = np.arange(n2)[None, :]
    masks = [c < r, c <= r, c == r]
    b = 1
    while b < LONG_CHUNK:
        masks.append((r // (2 * b) == c // (2 * b)) & ((r // b) % 2 == 1) & ((c // b) % 2 == 0))
        b *= 2
    return jnp.asarray(np.stack(masks).astype(np.float32))


def _wkv_long_kernel(r_ref, lw_ref, k_ref, v_ref, kk_ref, a_ref, s0_ref, lnw_ref, lnb_ref, rk_ref, mask_ref,
                     o_ref, sout_ref, st_scr, wq_scr, bk_scr, lkk_scr, v2_scr, kt_scr, bt_scr, lrb_scr, lvt_scr,
                     x_scr, gc_scr, lab_scr, tinv_scr, ol_scr, ds_scr, ut_scr):
    c = LONG_CHUNK
    n2 = PAIR * c
    tb, d = r_ref.shape
    n_pairs = d // LANES
    t_idx = pl.program_id(1)

    @pl.when(t_idx == 0)
    def _():
        st_scr[...] = s0_ref[...]

    rows_in = min(tb, c)
    n_chunks = max(tb // c, 1)
    items = [(ic, p) for ic in range(n_chunks) for p in range(n_pairs)]
    dotf = lambda x, y: jnp.dot(x, y, preferred_element_type=F32)
    lane = lax.broadcasted_iota(jnp.int32, (c, LANES), 1)
    head_lane = [lane // HEAD_DIM == h for h in range(PAIR)]
    tri = jnp.where(lax.broadcasted_iota(jnp.int32, (c, c), 1) <= lax.broadcasted_iota(jnp.int32, (c, c), 0),
                    1.0, 0.0).astype(BF16)

    def stack(x):
        return jnp.concatenate([jnp.where(m, x, 0.0) for m in head_lane], axis=0)

    for i, (ic, p) in enumerate(items):
        rows, ls = slice(ic * c, ic * c + rows_in), slice(p * LANES, (p + 1) * LANES)
        rc, lwc, kc, vc, kkc, ac = [ref[rows, ls] for ref in (r_ref, lw_ref, k_ref, v_ref, kk_ref, a_ref)]
        if rows_in < c:
            pad = jnp.zeros((c - rows_in, LANES), F32)
            rc, lwc, kc, vc, kkc, ac = [jnp.concatenate([x, pad], axis=0) for x in (rc, lwc, kc, vc, kkc, ac)]
        cum = _dot3(tri, lwc)
        g = jnp.exp(cum)
        gi = jnp.exp(-cum)
        g_end = g[c - 1:c, :]
        bb = ac * kkc * gi
        kb = kc * gi
        wq_scr[i, :n2] = stack(-kkc * jnp.exp(cum - lwc)).astype(BF16)
        wq_scr[i, n2:] = stack(rc * g).astype(BF16)
        bk_scr[i, :n2] = stack(bb).astype(BF16)
        bk_scr[i, n2:] = stack(kb).astype(BF16)
        v2_scr[i] = stack(vc).astype(BF16)
        kt_scr[i] = stack(kb * g_end).astype(BF16)
        bt_scr[i] = stack(bb * g_end).T.astype(BF16)
        gc_scr[i] = jnp.broadcast_to(g_end, (LANES, LANES)).T

    for i in range(len(items)):
        scores = lax.dot_general(wq_scr[i], bk_scr[i], _NT, preferred_element_type=F32)
        l_ab = scores[:n2, :n2] * mask_ref[0]
        lab_scr[i] = l_ab
        tinv_scr[i] = mask_ref[2] + l_ab * mask_ref[3]
        lkk_scr[i, :n2] = (scores[:n2, n2:] * mask_ref[0]).astype(BF16)
        lkk_scr[i, n2:] = (scores[n2:, n2:] * mask_ref[1]).astype(BF16)
        lrb_scr[i] = (scores[n2:, :n2] * mask_ref[1]).astype(BF16)

    for i in range(len(items)):
        lv = dotf(lkk_scr[i], v2_scr[i])
        lvt_scr[i] = lv[:n2].astype(BF16)
        ol_scr[i] = lv[n2:]
        ds_scr[i] = lax.dot_general(kt_scr[i], v2_scr[i], _TN, preferred_element_type=F32)

    for level in range(4, mask_ref.shape[0]):
        for i in range(len(items)):
            x_scr[i] = dotf((lab_scr[i] * mask_ref[level]).astype(BF16), tinv_scr[i].astype(BF16)).astype(BF16)
        for i in range(len(items)):
            tinv = tinv_scr[i]
            tinv_scr[i] = tinv + dotf(tinv.astype(BF16), x_scr[i])

    for i in range(len(items)):
        tinv = tinv_scr[i].astype(BF16)
        ut_scr[i] = dotf(tinv, lvt_scr[i])
        wq_scr[i, :n2] = dotf(tinv, wq_scr[i, :n2]).astype(BF16)

    def advance(ic, carry):
        rows = pl.ds(pl.multiple_of(ic * c, c), rows_in)
        for p in range(n_pairs):
            i = ic * n_pairs + p
            st = st_scr[p]
            ws = dotf(wq_scr[i], st.astype(BF16))
            u2 = (ws[:n2] + ut_scr[i]).astype(BF16)
            o2 = ws[n2:] + dotf(lrb_scr[i], u2) + ol_scr[i]
            st_scr[p] = gc_scr[i] * st + dotf(bt_scr[i], u2) + ds_scr[i]
            o = o2[0:c]
            for h in range(1, PAIR):
                o = o + o2[h * c:(h + 1) * c]
            o_ref[rows, p * LANES:(p + 1) * LANES] = o[:rows_in]
        return carry

    lax.fori_loop(0, n_chunks, advance, 0)

    ones = _head_ones(min(d, 2 * LANES))
    o = o_ref[...]
    cen = o - _head_sum(o, ones) * (1.0 / HEAD_DIM)
    var = _head_sum(cen * cen, ones) * (1.0 / HEAD_DIM)
    bonus = _head_sum(r_ref[...] * k_ref[...] * rk_ref[...], ones) * v_ref[...]
    o_ref[...] = cen * lax.rsqrt(var + GN_EPS) * lnw_ref[...] + lnb_ref[...] + bonus

    @pl.when(t_idx == pl.num_programs(1) - 1)
    def _():
        sout_ref[...] = st_scr[...]


def _wkv_long(r, lw, k, v, kk, a, s0t, lnw, lnb, rk, seq_len, tb):
    n, d = r.shape
    n_seq = n // seq_len
    n_pairs = d // LANES
    nt = seq_len // tb
    n_items = max(tb // LONG_CHUNK, 1) * n_pairs
    n2 = PAIR * LONG_CHUNK
    assert n2 == LANES
    masks = _long_masks()
    tok = pl.BlockSpec((tb, d), lambda b, t: (b * nt + t, 0))
    vec = pl.BlockSpec((1, d), lambda b, t: (0, 0))
    st = pl.BlockSpec((None, n_pairs, LANES, LANES), lambda b, t: (b, 0, 0, 0))
    per = lambda rows, dt: pltpu.VMEM((n_items, rows, LANES), dt)
    return pl.pallas_call(
        _wkv_long_kernel,
        grid=(n_seq, nt),
        in_specs=[tok] * 6 + [st, vec, vec, vec, pl.BlockSpec(masks.shape, lambda b, t: (0, 0, 0))],
        out_specs=[tok, st],
        out_shape=[jax.ShapeDtypeStruct((n, d), F32), jax.ShapeDtypeStruct(s0t.shape, F32)],
        scratch_shapes=[pltpu.VMEM((n_pairs, LANES, LANES), F32)]
        + [per(2 * n2, BF16)] * 3 + [per(n2, BF16)] * 6 + [per(n2, F32)] * 6,
        compiler_params=_cparams("parallel", "arbitrary"),
        name="wkv_long",
    )(r, lw, k, v, kk, a, s0t, lnw.reshape(1, d), lnb.reshape(1, d), rk.reshape(1, d), masks)


def _pair_state(s):
    n_seq, h, n, _ = s.shape
    s = s.reshape(n_seq, h // PAIR, PAIR, n, n)
    eye = jnp.eye(PAIR, dtype=s.dtype)
    out = s[:, :, :, :, None, :] * eye[None, None, :, None, :, None]
    return out.reshape(n_seq, h // PAIR, PAIR * n, PAIR * n)


def _unpair_state(sp):
    n_seq, hp, n2, _ = sp.shape
    n = n2 // PAIR
    s = sp.reshape(n_seq, hp, PAIR, n, PAIR, n)
    return jnp.stack([s[:, :, a, :, a, :] for a in range(PAIR)], axis=2).reshape(n_seq, hp * PAIR, n, n)


def _out_kernel(has_gate, res_w, *refs):
    if has_gate:
        a_ref, gt_ref, x_ref, m_ref, g_ref, w_ref, o_ref = refs
        a = a_ref[...] * gt_ref[...]
    else:
        a_ref, x_ref, m_ref, g_ref, w_ref, o_ref = refs
        a = a_ref[...]
    y = _dot(a, w_ref[...])
    o_ref[...] = x_ref[...] + res_w * m_ref[2] * _rms(y, g_ref[1:2])


def _out_proj(a, gate, x, m4, gn, w, tm, tiles_per_seq, res_w=1.0):
    n, d = x.shape
    tok = pl.BlockSpec((tm, d), lambda i: (i, 0))
    args = [a] + ([gate] if gate is not None else []) + [x, m4, gn, w]
    specs = [tok] * (len(args) - 3) + [_mod_spec(m4, tm, tiles_per_seq), pl.BlockSpec((2, d), lambda i: (0, 0)),
                                       pl.BlockSpec(w.shape, lambda i: (0, 0))]
    return pl.pallas_call(
        functools.partial(_out_kernel, gate is not None, res_w),
        grid=(n // tm,),
        in_specs=specs,
        out_specs=tok,
        out_shape=jax.ShapeDtypeStruct((n, d), F32),
        compiler_params=_cparams("parallel"),
        name="out_proj",
    )(*args)


def _kv_kernel(x_ref, g_ref, wkv_ref, wf_ref, bf_ref, k_ref, v_ref, kb_ref, vb_ref, lf_ref):
    d = x_ref.shape[1]
    s = _rms(x_ref[...], g_ref[...])
    kv = _dot(s, wkv_ref[...])
    k = kv[:, :d]
    v = kv[:, d:]
    k_ref[...] = k
    v_ref[...] = v
    kb_ref[...] = k.astype(BF16)
    vb_ref[...] = v.astype(BF16)
    z = _dot(s, wf_ref[...]) + bf_ref[...]
    lf_ref[...] = jnp.minimum(z, 0.0) - jnp.log(1.0 + jnp.exp(-jnp.abs(z)))


def _shared_kv(x, g_kv, w_kv, w_f, b_f, tm):
    n, d = x.shape
    h = w_f.shape[1]
    tok = pl.BlockSpec((tm, d), lambda i: (i, 0))
    return pl.pallas_call(
        _kv_kernel,
        grid=(n // tm,),
        in_specs=[tok, pl.BlockSpec((1, d), lambda i: (0, 0)), pl.BlockSpec(w_kv.shape, lambda i: (0, 0)),
                  pl.BlockSpec(w_f.shape, lambda i: (0, 0)), pl.BlockSpec((1, h), lambda i: (0, 0))],
        out_specs=[tok, tok, tok, tok, pl.BlockSpec((tm, h), lambda i: (i, 0))],
        out_shape=[jax.ShapeDtypeStruct((n, d), F32)] * 2 + [jax.ShapeDtypeStruct((n, d), BF16)] * 2
        + [jax.ShapeDtypeStruct((n, h), F32)],
        compiler_params=_cparams("parallel"),
        name="shared_kv",
    )(x, g_kv.reshape(1, d), w_kv, w_f, b_f.reshape(1, h))


def _cumsum_kernel(n_grp, has_new, has_parts, pt_ref, *refs):
    lf_refs = refs[:n_grp]
    rest = refs[n_grp:]
    if has_new:
        new_ref, ck_ref, cknew_ref, carry_scr = rest
    elif has_parts:
        ck_ref, parts_ref, carry_scr = rest
    else:
        ck_ref, carry_scr = rest
    del pt_ref
    j = pl.program_id(1)
    page, h = lf_refs[0].shape

    @pl.when(j == 0)
    def _():
        carry_scr[...] = jnp.zeros_like(carry_scr)

    ri = lax.broadcasted_iota(jnp.int32, (page, page), 0)
    ci = lax.broadcasted_iota(jnp.int32, (page, page), 1)
    tri = jnp.where(ci <= ri, 1.0, 0.0).astype(BF16)
    local = [_dot3(tri, lf[...]) for lf in lf_refs]
    base = carry_scr[...]
    for i in range(n_grp):
        ck = base + local[i]
        ck_ref[i * page:(i + 1) * page, :] = ck
        base = ck[page - 1:page, :]

        if has_parts:
            src_h = lax.broadcasted_iota(jnp.int32, (h, LANES), 0)
            dst = lax.broadcasted_iota(jnp.int32, (h, LANES), 1)
            out = jnp.where(lax.broadcasted_iota(jnp.int32, (page, LANES), 1) // h == N_BIAS, 1.0, 0.0)
            for x, part in enumerate(_split3(ck * LOG2E)):
                out = out + jnp.dot(part, jnp.where(dst == x * h + src_h, 1.0, 0.0).astype(BF16),
                                    preferred_element_type=F32)
            parts_ref[i * page:(i + 1) * page, :] = out.astype(BF16)
    carry_scr[...] = base

    if has_new:
        @pl.when(j == pl.num_programs(1) - 1)
        def _():
            t = new_ref.shape[0]
            cknew_ref[...] = base + _dot3(tri[:t, :t], new_ref[...])


def _cumsum_pages(page_table, lf_pages, lf_new, with_parts=False):
    n_seq, n_pages = page_table.shape
    _, page, h = lf_pages.shape
    has_new = lf_new is not None
    n_grp = _largest_tile(n_pages, 8)

    def page_spec(i):
        return pl.BlockSpec((None, page, h), lambda b, j, pt: (pt[b, j * n_grp + i], 0, 0))

    in_specs = [page_spec(i) for i in range(n_grp)]
    out_specs = [pl.BlockSpec((None, n_grp * page, h), lambda b, j, pt: (b, j, 0))]
    out_shape = [jax.ShapeDtypeStruct((n_seq, n_pages * page, h), F32)]
    args = [lf_pages] * n_grp
    if has_new:
        t = lf_new.shape[1]
        in_specs.append(pl.BlockSpec((None, t, h), lambda b, j, pt: (b, 0, 0)))
        out_specs.append(pl.BlockSpec((None, t, h), lambda b, j, pt: (b, 0, 0)))
        out_shape.append(jax.ShapeDtypeStruct((n_seq, t, h), F32))
        args.append(lf_new)
    elif with_parts:
        out_specs.append(pl.BlockSpec((None, n_grp * page, LANES), lambda b, j, pt: (b, j, 0)))
        out_shape.append(jax.ShapeDtypeStruct((n_seq, n_pages * page, LANES), BF16))
    return pl.pallas_call(
        functools.partial(_cumsum_kernel, n_grp, has_new, with_parts and not has_new),
        grid_spec=pltpu.PrefetchScalarGridSpec(
            num_scalar_prefetch=1, grid=(n_seq, n_pages // n_grp), in_specs=in_specs, out_specs=out_specs,
            scratch_shapes=[pltpu.VMEM((1, h), F32)]),
        out_shape=out_shape,
        compiler_params=_cparams("parallel", "arbitrary"),
        name="cumsum",
    )(page_table, *args)


def _q_kernel(x_ref, m_ref, g_ref, w_ref, q_ref):
    h = _rms(x_ref[...], g_ref[0:1]) * (1.0 + m_ref[1]) + m_ref[0]
    q_ref[...] = (_dot(h, w_ref[...]) * (HEAD_DIM ** -0.5)).astype(BF16)


def _q_proj(x, m4, gn, w, tm, tiles_per_seq):
    n, d = x.shape
    tok = pl.BlockSpec((tm, d), lambda i: (i, 0))
    return pl.pallas_call(
        _q_kernel,
        grid=(n // tm,),
        in_specs=[tok, _mod_spec(m4, tm, tiles_per_seq), pl.BlockSpec((2, d), lambda i: (0, 0)),
                  pl.BlockSpec(w.shape, lambda i: (0, 0))],
        out_specs=tok,
        out_shape=jax.ShapeDtypeStruct((n, d), BF16),
        compiler_params=_cparams("parallel"),
        name="q_proj",
    )(x, m4, gn, w)


def _pair_to_tiles():
    m = np.zeros((LANES, PAIR * LANES), np.float32)
    r = np.arange(LANES)
    m[r, (r // HEAD_DIM) * LANES + r % HEAD_DIM] = 1.0
    return jnp.asarray(m, BF16)


def _bias_place(h, for_keys):
    m = np.zeros((LANES, h * LANES), np.float32)
    for hh in range(h):
        for x in range(N_BIAS):
            if for_keys:
                m[N_BIAS * h + hh, hh * LANES + HEAD_DIM + x] = 1.0
                m[x * h + hh, hh * LANES + HEAD_DIM + N_BIAS + x] = -1.0
            else:
                m[x * h + hh, hh * LANES + HEAD_DIM + x] = 1.0
                m[N_BIAS * h + hh, hh * LANES + HEAD_DIM + N_BIAS + x] = 1.0
    return jnp.asarray(m, BF16)


def _to_head_tiles(x, place, extra):
    cols = []
    for p in range(x.shape[1] // LANES):
        sl = slice(p * PAIR * LANES, (p + 1) * PAIR * LANES)
        cols.append(jnp.dot(x[:, p * LANES:(p + 1) * LANES], place, preferred_element_type=F32) + extra[:, sl])
    return jnp.concatenate(cols, axis=1)


def _q_aug_kernel(x_ref, m_ref, g_ref, w_ref, parts_ref, place_ref, bias_ref, q_ref):
    h = _rms(x_ref[...], g_ref[0:1]) * (1.0 + m_ref[1]) + m_ref[0]
    q = (_dot(h, w_ref[...]) * (HEAD_DIM ** -0.5 * LOG2E)).astype(BF16)
    bias = jnp.dot(parts_ref[...], bias_ref[...], preferred_element_type=F32)
    q_ref[...] = _to_head_tiles(q, place_ref[...], bias).astype(BF16)


def _q_aug(x, m4, gn, w, parts, tm, tiles_per_seq):
    n, d = x.shape
    tok = pl.BlockSpec((tm, d), lambda i: (i, 0))
    place, bias = _pair_to_tiles(), _bias_place(d // HEAD_DIM, False)
    const = lambda a: pl.BlockSpec(a.shape, lambda i: (0, 0))
    return pl.pallas_call(
        _q_aug_kernel,
        grid=(n // tm,),
        in_specs=[tok, _mod_spec(m4, tm, tiles_per_seq), pl.BlockSpec((2, d), lambda i: (0, 0)), const(w),
                  pl.BlockSpec((tm, LANES), lambda i: (i, 0)), const(place), const(bias)],
        out_specs=pl.BlockSpec((tm, PAIR * d), lambda i: (i, 0)),
        out_shape=jax.ShapeDtypeStruct((n, PAIR * d), BF16),
        compiler_params=_cparams("parallel"),
        name="q_aug",
    )(x, m4, gn, w, parts, place, bias)


def _kv_aug_kernel(k_ref, v_ref, parts_ref, place_ref, bias_ref, ka_ref, va_ref):
    tm, d2 = ka_ref.shape
    bias = jnp.dot(parts_ref[...], bias_ref[...], preferred_element_type=F32)
    ka_ref[...] = _to_head_tiles(k_ref[...], place_ref[...], bias).astype(BF16)
    ones = jnp.where(lax.broadcasted_iota(jnp.int32, (1, d2), 1) % LANES >= HEAD_DIM, 1.0, 0.0)
    va_ref[...] = _to_head_tiles(v_ref[...], place_ref[...], ones).astype(BF16)


def _kv_aug(kb, vb, parts, tm):
    n, d = kb.shape
    tok = pl.BlockSpec((tm, d), lambda i: (i, 0))
    wide = pl.BlockSpec((tm, PAIR * d), lambda i: (i, 0))
    place, bias = _pair_to_tiles(), _bias_place(d // HEAD_DIM, True)
    const = lambda a: pl.BlockSpec(a.shape, lambda i: (0, 0))
    return pl.pallas_call(
        _kv_aug_kernel,
        grid=(n // tm,),
        in_specs=[tok, tok, pl.BlockSpec((tm, LANES), lambda i: (i, 0)), const(place), const(bias)],
        out_specs=[wide, wide],
        out_shape=[jax.ShapeDtypeStruct((n, PAIR * d), BF16)] * 2,
        compiler_params=_cparams("parallel"),
        name="kv_aug",
    )(kb, vb, parts, place, bias)


def _flash_kernel(qi_ref, kj_ref, q_ref, k_ref, v_ref, o_ref, m_scr, acc_scr):
    n = pl.program_id(2)
    qi = qi_ref[n]
    kj = kj_ref[n]
    tq = q_ref.shape[0]
    tk = k_ref.shape[0]
    n_heads = q_ref.shape[1] // LANES

    @pl.when(kj == 0)
    def _():
        m_scr[...] = jnp.full_like(m_scr, NEG_INF)
        acc_scr[...] = jnp.zeros_like(acc_scr)

    def update(masked):
        if masked:
            valid = (lax.broadcasted_iota(jnp.int32, (tq, tk), 1)
                     <= lax.broadcasted_iota(jnp.int32, (tq, tk), 0))
        for h in range(n_heads):
            sl = slice(h * LANES, (h + 1) * LANES)
            s = lax.dot_general(q_ref[:, sl], k_ref[:, sl], _NT, preferred_element_type=F32)
            if masked:
                s = jnp.where(valid, s, NEG_INF)
            cols = [s[:, c * LANES:(c + 1) * LANES] for c in range(tk // LANES)]
            m_cur = cols[0]
            for col in cols[1:]:
                m_cur = jnp.maximum(m_cur, col)
            m_old = m_scr[h]
            m_new = jnp.maximum(m_old, jnp.max(m_cur, axis=1, keepdims=True))
            p = jnp.concatenate([jnp.exp2(col - m_new) for col in cols], axis=1).astype(BF16)
            acc_scr[h] = jnp.exp2(m_old - m_new) * acc_scr[h] + jnp.dot(p, v_ref[:, sl],
                                                                        preferred_element_type=F32)
            m_scr[h] = m_new

    @pl.when(kj < qi)
    def _():
        update(False)

    @pl.when(kj == qi)
    def _():
        update(True)
        lane = lax.broadcasted_iota(jnp.int32, (tq, LANES), 1)
        outs = []
        for h in range(n_heads):
            acc = acc_scr[h]
            outs.append(acc / pltpu.roll(acc, HEAD_DIM, 1))
        for p in range(n_heads // PAIR):
            o_ref[:, p * LANES:(p + 1) * LANES] = jnp.where(
                lane < HEAD_DIM, outs[PAIR * p], pltpu.roll(outs[PAIR * p + 1], HEAD_DIM, 1))


def _flash_prompt(qa, ka, va, n_seq, tq, heads_per_step):
    n, d2 = qa.shape
    t = n // n_seq
    nq = t // tq
    width = heads_per_step * LANES
    qi = jnp.asarray([i for i in range(nq) for _ in range(i + 1)], jnp.int32)
    kj = jnp.asarray([j for i in range(nq) for j in range(i + 1)], jnp.int32)
    return pl.pallas_call(
        _flash_kernel,
        grid_spec=pltpu.PrefetchScalarGridSpec(
            num_scalar_prefetch=2,
            grid=(n_seq, d2 // width, int(qi.shape[0])),
            in_specs=[
                pl.BlockSpec((tq, width), lambda b, p, s, qi, kj: (b * nq + qi[s], p)),
                pl.BlockSpec((tq, width), lambda b, p, s, qi, kj: (b * nq + kj[s], p)),
                pl.BlockSpec((tq, width), lambda b, p, s, qi, kj: (b * nq + kj[s], p)),
            ],
            out_specs=pl.BlockSpec((tq, width // PAIR), lambda b, p, s, qi, kj: (b * nq + qi[s], p)),
            scratch_shapes=[pltpu.VMEM((heads_per_step, tq, LANES), F32),
                            pltpu.VMEM((heads_per_step, tq, LANES), F32)]),
        out_shape=jax.ShapeDtypeStruct((n, d2 // PAIR), F32),
        compiler_params=_cparams("parallel", "parallel", "arbitrary"),
        name="fox_prompt",
    )(qi, kj, qa, ka, va)


def _paged_kernel(n_grp, gathered, pt_ref, q_ref, cq_ref, ckp_ref, ckn_ref, kn_ref, vn_ref, *rest):
    t, d = q_ref.shape
    n_heads = d // HEAD_DIM
    rows = n_heads * t
    if gathered:
        kd_ref, vd_ref, o_ref, qbd_scr, m_scr, l_scr, acc_scr = rest
    else:
        k_refs = rest[:n_grp]
        v_refs = rest[n_grp:2 * n_grp]
        o_ref, kd_ref, vd_ref, qbd_scr, m_scr, l_scr, acc_scr = rest[2 * n_grp:]
    page = kd_ref.shape[1] // n_grp
    del pt_ref
    g = pl.program_id(1)

    row_head = lax.broadcasted_iota(jnp.int32, (rows, d), 0) // t
    lane_head = lax.broadcasted_iota(jnp.int32, (rows, d), 1) // HEAD_DIM
    expand = jnp.where(lax.broadcasted_iota(jnp.int32, (rows, n_heads), 0) // t
                       == lax.broadcasted_iota(jnp.int32, (rows, n_heads), 1), 1.0, 0.0).astype(BF16)

    @pl.when(g == 0)
    def _():
        q = q_ref[...].astype(F32)
        q_rows = jnp.concatenate([q] * n_heads, axis=0)
        qbd_scr[...] = jnp.where(row_head == lane_head, q_rows, 0.0).astype(BF16)
        m_scr[...] = jnp.full_like(m_scr, NEG_INF)
        l_scr[...] = jnp.zeros_like(l_scr)
        acc_scr[...] = jnp.zeros_like(acc_scr)

    qbd = qbd_scr[...]
    cq = cq_ref[...]

    def attend(s, weighted_values):
        m_old = m_scr[...]
        m_new = jnp.maximum(m_old, jnp.max(s, axis=1, keepdims=True))
        alpha = jnp.exp(m_old - m_new)
        p = jnp.exp(s - m_new)
        l_scr[...] = alpha * l_scr[...] + jnp.sum(p, axis=1, keepdims=True)
        acc_scr[...] = alpha * acc_scr[...] + weighted_values(p)
        m_scr[...] = m_new

    if gathered:
        kts = [kd_ref[:, i * page:(i + 1) * page] for i in range(n_grp)]
        vts = [vd_ref[:, i * page:(i + 1) * page] for i in range(n_grp)]
    else:
        kts = [kr[...].astype(BF16) for kr in k_refs]
        vts = [vr[...].astype(BF16) for vr in v_refs]
        for i in range(n_grp):
            kd_ref[:, i * page:(i + 1) * page] = kts[i]
            vd_ref[:, i * page:(i + 1) * page] = vts[i]

    def past_values(p):
        pv = _dot_nt(p[:, 0:page], vts[0])
        for i in range(1, n_grp):
            pv = pv + _dot_nt(p[:, i * page:(i + 1) * page], vts[i])
        return pv

    s = jnp.concatenate([_dot(qbd, kt) for kt in kts], axis=1)
    s = s + (cq - _dot3(expand, ckp_ref[...]))
    attend(s, past_values)

    @pl.when(g == pl.num_programs(1) - 1)
    def _():
        pad = jnp.zeros((page - t, d), F32)
        k_new = jnp.concatenate([kn_ref[...], pad], axis=0)
        v_new = jnp.concatenate([vn_ref[...], pad], axis=0)
        s_new = _dot_nt(qbd, k_new) + (cq - _dot3(expand, ckn_ref[...]))
        key = lax.broadcasted_iota(jnp.int32, (rows, page), 1)
        query = lax.broadcasted_iota(jnp.int32, (rows, page), 0) % t
        attend(jnp.where((key <= query) & (key < t), s_new, NEG_INF), lambda p: _dot(p, v_new))
        out = acc_scr[...] / l_scr[...]
        out = jnp.where(row_head == lane_head, out, 0.0).reshape(n_heads, t, d)
        o_ref[...] = jnp.sum(out, axis=0)


def _paged_attention(q, kv_src, page_table, page, ck_past, ck_new, k_new, v_new, n_grp):
    n_seq, t, d = q.shape
    n_pages = page_table.shape[1]
    h = d // HEAD_DIM
    rows = h * t
    assert n_pages % n_grp == 0
    gathered = kv_src[0].ndim == 3
    cq_col = ck_new.transpose(0, 2, 1).reshape(n_seq, rows, 1)
    ckp_rows = ck_past.transpose(0, 2, 1)
    ckn_rows = jnp.pad(ck_new.transpose(0, 2, 1), ((0, 0), (0, 0), (0, page - t)))
    per_seq = lambda shape: pl.BlockSpec((None,) + shape, lambda b, g, pt: (b, 0, 0))
    dense_spec = pl.BlockSpec((None, d, n_grp * page), lambda b, g, pt: (b, 0, g))

    def page_spec(i):
        return pl.BlockSpec((None, d, page), lambda b, g, pt: (pt[b, g * n_grp + i], 0, 0))

    in_specs = [per_seq((t, d)), per_seq((rows, 1)),
                pl.BlockSpec((None, h, n_grp * page), lambda b, g, pt: (b, 0, g)),
                per_seq((h, page)), per_seq((t, d)), per_seq((t, d))]
    out_specs = [per_seq((t, d))]
    out_shape = [jax.ShapeDtypeStruct((n_seq, t, d), F32)]
    if gathered:
        in_specs += [dense_spec, dense_spec]
        kv_args = list(kv_src)
    else:
        caches = [jnp.transpose(c, (0, 2, 3, 1)).reshape(c.shape[0], d, page) for c in kv_src]
        in_specs += [page_spec(i) for i in range(n_grp)] * 2
        kv_args = [caches[0]] * n_grp + [caches[1]] * n_grp
        out_specs += [dense_spec, dense_spec]
        out_shape += [jax.ShapeDtypeStruct((n_seq, d, n_pages * page), BF16)] * 2
    return pl.pallas_call(
        functools.partial(_paged_kernel, n_grp, gathered),
        grid_spec=pltpu.PrefetchScalarGridSpec(
            num_scalar_prefetch=1, grid=(n_seq, n_pages // n_grp), in_specs=in_specs, out_specs=out_specs,
            scratch_shapes=[pltpu.VMEM((rows, d), BF16), pltpu.VMEM((rows, 1), F32), pltpu.VMEM((rows, 1), F32),
                            pltpu.VMEM((rows, d), F32)]),
        out_shape=out_shape,
        compiler_params=_cparams("parallel", "arbitrary"),
        name="fox_sample",
    )(page_table, q, cq_col, ckp_rows, ckn_rows, k_new, v_new, *kv_args)


def _largest_tile(n, cap):
    t = min(n, cap)
    while n % t:
        t //= 2
    return t


def _trunk(x3, mods, wkv0, shift0, past, w):
    n_seq, t, d = x3.shape
    n = n_seq * t
    depth = mods.shape[0]
    n_a = wkv0.shape[0]
    h = d // HEAD_DIM
    x = x3.reshape(n, d)
    per_seq = t >= 256
    if per_seq:
        tm_ffn, tm_mix = _largest_tile(t, 1024), _largest_tile(t, 256)
        tb = _largest_tile(t, 2 * LONG_CHUNK)
    else:
        tm_ffn = tm_mix = n
        tb = t

    def mod4(l, sub):
        m = mods[l, :, sub]
        if per_seq:
            return m.reshape(n_seq, 3, 1, d)
        return jnp.repeat(m, t, axis=0).transpose(1, 0, 2).reshape(1, 3, n, d)

    def tiles(tm):
        return max(t // tm, 1)

    v_first = None
    wkv_new, shift_new = [], []
    kv = None
    for l in range(depth):
        if l == n_a:
            k_new, v_new, kb, vb, logf = _shared_kv(x, w['g_kv'], w['w_kv'], w['w_f'], w['b_f'], tm_mix)
            if past is None:
                ident = jnp.arange(n // LANES, dtype=jnp.int32).reshape(n_seq, t // LANES)
                _, parts = _cumsum_pages(ident, logf.reshape(n // LANES, LANES, h), None, with_parts=True)
                parts = parts.reshape(n, LANES)
                kv = list(_kv_aug(kb, vb, parts, tm_mix)) + [parts]
            else:
                cache_k, cache_v, cache_logf, page_table = past
                ck_past, ck_new = _cumsum_pages(page_table, cache_logf, logf.reshape(n_seq, t, h))
                kv = [(cache_k, cache_v), page_table, cache_k.shape[1], ck_past, ck_new,
                      k_new.reshape(n_seq, t, d), v_new.reshape(n_seq, t, d)]
        gn = w['g_norm'][l]
        x = _ffn(x, mod4(l, 0), gn[0], w['w_ffn_in'][l, 0], w['w_ffn_out'][l, 0], tm_ffn, tiles(tm_ffn))
        m4 = mod4(l, 1)
        if l < n_a:
            p = {key: w[key + '_a'][l] for key in ('mu', 'w_rkv', 'w0', 'w1', 'w2', 'a0', 'a1', 'a2', 'g1', 'g2',
                                                   'kk', 'ka')}
            if l > 0:
                p.update({key: w[key + '_a'][l - 1] for key in ('v0', 'v1', 'v2')})
            hl = shift0[l].reshape(n_seq, 1, d) if per_seq else jnp.repeat(shift0[l], t, axis=0)
            r, lw, k, v, kk, a, gt, hmod = _rwkv_proj(x, m4, gn[1], hl, t, tm_mix, p, v_first if l > 0 else None)
            if l == 0:
                v_first = v
            o, s_new = _wkv_long(r, lw, k, v, kk, a, _pair_state(jnp.swapaxes(wkv0[l], -1, -2)),
                                 w['lnx_w_a'][l], w['lnx_b_a'][l], w['rk_a'][l], t, tb)
            wkv_new.append(jnp.swapaxes(_unpair_state(s_new), -1, -2))
            shift_new.append(hmod.reshape(n_seq, t, d)[:, -1])
            x = _out_proj(o, gt, x, m4, gn[1], w['w_o_a'][l], tm_mix, tiles(tm_mix))
        else:
            j = l - n_a
            if past is None:
                qa = _q_aug(x, m4, gn[1], w['w_q_b'][j], kv[2], tm_mix, tiles(tm_mix))
                o = _flash_prompt(qa, kv[0], kv[1], n_seq, _largest_tile(t, 512), 4)
            else:
                q = _q_proj(x, m4, gn[1], w['w_q_b'][j], tm_mix, tiles(tm_mix))
                res = _paged_attention(q.reshape(n_seq, t, d), *kv, _largest_tile(kv[1].shape[1], 8))
                o = res[0].reshape(n, d)
                if len(res) > 1:
                    kv[0] = tuple(res[1:])
            x = _out_proj(o, None, x, m4, gn[1], w['w_o_b'][j], tm_mix, tiles(tm_mix))
        x = _ffn(x, mod4(l, 2), gn[2], w['w_ffn_in'][l, 1], w['w_ffn_out'][l, 1], tm_ffn, tiles(tm_ffn))
    return (x.reshape(n_seq, t, d), jnp.stack(wkv_new), jnp.stack(shift_new), k_new.reshape(n_seq, t, h, HEAD_DIM),
            v_new.reshape(n_seq, t, h, HEAD_DIM), logf.reshape(n_seq, t, h))


def kernel(x_prompt, x_sample, state_wkv, state_shift, cache_k, cache_v, cache_logf, page_table, c_prompt, c_sample,
           w_mod, b_mod, g_norm, w_ffn_in, w_ffn_out, mu_a, w_rkv_a, w_o_a, w0_a, w1_a, w2_a, a0_a, a1_a, a2_a,
           v0_a, v1_a, v2_a, g1_a, g2_a, kk_a, ka_a, rk_a, lnx_w_a, lnx_b_a, g_kv, w_kv, w_f, b_f, w_q_b, w_o_b):
    bp, _, d = x_prompt.shape
    bs = x_sample.shape[0]
    depth = w_mod.shape[0]
    n_a = state_wkv.shape[0]
    n_pool, page = cache_k.shape[:2]
    cast = lambda a: a.astype(BF16)
    w = dict(g_norm=g_norm, w_ffn_in=w_ffn_in, w_ffn_out=w_ffn_out, mu_a=mu_a, w_rkv_a=cast(w_rkv_a),
             w_o_a=cast(w_o_a), w0_a=w0_a, w1_a=cast(w1_a), w2_a=cast(w2_a), a0_a=a0_a, a1_a=cast(a1_a),
             a2_a=cast(a2_a), v0_a=v0_a, v1_a=cast(v1_a), v2_a=cast(v2_a), g1_a=cast(g1_a), g2_a=cast(g2_a),
             kk_a=kk_a, ka_a=ka_a, rk_a=rk_a.reshape(n_a, d), lnx_w_a=lnx_w_a, lnx_b_a=lnx_b_a, g_kv=g_kv,
             w_kv=cast(w_kv), w_f=cast(w_f), b_f=b_f, w_q_b=cast(w_q_b), w_o_b=cast(w_o_b))

    mods = _mods(jnp.concatenate([c_prompt, c_sample], axis=0), w_mod, b_mod)
    mods = mods.reshape(depth, bp + bs, 3, 3, d)

    n_heads = d // HEAD_DIM
    zeros_wkv = jnp.zeros((n_a, bp, n_heads, HEAD_DIM, HEAD_DIM), F32)
    zeros_shift = jnp.zeros((n_a, bp, d), F32)
    y_p, wkv_p, shift_p, k_p, v_p, logf_p = _trunk(x_prompt, mods[:, :bp], zeros_wkv, zeros_shift, None, w)
    past = (cache_k, cache_v, cache_logf, page_table)
    y_s, wkv_s, shift_s, k_s, v_s, logf_s = _trunk(x_sample, mods[:, bp:], state_wkv, state_shift, past, w)
    return (y_p, y_s, wkv_p, shift_p, k_p, v_p, logf_p, wkv_s, shift_s, k_s, v_s, logf_s)
```

```python
import functools

import jax
import jax.numpy as jnp
import numpy as np
from jax import lax
from jax.experimental import pallas as pl
from jax.experimental.pallas import tpu as pltpu

F32 = jnp.float32
BF16 = jnp.bfloat16

NORM_EPS = 1e-6
GN_EPS = 64e-5
MACARON_W = 0.5
NEG_INF = -1e30
HEAD_DIM = 64
LANES = 128
PAIR = LANES // HEAD_DIM
VMEM_LIMIT = 56 * 1024 * 1024

LOG2E = 1.4426950408889634
N_BIAS = 3

_NT = (((1,), (1,)), ((), ()))
_TN = (((0,), (0,)), ((), ()))


def _cparams(*sem):
    return pltpu.CompilerParams(dimension_semantics=sem, vmem_limit_bytes=VMEM_LIMIT)


def _dot(a, b):
    return jnp.dot(a.astype(BF16), b.astype(BF16), preferred_element_type=F32)


def _dot_nt(a, b):
    return lax.dot_general(a.astype(BF16), b.astype(BF16), _NT, preferred_element_type=F32)


def _dot_tn(a, b):
    return lax.dot_general(a.astype(BF16), b.astype(BF16), _TN, preferred_element_type=F32)


def _split3(x):
    hi = x.astype(BF16)
    r1 = x - hi.astype(F32)
    mid = r1.astype(BF16)
    lo = (r1 - mid.astype(F32)).astype(BF16)
    return hi, mid, lo


def _dot3(sel, x):
    hi, mid, lo = _split3(x)
    out = jnp.dot(sel, hi, preferred_element_type=F32)
    out = out + jnp.dot(sel, mid, preferred_element_type=F32)
    return out + jnp.dot(sel, lo, preferred_element_type=F32)


def _dot3_rhs(x, sel):
    hi, mid, lo = _split3(x)
    out = jnp.dot(hi, sel, preferred_element_type=F32)
    out = out + jnp.dot(mid, sel, preferred_element_type=F32)
    return out + jnp.dot(lo, sel, preferred_element_type=F32)


def _rms(x, g):
    return x * lax.rsqrt(jnp.mean(x * x, axis=-1, keepdims=True) + NORM_EPS) * g


def _sigmoid(x):
    return 1.0 / (1.0 + jnp.exp(-x))


def _silu(x):
    return x * _sigmoid(x)


def _head_ones(n):
    r = lax.broadcasted_iota(jnp.int32, (n, n), 0) // HEAD_DIM
    c = lax.broadcasted_iota(jnp.int32, (n, n), 1) // HEAD_DIM
    return jnp.where(r == c, 1.0, 0.0).astype(BF16)


def _head_sum(x, ones):
    n = ones.shape[0]
    hi = x.astype(BF16)
    lo = (x - hi.astype(F32)).astype(BF16)
    cols = []
    for i in range(0, x.shape[1], n):
        s = jnp.dot(hi[:, i:i + n], ones, preferred_element_type=F32)
        cols.append(s + jnp.dot(lo[:, i:i + n], ones, preferred_element_type=F32))
    return cols[0] if len(cols) == 1 else jnp.concatenate(cols, axis=1)


def _mods_kernel(c_ref, w_ref, b_ref, o_ref):
    o_ref[...] = _dot(_silu(c_ref[...]), w_ref[...]) + b_ref[...]


def _mods(c_all, w_mod, b_mod):
    depth, d, n9 = w_mod.shape
    nb = c_all.shape[0]
    tn = d
    return pl.pallas_call(
        _mods_kernel,
        grid=(depth, n9 // tn),
        in_specs=[
            pl.BlockSpec((nb, d), lambda l, j: (0, 0)),
            pl.BlockSpec((None, d, tn), lambda l, j: (l, 0, j)),
            pl.BlockSpec((None, 1, tn), lambda l, j: (l, 0, j)),
        ],
        out_specs=pl.BlockSpec((None, nb, tn), lambda l, j: (l, 0, j)),
        out_shape=jax.ShapeDtypeStruct((depth, nb, n9), F32),
        compiler_params=_cparams("arbitrary", "arbitrary"),
        name="mods",
    )(c_all, w_mod, b_mod.reshape(depth, 1, n9))


def _mod_spec(m4, tm, tiles_per_seq):
    _, _, rows, d = m4.shape
    if rows == 1:
        return pl.BlockSpec((None, 3, 1, d), lambda i, *_: (i // tiles_per_seq, 0, 0, 0))
    return pl.BlockSpec((None, 3, tm, d), lambda i, *_: (0, 0, i, 0))


def _ffn_kernel(x_ref, m_ref, g_ref, wg_ref, wu_ref, wo_ref, o_ref, h_scr, acc_scr):
    j = pl.program_id(1)

    @pl.when(j == 0)
    def _():
        h = _rms(x_ref[...], g_ref[0:1]) * (1.0 + m_ref[1]) + m_ref[0]
        h_scr[...] = h.astype(BF16)
        acc_scr[...] = jnp.zeros_like(acc_scr)

    h = h_scr[...]
    gate = _dot(h, wg_ref[...])
    up = _dot(h, wu_ref[...])
    acc_scr[...] += _dot(_silu(gate) * up, wo_ref[...])

    @pl.when(j == pl.num_programs(1) - 1)
    def _():
        o_ref[...] = x_ref[...] + MACARON_W * m_ref[2] * _rms(acc_scr[...], g_ref[1:2])


def _ffn(x, m4, gn, w_in, w_out, layer, sub, tm, tiles_per_seq):
    n, d = x.shape
    f = w_out.shape[2]
    tf = 256 if f % 256 == 0 else LANES
    nf = f // tf
    return pl.pallas_call(
        _ffn_kernel,
        grid=(n // tm, nf),
        in_specs=[
            pl.BlockSpec((tm, d), lambda i, j: (i, 0)),
            _mod_spec(m4, tm, tiles_per_seq),
            pl.BlockSpec((2, d), lambda i, j: (0, 0)),
            pl.BlockSpec((None, None, d, tf), lambda i, j: (layer, sub, 0, j)),
            pl.BlockSpec((None, None, d, tf), lambda i, j: (layer, sub, 0, j + nf)),
            pl.BlockSpec((None, None, tf, d), lambda i, j: (layer, sub, j, 0)),
        ],
        out_specs=pl.BlockSpec((tm, d), lambda i, j: (i, 0)),
        out_shape=jax.ShapeDtypeStruct((n, d), F32),
        scratch_shapes=[pltpu.VMEM((tm, d), BF16), pltpu.VMEM((tm, d), F32)],
        compiler_params=_cparams("parallel", "arbitrary"),
        name="ffn",
    )(x, m4, gn, w_in, w_in, w_out)


def _rwkv_proj_kernel(has_vfirst, seq_len, x_ref, m_ref, g_ref, hl_ref, mu_ref, wrkv_ref, w0_ref, w1_ref,
                      w2_ref, a0_ref, a1_ref, a2_ref, g1_ref, g2_ref, kka_ref, kaa_ref, *rest):
    if has_vfirst:
        v0_ref, v1_ref, v2_ref, vf_ref = rest[:4]
        rest = rest[4:]
    r_ref, lw_ref, k_ref, v_ref, kk_ref, a_ref, gt_ref, h_ref, carry_scr = rest
    tm, d = x_ref.shape
    i = pl.program_id(0)

    h = _rms(x_ref[...], g_ref[0:1]) * (1.0 + m_ref[1]) + m_ref[0]
    h_ref[...] = h
    row = lax.broadcasted_iota(jnp.int32, (tm, d), 0)
    rolled = pltpu.roll(h, 1, 0)
    if hl_ref.shape[0] == 1:
        tiles_per_seq = seq_len // tm
        first = jnp.where(i % tiles_per_seq == 0, hl_ref[...], carry_scr[...])
        h_prev = jnp.where(row == 0, first, rolled)
        carry_scr[...] = h[tm - 1:tm, :]
    else:
        h_prev = jnp.where(row % seq_len == 0, hl_ref[...], rolled)
    xx = h_prev - h
    xr, xw, xk, xv, xa, xg = [h + xx * mu_ref[j:j + 1] for j in range(6)]

    r = _dot(xr, wrkv_ref[0])
    k = _dot(xk, wrkv_ref[1])
    v = _dot(xv, wrkv_ref[2])
    z = w0_ref[...] + _dot(jnp.tanh(_dot(xw, w1_ref[...])), w2_ref[...])
    lw = -_sigmoid(z) * jnp.exp(F32(-0.5))
    a = _sigmoid(a0_ref[...] + _dot(_dot(xa, a1_ref[...]), a2_ref[...]))
    gt = _dot(_sigmoid(_dot(xg, g1_ref[...])), g2_ref[...])
    if has_vfirst:
        mix = _sigmoid(v0_ref[...] + _dot(_dot(xv, v1_ref[...]), v2_ref[...]))
        v = v + (vf_ref[...] - v) * mix
    kk = k * kka_ref[...]
    ones = _head_ones(min(d, 2 * LANES))
    kk = kk * lax.rsqrt(jnp.maximum(_head_sum(kk * kk, ones), 1e-24))
    k = k * (1.0 + (a - 1.0) * kaa_ref[...])

    r_ref[...] = r
    lw_ref[...] = lw
    k_ref[...] = k
    v_ref[...] = v
    kk_ref[...] = kk
    a_ref[...] = a
    gt_ref[...] = gt


def _rwkv_proj(x, m4, gn, hl, seq_len, tm, p, v_first):
    n, d = x.shape
    tiles_per_seq = max(seq_len // tm, 1)
    has_vfirst = v_first is not None
    row = lambda a: a.reshape(1, d)
    const = lambda a: pl.BlockSpec(a.shape, lambda i: (0,) * a.ndim)
    tok = pl.BlockSpec((tm, d), lambda i: (i, 0))
    if hl.ndim == 3:
        hl_spec = pl.BlockSpec((None, 1, d), lambda i: (i // tiles_per_seq, 0, 0))
    else:
        hl_spec = tok
    args = [x, m4, gn, hl, p['mu'], p['w_rkv'], row(p['w0']), p['w1'], p['w2'], row(p['a0']), p['a1'], p['a2'],
            p['g1'], p['g2'], row(p['kk']), row(p['ka'])]
    specs = [tok, _mod_spec(m4, tm, tiles_per_seq), const(gn), hl_spec] + [const(a) for a in args[4:]]
    if has_vfirst:
        extra = [row(p['v0']), p['v1'], p['v2']]
        args += extra + [v_first]
        specs += [const(a) for a in extra] + [tok]
    return pl.pallas_call(
        functools.partial(_rwkv_proj_kernel, has_vfirst, seq_len),
        grid=(n // tm,),
        in_specs=specs,
        out_specs=[tok] * 8,
        out_shape=[jax.ShapeDtypeStruct((n, d), F32)] * 8,
        scratch_shapes=[pltpu.VMEM((1, d), F32)],
        compiler_params=_cparams("arbitrary"),
        name="rwkv_proj",
    )(*args)


LONG_CHUNK = 64


def _long_masks():
    n2 = PAIR * LONG_CHUNK
    r = np.arange(n2)[:, None]
    c = np.arange(n2)[None, :]
    masks = [c < r, c <= r, c == r]
    b = 1
    while b < LONG_CHUNK:
        masks.append((r // (2 * b) == c // (2 * b)) & ((r // b) % 2 == 1) & ((c // b) % 2 == 0))
        b *= 2
    return jnp.asarray(np.stack(masks).astype(np.float32))


def _wkv_long_kernel(r_ref, lw_ref, k_ref, v_ref, kk_ref, a_ref, s0_ref, lnw_ref, lnb_ref, rk_ref, mask_ref,
                     o_ref, sout_ref, st_scr, wq_scr, bk_scr, lkk_scr, v2_scr, kt_scr, bt_scr, lrb_scr, lvt_scr,
                     x_scr, gc_scr, lab_scr, tinv_scr, ol_scr, ds_scr, ut_scr):
    c = LONG_CHUNK
    n2 = PAIR * c
    tb, d = r_ref.shape
    n_pairs = d // LANES
    t_idx = pl.program_id(1)

    @pl.when(t_idx == 0)
    def _():
        zero = jnp.zeros((HEAD_DIM, HEAD_DIM), F32)
        for p in range(n_pairs):
            top = jnp.concatenate([s0_ref[PAIR * p], zero], axis=1)
            bottom = jnp.concatenate([zero, s0_ref[PAIR * p + 1]], axis=1)
            st_scr[p] = jnp.concatenate([top, bottom], axis=0).T

    rows_in = min(tb, c)
    n_chunks = max(tb // c, 1)
    items = [(ic, p) for ic in range(n_chunks) for p in range(n_pairs)]
    dotf = lambda x, y: jnp.dot(x, y, preferred_element_type=F32)
    lane = lax.broadcasted_iota(jnp.int32, (c, LANES), 1)
    head_lane = [lane // HEAD_DIM == h for h in range(PAIR)]
    tri = jnp.where(lax.broadcasted_iota(jnp.int32, (c, c), 1) <= lax.broadcasted_iota(jnp.int32, (c, c), 0),
                    1.0, 0.0).astype(BF16)

    def stack(x):
        return jnp.concatenate([jnp.where(m, x, 0.0) for m in head_lane], axis=0)

    for i, (ic, p) in enumerate(items):
        rows, ls = slice(ic * c, ic * c + rows_in), slice(p * LANES, (p + 1) * LANES)
        rc, lwc, kc, vc, kkc, ac = [ref[rows, ls] for ref in (r_ref, lw_ref, k_ref, v_ref, kk_ref, a_ref)]
        if rows_in < c:
            pad = jnp.zeros((c - rows_in, LANES), F32)
            rc, lwc, kc, vc, kkc, ac = [jnp.concatenate([x, pad], axis=0) for x in (rc, lwc, kc, vc, kkc, ac)]
        cum = _dot3(tri, lwc)
        g = jnp.exp(cum)
        gi = jnp.exp(-cum)
        g_end = g[c - 1:c, :]
        bb = ac * kkc * gi
        kb = kc * gi
        wq_scr[i, :n2] = stack(-kkc * jnp.exp(cum - lwc)).astype(BF16)
        wq_scr[i, n2:] = stack(rc * g).astype(BF16)
        bk_scr[i, :n2] = stack(bb).astype(BF16)
        bk_scr[i, n2:] = stack(kb).astype(BF16)
        v2_scr[i] = stack(vc).astype(BF16)
        kt_scr[i] = stack(kb * g_end).astype(BF16)
        bt_scr[i] = stack(bb * g_end).T.astype(BF16)
        gc_scr[i] = jnp.broadcast_to(g_end, (LANES, LANES)).T

    for i in range(len(items)):
        scores = lax.dot_general(wq_scr[i], bk_scr[i], _NT, preferred_element_type=F32)
        l_ab = scores[:n2, :n2] * mask_ref[0]
        lab_scr[i] = l_ab
        tinv_scr[i] = mask_ref[2] + l_ab * mask_ref[3]
        lkk_scr[i, :n2] = (scores[:n2, n2:] * mask_ref[0]).astype(BF16)
        lkk_scr[i, n2:] = (scores[n2:, n2:] * mask_ref[1]).astype(BF16)
        lrb_scr[i] = (scores[n2:, :n2] * mask_ref[1]).astype(BF16)

    for i in range(len(items)):
        lv = dotf(lkk_scr[i], v2_scr[i])
        lvt_scr[i] = lv[:n2].astype(BF16)
        ol_scr[i] = lv[n2:]
        ds_scr[i] = lax.dot_general(kt_scr[i], v2_scr[i], _TN, preferred_element_type=F32)

    for level in range(4, mask_ref.shape[0]):
        for i in range(len(items)):
            x_scr[i] = dotf((lab_scr[i] * mask_ref[level]).astype(BF16), tinv_scr[i].astype(BF16)).astype(BF16)
        for i in range(len(items)):
            tinv = tinv_scr[i]
            tinv_scr[i] = tinv + dotf(tinv.astype(BF16), x_scr[i])

    for i in range(len(items)):
        tinv = tinv_scr[i].astype(BF16)
        ut_scr[i] = dotf(tinv, lvt_scr[i])
        wq_scr[i, :n2] = dotf(tinv, wq_scr[i, :n2]).astype(BF16)

    def advance(ic, carry):
        rows = pl.ds(pl.multiple_of(ic * c, c), rows_in)
        for p in range(n_pairs):
            i = ic * n_pairs + p
            st = st_scr[p]
            ws = dotf(wq_scr[i], st.astype(BF16))
            u2 = (ws[:n2] + ut_scr[i]).astype(BF16)
            o2 = ws[n2:] + dotf(lrb_scr[i], u2) + ol_scr[i]
            st_scr[p] = gc_scr[i] * st + dotf(bt_scr[i], u2) + ds_scr[i]
            o = o2[0:c]
            for h in range(1, PAIR):
                o = o + o2[h * c:(h + 1) * c]
            o_ref[rows, p * LANES:(p + 1) * LANES] = o[:rows_in]
        return carry

    lax.fori_loop(0, n_chunks, advance, 0)

    ones = _head_ones(min(d, 2 * LANES))
    o = o_ref[...]
    cen = o - _head_sum(o, ones) * (1.0 / HEAD_DIM)
    var = _head_sum(cen * cen, ones) * (1.0 / HEAD_DIM)
    bonus = _head_sum(r_ref[...] * k_ref[...] * rk_ref[...], ones) * v_ref[...]
    o_ref[...] = cen * lax.rsqrt(var + GN_EPS) * lnw_ref[...] + lnb_ref[...] + bonus

    @pl.when(t_idx == pl.num_programs(1) - 1)
    def _():
        for p in range(n_pairs):
            s = st_scr[p].T
            sout_ref[PAIR * p] = s[:HEAD_DIM, :HEAD_DIM]
            sout_ref[PAIR * p + 1] = s[HEAD_DIM:, HEAD_DIM:]


def _wkv_long(r, lw, k, v, kk, a, s0, lnw, lnb, rk, seq_len, tb):
    assert PAIR == 2
    n, d = r.shape
    n_seq = n // seq_len
    n_pairs = d // LANES
    nt = seq_len // tb
    n_items = max(tb // LONG_CHUNK, 1) * n_pairs
    n2 = PAIR * LONG_CHUNK
    assert n2 == LANES
    masks = _long_masks()
    tok = pl.BlockSpec((tb, d), lambda b, t: (b * nt + t, 0))
    vec = pl.BlockSpec((1, d), lambda b, t: (0, 0))
    st = pl.BlockSpec((None, PAIR * n_pairs, HEAD_DIM, HEAD_DIM), lambda b, t: (b, 0, 0, 0))
    per = lambda rows, dt: pltpu.VMEM((n_items, rows, LANES), dt)
    return pl.pallas_call(
        _wkv_long_kernel,
        grid=(n_seq, nt),
        in_specs=[tok] * 6 + [st, vec, vec, vec, pl.BlockSpec(masks.shape, lambda b, t: (0, 0, 0))],
        out_specs=[tok, st],
        out_shape=[jax.ShapeDtypeStruct((n, d), F32), jax.ShapeDtypeStruct(s0.shape, F32)],
        scratch_shapes=[pltpu.VMEM((n_pairs, LANES, LANES), F32)]
        + [per(2 * n2, BF16)] * 3 + [per(n2, BF16)] * 6 + [per(n2, F32)] * 6,
        compiler_params=_cparams("parallel", "arbitrary"),
        name="wkv_long",
    )(r, lw, k, v, kk, a, s0, lnw.reshape(1, d), lnb.reshape(1, d), rk.reshape(1, d), masks)


def _out_kernel(has_gate, res_w, *refs):
    if has_gate:
        a_ref, gt_ref, x_ref, m_ref, g_ref, w_ref, o_ref = refs
        a = a_ref[...] * gt_ref[...]
    else:
        a_ref, x_ref, m_ref, g_ref, w_ref, o_ref = refs
        a = a_ref[...]
    y = _dot(a, w_ref[...])
    o_ref[...] = x_ref[...] + res_w * m_ref[2] * _rms(y, g_ref[1:2])


def _out_proj(a, gate, x, m4, gn, w, tm, tiles_per_seq, res_w=1.0):
    n, d = x.shape
    tok = pl.BlockSpec((tm, d), lambda i: (i, 0))
    args = [a] + ([gate] if gate is not None else []) + [x, m4, gn, w]
    specs = [tok] * (len(args) - 3) + [_mod_spec(m4, tm, tiles_per_seq), pl.BlockSpec((2, d), lambda i: (0, 0)),
                                       pl.BlockSpec(w.shape, lambda i: (0, 0))]
    return pl.pallas_call(
        functools.partial(_out_kernel, gate is not None, res_w),
        grid=(n // tm,),
        in_specs=specs,
        out_specs=tok,
        out_shape=jax.ShapeDtypeStruct((n, d), F32),
        compiler_params=_cparams("parallel"),
        name="out_proj",
    )(*args)


def _kv_kernel(x_ref, g_ref, wkv_ref, wf_ref, bf_ref, k_ref, v_ref, kb_ref, vb_ref, lf_ref):
    d = x_ref.shape[1]
    s = _rms(x_ref[...], g_ref[...])
    kv = _dot(s, wkv_ref[...])
    k = kv[:, :d]
    v = kv[:, d:]
    k_ref[...] = k
    v_ref[...] = v
    kb_ref[...] = k.astype(BF16)
    vb_ref[...] = v.astype(BF16)
    z = _dot(s, wf_ref[...]) + bf_ref[...]
    lf_ref[...] = jnp.minimum(z, 0.0) - jnp.log(1.0 + jnp.exp(-jnp.abs(z)))


def _shared_kv(x, g_kv, w_kv, w_f, b_f, tm):
    n, d = x.shape
    h = w_f.shape[1]
    tok = pl.BlockSpec((tm, d), lambda i: (i, 0))
    return pl.pallas_call(
        _kv_kernel,
        grid=(n // tm,),
        in_specs=[tok, pl.BlockSpec((1, d), lambda i: (0, 0)), pl.BlockSpec(w_kv.shape, lambda i: (0, 0)),
                  pl.BlockSpec(w_f.shape, lambda i: (0, 0)), pl.BlockSpec((1, h), lambda i: (0, 0))],
        out_specs=[tok, tok, tok, tok, pl.BlockSpec((tm, h), lambda i: (i, 0))],
        out_shape=[jax.ShapeDtypeStruct((n, d), F32)] * 2 + [jax.ShapeDtypeStruct((n, d), BF16)] * 2
        + [jax.ShapeDtypeStruct((n, h), F32)],
        compiler_params=_cparams("parallel"),
        name="shared_kv",
    )(x, g_kv.reshape(1, d), w_kv, w_f, b_f.reshape(1, h))


def _cumsum_kernel(n_grp, has_new, has_parts, pt_ref, *refs):
    lf_refs = refs[:n_grp]
    rest = refs[n_grp:]
    if has_new:
        new_ref, ck_ref, cknew_ref, carry_scr = rest
    elif has_parts:
        ck_ref, parts_ref, carry_scr = rest
    else:
        ck_ref, carry_scr = rest
    del pt_ref
    j = pl.program_id(1)
    page, h = lf_refs[0].shape

    @pl.when(j == 0)
    def _():
        carry_scr[...] = jnp.zeros_like(carry_scr)

    ri = lax.broadcasted_iota(jnp.int32, (page, page), 0)
    ci = lax.broadcasted_iota(jnp.int32, (page, page), 1)
    tri = jnp.where(ci <= ri, 1.0, 0.0).astype(BF16)
    local = [_dot3(tri, lf[...]) for lf in lf_refs]
    base = carry_scr[...]
    for i in range(n_grp):
        ck = base + local[i]
        ck_ref[i * page:(i + 1) * page, :] = ck
        base = ck[page - 1:page, :]

        if has_parts:
            src_h = lax.broadcasted_iota(jnp.int32, (h, LANES), 0)
            dst = lax.broadcasted_iota(jnp.int32, (h, LANES), 1)
            out = jnp.where(lax.broadcasted_iota(jnp.int32, (page, LANES), 1) // h == N_BIAS, 1.0, 0.0)
            for x, part in enumerate(_split3(ck * LOG2E)):
                out = out + jnp.dot(part, jnp.where(dst == x * h + src_h, 1.0, 0.0).astype(BF16),
                                    preferred_element_type=F32)
            parts_ref[i * page:(i + 1) * page, :] = out.astype(BF16)
    carry_scr[...] = base

    if has_new:
        @pl.when(j == pl.num_programs(1) - 1)
        def _():
            t = new_ref.shape[0]
            cknew_ref[...] = base + _dot3(tri[:t, :t], new_ref[...])


def _cumsum_pages(page_table, lf_pages, lf_new, with_parts=False):
    n_seq, n_pages = page_table.shape
    _, page, h = lf_pages.shape
    has_new = lf_new is not None
    n_grp = _largest_tile(n_pages, 8)

    def page_spec(i):
        return pl.BlockSpec((None, page, h), lambda b, j, pt: (pt[b, j * n_grp + i], 0, 0))

    in_specs = [page_spec(i) for i in range(n_grp)]
    out_specs = [pl.BlockSpec((None, n_grp * page, h), lambda b, j, pt: (b, j, 0))]
    out_shape = [jax.ShapeDtypeStruct((n_seq, n_pages * page, h), F32)]
    args = [lf_pages] * n_grp
    if has_new:
        t = lf_new.shape[1]
        in_specs.append(pl.BlockSpec((None, t, h), lambda b, j, pt: (b, 0, 0)))
        out_specs.append(pl.BlockSpec((None, t, h), lambda b, j, pt: (b, 0, 0)))
        out_shape.append(jax.ShapeDtypeStruct((n_seq, t, h), F32))
        args.append(lf_new)
    elif with_parts:
        out_specs.append(pl.BlockSpec((None, n_grp * page, LANES), lambda b, j, pt: (b, j, 0)))
        out_shape.append(jax.ShapeDtypeStruct((n_seq, n_pages * page, LANES), BF16))
    return pl.pallas_call(
        functools.partial(_cumsum_kernel, n_grp, has_new, with_parts and not has_new),
        grid_spec=pltpu.PrefetchScalarGridSpec(
            num_scalar_prefetch=1, grid=(n_seq, n_pages // n_grp), in_specs=in_specs, out_specs=out_specs,
            scratch_shapes=[pltpu.VMEM((1, h), F32)]),
        out_shape=out_shape,
        compiler_params=_cparams("parallel", "arbitrary"),
        name="cumsum",
    )(page_table, *args)


def _q_kernel(x_ref, m_ref, g_ref, w_ref, q_ref):
    h = _rms(x_ref[...], g_ref[0:1]) * (1.0 + m_ref[1]) + m_ref[0]
    q_ref[...] = (_dot(h, w_ref[...]) * (HEAD_DIM ** -0.5)).astype(BF16)


def _q_proj(x, m4, gn, w, tm, tiles_per_seq):
    n, d = x.shape
    tok = pl.BlockSpec((tm, d), lambda i: (i, 0))
    return pl.pallas_call(
        _q_kernel,
        grid=(n // tm,),
        in_specs=[tok, _mod_spec(m4, tm, tiles_per_seq), pl.BlockSpec((2, d), lambda i: (0, 0)),
                  pl.BlockSpec(w.shape, lambda i: (0, 0))],
        out_specs=tok,
        out_shape=jax.ShapeDtypeStruct((n, d), BF16),
        compiler_params=_cparams("parallel"),
        name="q_proj",
    )(x, m4, gn, w)


def _pair_to_tiles():
    m = np.zeros((LANES, PAIR * LANES), np.float32)
    r = np.arange(LANES)
    m[r, (r // HEAD_DIM) * LANES + r % HEAD_DIM] = 1.0
    return jnp.asarray(m, BF16)


def _bias_place(h, for_keys):
    m = np.zeros((LANES, h * LANES), np.float32)
    for hh in range(h):
        for x in range(N_BIAS):
            if for_keys:
                m[N_BIAS * h + hh, hh * LANES + HEAD_DIM + x] = 1.0
                m[x * h + hh, hh * LANES + HEAD_DIM + N_BIAS + x] = -1.0
            else:
                m[x * h + hh, hh * LANES + HEAD_DIM + x] = 1.0
                m[N_BIAS * h + hh, hh * LANES + HEAD_DIM + N_BIAS + x] = 1.0
    return jnp.asarray(m, BF16)


def _to_head_tiles(x, place, extra):
    cols = []
    for p in range(x.shape[1] // LANES):
        sl = slice(p * PAIR * LANES, (p + 1) * PAIR * LANES)
        cols.append(jnp.dot(x[:, p * LANES:(p + 1) * LANES], place, preferred_element_type=F32) + extra[:, sl])
    return jnp.concatenate(cols, axis=1)


def _q_aug_kernel(x_ref, m_ref, g_ref, w_ref, parts_ref, place_ref, bias_ref, q_ref):
    h = _rms(x_ref[...], g_ref[0:1]) * (1.0 + m_ref[1]) + m_ref[0]
    q = (_dot(h, w_ref[...]) * (HEAD_DIM ** -0.5 * LOG2E)).astype(BF16)
    bias = jnp.dot(parts_ref[...], bias_ref[...], preferred_element_type=F32)
    q_ref[...] = _to_head_tiles(q, place_ref[...], bias).astype(BF16)


def _q_aug(x, m4, gn, w, parts, tm, tiles_per_seq):
    n, d = x.shape
    tok = pl.BlockSpec((tm, d), lambda i: (i, 0))
    place, bias = _pair_to_tiles(), _bias_place(d // HEAD_DIM, False)
    const = lambda a: pl.BlockSpec(a.shape, lambda i: (0, 0))
    return pl.pallas_call(
        _q_aug_kernel,
        grid=(n // tm,),
        in_specs=[tok, _mod_spec(m4, tm, tiles_per_seq), pl.BlockSpec((2, d), lambda i: (0, 0)), const(w),
                  pl.BlockSpec((tm, LANES), lambda i: (i, 0)), const(place), const(bias)],
        out_specs=pl.BlockSpec((tm, PAIR * d), lambda i: (i, 0)),
        out_shape=jax.ShapeDtypeStruct((n, PAIR * d), BF16),
        compiler_params=_cparams("parallel"),
        name="q_aug",
    )(x, m4, gn, w, parts, place, bias)


def _kv_aug_kernel(k_ref, v_ref, parts_ref, place_ref, bias_ref, ka_ref, va_ref):
    tm, d2 = ka_ref.shape
    bias = jnp.dot(parts_ref[...], bias_ref[...], preferred_element_type=F32)
    ka_ref[...] = _to_head_tiles(k_ref[...], place_ref[...], bias).astype(BF16)
    ones = jnp.where(lax.broadcasted_iota(jnp.int32, (1, d2), 1) % LANES >= HEAD_DIM, 1.0, 0.0)
    va_ref[...] = _to_head_tiles(v_ref[...], place_ref[...], ones).astype(BF16)


def _kv_aug(kb, vb, parts, tm):
    n, d = kb.shape
    tok = pl.BlockSpec((tm, d), lambda i: (i, 0))
    wide = pl.BlockSpec((tm, PAIR * d), lambda i: (i, 0))
    place, bias = _pair_to_tiles(), _bias_place(d // HEAD_DIM, True)
    const = lambda a: pl.BlockSpec(a.shape, lambda i: (0, 0))
    return pl.pallas_call(
        _kv_aug_kernel,
        grid=(n // tm,),
        in_specs=[tok, tok, pl.BlockSpec((tm, LANES), lambda i: (i, 0)), const(place), const(bias)],
        out_specs=[wide, wide],
        out_shape=[jax.ShapeDtypeStruct((n, PAIR * d), BF16)] * 2,
        compiler_params=_cparams("parallel"),
        name="kv_aug",
    )(kb, vb, parts, place, bias)


def _flash_kernel(qi_ref, kj_ref, q_ref, k_ref, v_ref, o_ref, m_scr, acc_scr):
    n = pl.program_id(2)
    qi = qi_ref[n]
    kj = kj_ref[n]
    tq = q_ref.shape[0]
    tk = k_ref.shape[0]
    n_heads = q_ref.shape[1] // LANES

    @pl.when(kj == 0)
    def _():
        m_scr[...] = jnp.full_like(m_scr, NEG_INF)
        acc_scr[...] = jnp.zeros_like(acc_scr)

    def update(masked):
        if masked:
            valid = (lax.broadcasted_iota(jnp.int32, (tq, tk), 1)
                     <= lax.broadcasted_iota(jnp.int32, (tq, tk), 0))
        for h in range(n_heads):
            sl = slice(h * LANES, (h + 1) * LANES)
            s = lax.dot_general(q_ref[:, sl], k_ref[:, sl], _NT, preferred_element_type=F32)
            if masked:
                s = jnp.where(valid, s, NEG_INF)
            cols = [s[:, c * LANES:(c + 1) * LANES] for c in range(tk // LANES)]
            m_cur = cols[0]
            for col in cols[1:]:
                m_cur = jnp.maximum(m_cur, col)
            m_old = m_scr[h]
            m_new = jnp.maximum(m_old, jnp.max(m_cur, axis=1, keepdims=True))
            p = jnp.concatenate([jnp.exp2(col - m_new) for col in cols], axis=1).astype(BF16)
            acc_scr[h] = jnp.exp2(m_old - m_new) * acc_scr[h] + jnp.dot(p, v_ref[:, sl],
                                                                        preferred_element_type=F32)
            m_scr[h] = m_new

    @pl.when(kj < qi)
    def _():
        update(False)

    @pl.when(kj == qi)
    def _():
        update(True)
        lane = lax.broadcasted_iota(jnp.int32, (tq, LANES), 1)
        outs = []
        for h in range(n_heads):
            acc = acc_scr[h]
            outs.append(acc / pltpu.roll(acc, HEAD_DIM, 1))
        for p in range(n_heads // PAIR):
            o_ref[:, p * LANES:(p + 1) * LANES] = jnp.where(
                lane < HEAD_DIM, outs[PAIR * p], pltpu.roll(outs[PAIR * p + 1], HEAD_DIM, 1))


def _flash_prompt(qa, ka, va, n_seq, tq, heads_per_step):
    n, d2 = qa.shape
    t = n // n_seq
    nq = t // tq
    width = heads_per_step * LANES
    qi = jnp.asarray([i for i in range(nq) for _ in range(i + 1)], jnp.int32)
    kj = jnp.asarray([j for i in range(nq) for j in range(i + 1)], jnp.int32)
    return pl.pallas_call(
        _flash_kernel,
        grid_spec=pltpu.PrefetchScalarGridSpec(
            num_scalar_prefetch=2,
            grid=(n_seq, d2 // width, int(qi.shape[0])),
            in_specs=[
                pl.BlockSpec((tq, width), lambda b, p, s, qi, kj: (b * nq + qi[s], p)),
                pl.BlockSpec((tq, width), lambda b, p, s, qi, kj: (b * nq + kj[s], p)),
                pl.BlockSpec((tq, width), lambda b, p, s, qi, kj: (b * nq + kj[s], p)),
            ],
            out_specs=pl.BlockSpec((tq, width // PAIR), lambda b, p, s, qi, kj: (b * nq + qi[s], p)),
            scratch_shapes=[pltpu.VMEM((heads_per_step, tq, LANES), F32),
                            pltpu.VMEM((heads_per_step, tq, LANES), F32)]),
        out_shape=jax.ShapeDtypeStruct((n, d2 // PAIR), F32),
        compiler_params=_cparams("parallel", "parallel", "arbitrary"),
        name="fox_prompt",
    )(qi, kj, qa, ka, va)


def _paged_kernel(n_grp, pt_ref, q_ref, cq_ref, ckp_ref, ckn_ref, kn_ref, vn_ref, *rest):
    t, d = q_ref.shape
    n_heads = d // HEAD_DIM
    rows = n_heads * t
    k_refs = rest[:n_grp]
    v_refs = rest[n_grp:2 * n_grp]
    o_ref, qbd_scr, m_scr, l_scr, acc_scr = rest[2 * n_grp:]
    page = k_refs[0].shape[1]
    del pt_ref
    g = pl.program_id(1)

    row_head = lax.broadcasted_iota(jnp.int32, (rows, d), 0) // t
    lane_head = lax.broadcasted_iota(jnp.int32, (rows, d), 1) // HEAD_DIM
    expand = jnp.where(lax.broadcasted_iota(jnp.int32, (rows, n_heads), 0) // t
                       == lax.broadcasted_iota(jnp.int32, (rows, n_heads), 1), 1.0, 0.0).astype(BF16)

    @pl.when(g == 0)
    def _():
        q = q_ref[...].astype(F32)
        q_rows = jnp.concatenate([q] * n_heads, axis=0)
        qbd_scr[...] = jnp.where(row_head == lane_head, q_rows, 0.0).astype(BF16)
        m_scr[...] = jnp.full_like(m_scr, NEG_INF)
        l_scr[...] = jnp.zeros_like(l_scr)
        acc_scr[...] = jnp.zeros_like(acc_scr)

    qbd = qbd_scr[...]
    cq = cq_ref[...]

    def attend(s, weighted_values):
        m_old = m_scr[...]
        m_new = jnp.maximum(m_old, jnp.max(s, axis=1, keepdims=True))
        alpha = jnp.exp(m_old - m_new)
        p = jnp.exp(s - m_new)
        l_scr[...] = alpha * l_scr[...] + jnp.sum(p, axis=1, keepdims=True)
        acc_scr[...] = alpha * acc_scr[...] + weighted_values(p)
        m_scr[...] = m_new

    kts = [kr[...].astype(BF16) for kr in k_refs]
    vts = [vr[...].astype(BF16) for vr in v_refs]

    def past_values(p):
        pv = _dot_nt(p[:, 0:page], vts[0])
        for i in range(1, n_grp):
            pv = pv + _dot_nt(p[:, i * page:(i + 1) * page], vts[i])
        return pv

    s = jnp.concatenate([_dot(qbd, kt) for kt in kts], axis=1)
    s = s + (cq - _dot3(expand, ckp_ref[...]))
    attend(s, past_values)

    @pl.when(g == pl.num_programs(1) - 1)
    def _():
        pad = jnp.zeros((page - t, d), F32)
        k_new = jnp.concatenate([kn_ref[...], pad], axis=0)
        v_new = jnp.concatenate([vn_ref[...], pad], axis=0)
        s_new = _dot_nt(qbd, k_new) + (cq - _dot3(expand, ckn_ref[...]))
        key = lax.broadcasted_iota(jnp.int32, (rows, page), 1)
        query = lax.broadcasted_iota(jnp.int32, (rows, page), 0) % t
        attend(jnp.where((key <= query) & (key < t), s_new, NEG_INF), lambda p: _dot(p, v_new))
        out = acc_scr[...] / l_scr[...]
        out = jnp.where(row_head == lane_head, out, 0.0).reshape(n_heads, t, d)
        o_ref[...] = jnp.sum(out, axis=0)


def _paged_attention(q, pool_kt, pool_vt, page_table, ck_past, ck_new, k_new, v_new, n_grp):
    n_seq, t, d = q.shape
    n_pages = page_table.shape[1]
    page = pool_kt.shape[2]
    h = d // HEAD_DIM
    rows = h * t
    assert n_pages % n_grp == 0
    cq_col = ck_new.transpose(0, 2, 1).reshape(n_seq, rows, 1)
    ckp_rows = ck_past.transpose(0, 2, 1)
    ckn_rows = jnp.pad(ck_new.transpose(0, 2, 1), ((0, 0), (0, 0), (0, page - t)))
    per_seq = lambda shape: pl.BlockSpec((None,) + shape, lambda b, g, pt: (b, 0, 0))

    def page_spec(i):
        return pl.BlockSpec((None, d, page), lambda b, g, pt: (pt[b, g * n_grp + i], 0, 0))

    in_specs = [per_seq((t, d)), per_seq((rows, 1)),
                pl.BlockSpec((None, h, n_grp * page), lambda b, g, pt: (b, 0, g)),
                per_seq((h, page)), per_seq((t, d)), per_seq((t, d))]
    in_specs += [page_spec(i) for i in range(n_grp)] * 2
    return pl.pallas_call(
        functools.partial(_paged_kernel, n_grp),
        grid_spec=pltpu.PrefetchScalarGridSpec(
            num_scalar_prefetch=1, grid=(n_seq, n_pages // n_grp), in_specs=in_specs,
            out_specs=per_seq((t, d)),
            scratch_shapes=[pltpu.VMEM((rows, d), BF16), pltpu.VMEM((rows, 1), F32), pltpu.VMEM((rows, 1), F32),
                            pltpu.VMEM((rows, d), F32)]),
        out_shape=jax.ShapeDtypeStruct((n_seq, t, d), F32),
        compiler_params=_cparams("parallel", "arbitrary"),
        name="fox_sample",
    )(page_table, q, cq_col, ckp_rows, ckn_rows, k_new, v_new, *([pool_kt] * n_grp), *([pool_vt] * n_grp))


def _largest_tile(n, cap):
    t = min(n, cap)
    while n % t:
        t //= 2
    return t


def _trunk(x3, mods, wkv0, shift0, past, w):
    n_seq, t, d = x3.shape
    n = n_seq * t
    depth = mods.shape[0]
    n_a = wkv0.shape[0]
    h = d // HEAD_DIM
    x = x3.reshape(n, d)
    per_seq = t >= 256
    if per_seq:
        tm_ffn, tm_mix = _largest_tile(t, 1024), _largest_tile(t, 256)
        tb = _largest_tile(t, 2 * LONG_CHUNK)
    else:
        tm_ffn = tm_mix = n
        tb = t

    def mod4(l, sub):
        m = mods[l, :, sub]
        if per_seq:
            return m.reshape(n_seq, 3, 1, d)
        return jnp.repeat(m, t, axis=0).transpose(1, 0, 2).reshape(1, 3, n, d)

    def tiles(tm):
        return max(t // tm, 1)

    v_first = None
    wkv_new, shift_new = [], []
    kv = None
    for l in range(depth):
        if l == n_a:
            k_new, v_new, kb, vb, logf = _shared_kv(x, w['g_kv'], w['w_kv'], w['w_f'], w['b_f'], tm_mix)
            if past is None:
                ident = jnp.arange(n // LANES, dtype=jnp.int32).reshape(n_seq, t // LANES)
                _, parts = _cumsum_pages(ident, logf.reshape(n // LANES, LANES, h), None, with_parts=True)
                parts = parts.reshape(n, LANES)
                kv = list(_kv_aug(kb, vb, parts, tm_mix)) + [parts]
            else:
                cache_k, cache_v, cache_logf, page_table = past
                ck_past, ck_new = _cumsum_pages(page_table, cache_logf, logf.reshape(n_seq, t, h))
                kv = [cache_k, cache_v, page_table, ck_past, ck_new,
                      k_new.reshape(n_seq, t, d), v_new.reshape(n_seq, t, d)]
        gn = w['g_norm'][l]
        x = _ffn(x, mod4(l, 0), gn[0], w['w_ffn_in'], w['w_ffn_out'], l, 0, tm_ffn, tiles(tm_ffn))
        m4 = mod4(l, 1)
        if l < n_a:
            p = {key: w[key + '_a'][l] for key in ('mu', 'w_rkv', 'w0', 'w1', 'w2', 'a0', 'a1', 'a2', 'g1', 'g2',
                                                   'kk', 'ka')}
            if l > 0:
                p.update({key: w[key + '_a'][l - 1] for key in ('v0', 'v1', 'v2')})
            hl = shift0[l].reshape(n_seq, 1, d) if per_seq else jnp.repeat(shift0[l], t, axis=0)
            r, lw, k, v, kk, a, gt, hmod = _rwkv_proj(x, m4, gn[1], hl, t, tm_mix, p, v_first if l > 0 else None)
            if l == 0:
                v_first = v
            o, s_new = _wkv_long(r, lw, k, v, kk, a, wkv0[l], w['lnx_w_a'][l], w['lnx_b_a'][l], w['rk_a'][l], t, tb)
            wkv_new.append(s_new)
            shift_new.append(hmod.reshape(n_seq, t, d)[:, -1])
            x = _out_proj(o, gt, x, m4, gn[1], w['w_o_a'][l], tm_mix, tiles(tm_mix))
        else:
            j = l - n_a
            if past is None:
                qa = _q_aug(x, m4, gn[1], w['w_q_b'][j], kv[2], tm_mix, tiles(tm_mix))
                o = _flash_prompt(qa, kv[0], kv[1], n_seq, _largest_tile(t, 512), 4)
            else:
                q = _q_proj(x, m4, gn[1], w['w_q_b'][j], tm_mix, tiles(tm_mix))
                o = _paged_attention(q.reshape(n_seq, t, d), *kv, _largest_tile(kv[2].shape[1], 8)).reshape(n, d)
            x = _out_proj(o, None, x, m4, gn[1], w['w_o_b'][j], tm_mix, tiles(tm_mix))
        x = _ffn(x, mod4(l, 2), gn[2], w['w_ffn_in'], w['w_ffn_out'], l, 1, tm_ffn, tiles(tm_ffn))
    return (x.reshape(n_seq, t, d), jnp.stack(wkv_new), jnp.stack(shift_new), k_new.reshape(n_seq, t, h, HEAD_DIM),
            v_new.reshape(n_seq, t, h, HEAD_DIM), logf.reshape(n_seq, t, h))


def kernel(x_prompt, x_sample, state_wkv, state_shift, cache_k, cache_v, cache_logf, page_table, c_prompt, c_sample,
           w_mod, b_mod, g_norm, w_ffn_in, w_ffn_out, mu_a, w_rkv_a, w_o_a, w0_a, w1_a, w2_a, a0_a, a1_a, a2_a,
           v0_a, v1_a, v2_a, g1_a, g2_a, kk_a, ka_a, rk_a, lnx_w_a, lnx_b_a, g_kv, w_kv, w_f, b_f, w_q_b, w_o_b):
    bp, _, d = x_prompt.shape
    bs = x_sample.shape[0]
    depth = w_mod.shape[0]
    n_a = state_wkv.shape[0]
    n_pool, page = cache_k.shape[:2]
    cast = lambda a: a.astype(BF16)
    w = dict(g_norm=g_norm, w_ffn_in=w_ffn_in, w_ffn_out=w_ffn_out, mu_a=mu_a, w_rkv_a=cast(w_rkv_a),
             w_o_a=cast(w_o_a), w0_a=w0_a, w1_a=cast(w1_a), w2_a=cast(w2_a), a0_a=a0_a, a1_a=cast(a1_a),
             a2_a=cast(a2_a), v0_a=v0_a, v1_a=cast(v1_a), v2_a=cast(v2_a), g1_a=cast(g1_a), g2_a=cast(g2_a),
             kk_a=kk_a, ka_a=ka_a, rk_a=rk_a.reshape(n_a, d), lnx_w_a=lnx_w_a, lnx_b_a=lnx_b_a, g_kv=g_kv,
             w_kv=cast(w_kv), w_f=cast(w_f), b_f=b_f, w_q_b=cast(w_q_b), w_o_b=cast(w_o_b))

    mods = _mods(jnp.concatenate([c_prompt, c_sample], axis=0), w_mod, b_mod)
    mods = mods.reshape(depth, bp + bs, 3, 3, d)

    n_heads = d // HEAD_DIM
    zeros_wkv = jnp.zeros((n_a, bp, n_heads, HEAD_DIM, HEAD_DIM), F32)
    zeros_shift = jnp.zeros((n_a, bp, d), F32)
    y_p, wkv_p, shift_p, k_p, v_p, logf_p = _trunk(x_prompt, mods[:, :bp], zeros_wkv, zeros_shift, None, w)
    pool_t = lambda c: jnp.transpose(c, (0, 2, 3, 1)).reshape(n_pool, d, page)
    past = (pool_t(cache_k), pool_t(cache_v), cache_logf, page_table)
    y_s, wkv_s, shift_s, k_s, v_s, logf_s = _trunk(x_sample, mods[:, bp:], state_wkv, state_shift, past, w)
    return (y_p, y_s, wkv_p, shift_p, k_p, v_p, logf_p, wkv_s, shift_s, k_s, v_s, logf_s)
```

```python
import functools

import jax
import jax.numpy as jnp
import numpy as np
from jax import lax
from jax.experimental import pallas as pl
from jax.experimental.pallas import tpu as pltpu

F32 = jnp.float32
BF16 = jnp.bfloat16

NORM_EPS = 1e-6
GN_EPS = 64e-5
MACARON_W = 0.5
NEG_INF = -1e30
HEAD_DIM = 64
LANES = 128
PAIR = LANES // HEAD_DIM
VMEM_LIMIT = 56 * 1024 * 1024

LOG2E = 1.4426950408889634
N_BIAS = 3

_NT = (((1,), (1,)), ((), ()))
_TN = (((0,), (0,)), ((), ()))


def _cparams(*sem):
    return pltpu.CompilerParams(dimension_semantics=sem, vmem_limit_bytes=VMEM_LIMIT)


def _dot(a, b):
    return jnp.dot(a.astype(BF16), b.astype(BF16), preferred_element_type=F32)


def _dot_nt(a, b):
    return lax.dot_general(a.astype(BF16), b.astype(BF16), _NT, preferred_element_type=F32)


def _dot_tn(a, b):
    return lax.dot_general(a.astype(BF16), b.astype(BF16), _TN, preferred_element_type=F32)


def _split3(x):
    hi = x.astype(BF16)
    r1 = x - hi.astype(F32)
    mid = r1.astype(BF16)
    lo = (r1 - mid.astype(F32)).astype(BF16)
    return hi, mid, lo


def _dot3(sel, x):
    hi, mid, lo = _split3(x)
    out = jnp.dot(sel, hi, preferred_element_type=F32)
    out = out + jnp.dot(sel, mid, preferred_element_type=F32)
    return out + jnp.dot(sel, lo, preferred_element_type=F32)


def _dot3_rhs(x, sel):
    hi, mid, lo = _split3(x)
    out = jnp.dot(hi, sel, preferred_element_type=F32)
    out = out + jnp.dot(mid, sel, preferred_element_type=F32)
    return out + jnp.dot(lo, sel, preferred_element_type=F32)


def _rms(x, g):
    return x * lax.rsqrt(jnp.mean(x * x, axis=-1, keepdims=True) + NORM_EPS) * g


def _sigmoid(x):
    return 1.0 / (1.0 + jnp.exp(-x))


def _silu(x):
    return x * _sigmoid(x)


def _head_ones(n):
    r = lax.broadcasted_iota(jnp.int32, (n, n), 0) // HEAD_DIM
    c = lax.broadcasted_iota(jnp.int32, (n, n), 1) // HEAD_DIM
    return jnp.where(r == c, 1.0, 0.0).astype(BF16)


def _head_sum(x, ones):
    n = ones.shape[0]
    hi = x.astype(BF16)
    lo = (x - hi.astype(F32)).astype(BF16)
    cols = []
    for i in range(0, x.shape[1], n):
        s = jnp.dot(hi[:, i:i + n], ones, preferred_element_type=F32)
        cols.append(s + jnp.dot(lo[:, i:i + n], ones, preferred_element_type=F32))
    return cols[0] if len(cols) == 1 else jnp.concatenate(cols, axis=1)


def _mods_kernel(c_ref, w_ref, b_ref, o_ref):
    o_ref[...] = _dot(_silu(c_ref[...]), w_ref[...]) + b_ref[...]


def _mods(c_all, w_mod, b_mod):
    depth, d, n9 = w_mod.shape
    nb = c_all.shape[0]
    tn = d
    return pl.pallas_call(
        _mods_kernel,
        grid=(depth, n9 // tn),
        in_specs=[
            pl.BlockSpec((nb, d), lambda l, j: (0, 0)),
            pl.BlockSpec((None, d, tn), lambda l, j: (l, 0, j)),
            pl.BlockSpec((None, 1, tn), lambda l, j: (l, 0, j)),
        ],
        out_specs=pl.BlockSpec((None, nb, tn), lambda l, j: (l, 0, j)),
        out_shape=jax.ShapeDtypeStruct((depth, nb, n9), F32),
        compiler_params=_cparams("arbitrary", "arbitrary"),
        name="mods",
    )(c_all, w_mod, b_mod.reshape(depth, 1, n9))


def _mod_spec(m4, tm, tiles_per_seq):
    _, _, rows, d = m4.shape
    if rows == 1:
        return pl.BlockSpec((None, 3, 1, d), lambda i, *_: (i // tiles_per_seq, 0, 0, 0))
    return pl.BlockSpec((None, 3, tm, d), lambda i, *_: (0, 0, i, 0))


def _ffn_kernel(x_ref, m_ref, g_ref, wg_ref, wu_ref, wo_ref, o_ref, h_scr, acc_scr):
    j = pl.program_id(1)

    @pl.when(j == 0)
    def _():
        h = _rms(x_ref[...], g_ref[0:1]) * (1.0 + m_ref[1]) + m_ref[0]
        h_scr[...] = h.astype(BF16)
        acc_scr[...] = jnp.zeros_like(acc_scr)

    h = h_scr[...]
    gate = _dot(h, wg_ref[...])
    up = _dot(h, wu_ref[...])
    acc_scr[...] += _dot(_silu(gate) * up, wo_ref[...])

    @pl.when(j == pl.num_programs(1) - 1)
    def _():
        o_ref[...] = x_ref[...] + MACARON_W * m_ref[2] * _rms(acc_scr[...], g_ref[1:2])


def _ffn(x, m4, gn, w_in, w_out, layer, sub, tm, tiles_per_seq):
    n, d = x.shape
    f = w_out.shape[2]
    tf = 256 if f % 256 == 0 else LANES
    nf = f // tf
    return pl.pallas_call(
        _ffn_kernel,
        grid=(n // tm, nf),
        in_specs=[
            pl.BlockSpec((tm, d), lambda i, j: (i, 0)),
            _mod_spec(m4, tm, tiles_per_seq),
            pl.BlockSpec((2, d), lambda i, j: (0, 0)),
            pl.BlockSpec((None, None, d, tf), lambda i, j: (layer, sub, 0, j)),
            pl.BlockSpec((None, None, d, tf), lambda i, j: (layer, sub, 0, j + nf)),
            pl.BlockSpec((None, None, tf, d), lambda i, j: (layer, sub, j, 0)),
        ],
        out_specs=pl.BlockSpec((tm, d), lambda i, j: (i, 0)),
        out_shape=jax.ShapeDtypeStruct((n, d), F32),
        scratch_shapes=[pltpu.VMEM((tm, d), BF16), pltpu.VMEM((tm, d), F32)],
        compiler_params=_cparams("parallel", "arbitrary"),
        name="ffn",
    )(x, m4, gn, w_in, w_in, w_out)


def _rwkv_proj_kernel(has_vfirst, seq_len, x_ref, m_ref, g_ref, hl_ref, mu_ref, wrkv_ref, w0_ref, w1_ref,
                      w2_ref, a0_ref, a1_ref, a2_ref, g1_ref, g2_ref, kka_ref, kaa_ref, *rest):
    if has_vfirst:
        v0_ref, v1_ref, v2_ref, vf_ref = rest[:4]
        rest = rest[4:]
    r_ref, lw_ref, k_ref, v_ref, kk_ref, a_ref, gt_ref, h_ref, carry_scr = rest
    tm, d = x_ref.shape
    i = pl.program_id(0)

    h = _rms(x_ref[...], g_ref[0:1]) * (1.0 + m_ref[1]) + m_ref[0]
    h_ref[...] = h
    row = lax.broadcasted_iota(jnp.int32, (tm, d), 0)
    rolled = pltpu.roll(h, 1, 0)
    if hl_ref.shape[0] == 1:
        tiles_per_seq = seq_len // tm
        first = jnp.where(i % tiles_per_seq == 0, hl_ref[...], carry_scr[...])
        h_prev = jnp.where(row == 0, first, rolled)
        carry_scr[...] = h[tm - 1:tm, :]
    else:
        h_prev = jnp.where(row % seq_len == 0, hl_ref[...], rolled)
    xx = h_prev - h
    xr, xw, xk, xv, xa, xg = [h + xx * mu_ref[j:j + 1] for j in range(6)]

    r = _dot(xr, wrkv_ref[0])
    k = _dot(xk, wrkv_ref[1])
    v = _dot(xv, wrkv_ref[2])
    z = w0_ref[...] + _dot(jnp.tanh(_dot(xw, w1_ref[...])), w2_ref[...])
    lw = -_sigmoid(z) * jnp.exp(F32(-0.5))
    a = _sigmoid(a0_ref[...] + _dot(_dot(xa, a1_ref[...]), a2_ref[...]))
    gt = _dot(_sigmoid(_dot(xg, g1_ref[...])), g2_ref[...])
    if has_vfirst:
        mix = _sigmoid(v0_ref[...] + _dot(_dot(xv, v1_ref[...]), v2_ref[...]))
        v = v + (vf_ref[...] - v) * mix
    kk = k * kka_ref[...]
    ones = _head_ones(min(d, 2 * LANES))
    kk = kk * lax.rsqrt(jnp.maximum(_head_sum(kk * kk, ones), 1e-24))
    k = k * (1.0 + (a - 1.0) * kaa_ref[...])

    r_ref[...] = r
    lw_ref[...] = lw
    k_ref[...] = k
    v_ref[...] = v
    kk_ref[...] = kk
    a_ref[...] = a
    gt_ref[...] = gt


def _rwkv_proj(x, m4, gn, hl, seq_len, tm, p, v_first):
    n, d = x.shape
    tiles_per_seq = max(seq_len // tm, 1)
    has_vfirst = v_first is not None
    row = lambda a: a.reshape(1, d)
    const = lambda a: pl.BlockSpec(a.shape, lambda i: (0,) * a.ndim)
    tok = pl.BlockSpec((tm, d), lambda i: (i, 0))
    if hl.ndim == 3:
        hl_spec = pl.BlockSpec((None, 1, d), lambda i: (i // tiles_per_seq, 0, 0))
    else:
        hl_spec = tok
    args = [x, m4, gn, hl, p['mu'], p['w_rkv'], row(p['w0']), p['w1'], p['w2'], row(p['a0']), p['a1'], p['a2'],
            p['g1'], p['g2'], row(p['kk']), row(p['ka'])]
    specs = [tok, _mod_spec(m4, tm, tiles_per_seq), const(gn), hl_spec] + [const(a) for a in args[4:]]
    if has_vfirst:
        extra = [row(p['v0']), p['v1'], p['v2']]
        args += extra + [v_first]
        specs += [const(a) for a in extra] + [tok]
    return pl.pallas_call(
        functools.partial(_rwkv_proj_kernel, has_vfirst, seq_len),
        grid=(n // tm,),
        in_specs=specs,
        out_specs=[tok] * 8,
        out_shape=[jax.ShapeDtypeStruct((n, d), F32)] * 8,
        scratch_shapes=[pltpu.VMEM((1, d), F32)],
        compiler_params=_cparams("arbitrary"),
        name="rwkv_proj",
    )(*args)


LONG_CHUNK = 64


def _long_masks():
    n2 = PAIR * LONG_CHUNK
    r = np.arange(n2)[:, None]
    c = np.arange(n2)[None, :]
    masks = [c < r, c <= r, c == r]
    b = 1
    while b < LONG_CHUNK:
        masks.append((r // (2 * b) == c // (2 * b)) & ((r // b) % 2 == 1) & ((c // b) % 2 == 0))
        b *= 2
    return jnp.asarray(np.stack(masks).astype(np.float32))


def _wkv_long_kernel(r_ref, lw_ref, k_ref, v_ref, kk_ref, a_ref, s0_ref, lnw_ref, lnb_ref, rk_ref, mask_ref,
                     o_ref, sout_ref, st_scr, u2_scr, o2_scr, wq_scr, bk_scr, lkk_scr, v2_scr, kt_scr, bt_scr,
                     lrb_scr, lvt_scr, x_scr, gc_scr, lab_scr, tinv_scr, ol_scr, ds_scr, ut_scr):
    c = LONG_CHUNK
    n2 = PAIR * c
    tb, d = r_ref.shape
    n_pairs = d // LANES
    t_idx = pl.program_id(1)

    @pl.when(t_idx == 0)
    def _():
        zero = jnp.zeros((HEAD_DIM, HEAD_DIM), F32)
        for p in range(n_pairs):
            top = jnp.concatenate([s0_ref[PAIR * p], zero], axis=1)
            bottom = jnp.concatenate([zero, s0_ref[PAIR * p + 1]], axis=1)
            st_scr[p] = jnp.concatenate([top, bottom], axis=0).T

    rows_in = min(tb, c)
    n_chunks = max(tb // c, 1)
    items = [(ic, p) for ic in range(n_chunks) for p in range(n_pairs)]
    dotf = lambda x, y: jnp.dot(x, y, preferred_element_type=F32)
    lane = lax.broadcasted_iota(jnp.int32, (c, LANES), 1)
    head_lane = [lane // HEAD_DIM == h for h in range(PAIR)]
    tri = jnp.where(lax.broadcasted_iota(jnp.int32, (c, c), 1) <= lax.broadcasted_iota(jnp.int32, (c, c), 0),
                    1.0, 0.0).astype(BF16)

    def stack(x):
        return jnp.concatenate([jnp.where(m, x, 0.0) for m in head_lane], axis=0)

    for i, (ic, p) in enumerate(items):
        rows, ls = slice(ic * c, ic * c + rows_in), slice(p * LANES, (p + 1) * LANES)
        rc, lwc, kc, vc, kkc, ac = [ref[rows, ls] for ref in (r_ref, lw_ref, k_ref, v_ref, kk_ref, a_ref)]
        if rows_in < c:
            pad = jnp.zeros((c - rows_in, LANES), F32)
            rc, lwc, kc, vc, kkc, ac = [jnp.concatenate([x, pad], axis=0) for x in (rc, lwc, kc, vc, kkc, ac)]
        cum = _dot3(tri, lwc)
        g = jnp.exp(cum)
        gi = jnp.exp(-cum)
        g_end = g[c - 1:c, :]
        bb = ac * kkc * gi
        kb = kc * gi
        wq_scr[i, :n2] = stack(-kkc * jnp.exp(cum - lwc)).astype(BF16)
        wq_scr[i, n2:] = stack(rc * g).astype(BF16)
        bk_scr[i, :n2] = stack(bb).astype(BF16)
        bk_scr[i, n2:] = stack(kb).astype(BF16)
        v2_scr[i] = stack(vc).astype(BF16)
        kt_scr[i] = stack(kb * g_end).astype(BF16)
        bt_scr[i] = stack(bb * g_end).T.astype(BF16)
        gc_scr[i] = jnp.broadcast_to(g_end, (LANES, LANES)).T

    for i in range(len(items)):
        scores = lax.dot_general(wq_scr[i], bk_scr[i], _NT, preferred_element_type=F32)
        l_ab = scores[:n2, :n2] * mask_ref[0]
        lab_scr[i] = l_ab
        tinv_scr[i] = mask_ref[2] + l_ab * mask_ref[3]
        lkk_scr[i, :n2] = (scores[:n2, n2:] * mask_ref[0]).astype(BF16)
        lkk_scr[i, n2:] = (scores[n2:, n2:] * mask_ref[1]).astype(BF16)
        lrb_scr[i] = (scores[n2:, :n2] * mask_ref[1]).astype(BF16)

    for i in range(len(items)):
        lv = dotf(lkk_scr[i], v2_scr[i])
        lvt_scr[i] = lv[:n2].astype(BF16)
        ol_scr[i] = lv[n2:]
        ds_scr[i] = lax.dot_general(kt_scr[i], v2_scr[i], _TN, preferred_element_type=F32)

    for level in range(4, mask_ref.shape[0]):
        for i in range(len(items)):
            x_scr[i] = dotf((lab_scr[i] * mask_ref[level]).astype(BF16), tinv_scr[i].astype(BF16)).astype(BF16)
        for i in range(len(items)):
            tinv = tinv_scr[i]
            tinv_scr[i] = tinv + dotf(tinv.astype(BF16), x_scr[i])

    for i in range(len(items)):
        tinv = tinv_scr[i].astype(BF16)
        ut_scr[i] = dotf(tinv, lvt_scr[i])
        wq_scr[i, :n2] = dotf(tinv, wq_scr[i, :n2]).astype(BF16)

    def advance(ic, carry):
        rows = pl.ds(pl.multiple_of(ic * c, c), rows_in)
        for p in range(n_pairs):
            i = ic * n_pairs + p
            ws = dotf(wq_scr[i], st_scr[p].astype(BF16))
            u2_scr[p] = (ws[:n2] + ut_scr[i]).astype(BF16)
            o2_scr[p] = ws[n2:] + ol_scr[i]
        for p in range(n_pairs):
            i = ic * n_pairs + p
            u2 = u2_scr[p]
            o2 = o2_scr[p] + dotf(lrb_scr[i], u2)
            st_scr[p] = gc_scr[i] * st_scr[p] + dotf(bt_scr[i], u2) + ds_scr[i]
            o = o2[0:c]
            for h in range(1, PAIR):
                o = o + o2[h * c:(h + 1) * c]
            o_ref[rows, p * LANES:(p + 1) * LANES] = o[:rows_in]
        return carry

    lax.fori_loop(0, n_chunks, advance, 0)

    ones = _head_ones(min(d, 2 * LANES))
    o = o_ref[...]
    cen = o - _head_sum(o, ones) * (1.0 / HEAD_DIM)
    var = _head_sum(cen * cen, ones) * (1.0 / HEAD_DIM)
    bonus = _head_sum(r_ref[...] * k_ref[...] * rk_ref[...], ones) * v_ref[...]
    o_ref[...] = cen * lax.rsqrt(var + GN_EPS) * lnw_ref[...] + lnb_ref[...] + bonus

    @pl.when(t_idx == pl.num_programs(1) - 1)
    def _():
        for p in range(n_pairs):
            s = st_scr[p].T
            sout_ref[PAIR * p] = s[:HEAD_DIM, :HEAD_DIM]
            sout_ref[PAIR * p + 1] = s[HEAD_DIM:, HEAD_DIM:]


def _wkv_long(r, lw, k, v, kk, a, s0, lnw, lnb, rk, seq_len, tb):
    assert PAIR == 2
    n, d = r.shape
    n_seq = n // seq_len
    n_pairs = d // LANES
    nt = seq_len // tb
    n_items = max(tb // LONG_CHUNK, 1) * n_pairs
    n2 = PAIR * LONG_CHUNK
    assert n2 == LANES
    masks = _long_masks()
    tok = pl.BlockSpec((tb, d), lambda b, t: (b * nt + t, 0))
    vec = pl.BlockSpec((1, d), lambda b, t: (0, 0))
    st = pl.BlockSpec((None, PAIR * n_pairs, HEAD_DIM, HEAD_DIM), lambda b, t: (b, 0, 0, 0))
    per = lambda rows, dt: pltpu.VMEM((n_items, rows, LANES), dt)
    return pl.pallas_call(
        _wkv_long_kernel,
        grid=(n_seq, nt),
        in_specs=[tok] * 6 + [st, vec, vec, vec, pl.BlockSpec(masks.shape, lambda b, t: (0, 0, 0))],
        out_specs=[tok, st],
        out_shape=[jax.ShapeDtypeStruct((n, d), F32), jax.ShapeDtypeStruct(s0.shape, F32)],
        scratch_shapes=[pltpu.VMEM((n_pairs, LANES, LANES), F32), pltpu.VMEM((n_pairs, n2, LANES), BF16),
                        pltpu.VMEM((n_pairs, n2, LANES), F32)]
        + [per(2 * n2, BF16)] * 3 + [per(n2, BF16)] * 6 + [per(n2, F32)] * 6,
        compiler_params=_cparams("parallel", "arbitrary"),
        name="wkv_long",
    )(r, lw, k, v, kk, a, s0, lnw.reshape(1, d), lnb.reshape(1, d), rk.reshape(1, d), masks)


def _out_kernel(has_gate, res_w, *refs):
    if has_gate:
        a_ref, gt_ref, x_ref, m_ref, g_ref, w_ref, o_ref = refs
        a = a_ref[...] * gt_ref[...]
    else:
        a_ref, x_ref, m_ref, g_ref, w_ref, o_ref = refs
        a = a_ref[...]
    y = _dot(a, w_ref[...])
    o_ref[...] = x_ref[...] + res_w * m_ref[2] * _rms(y, g_ref[1:2])


def _out_proj(a, gate, x, m4, gn, w, tm, tiles_per_seq, res_w=1.0):
    n, d = x.shape
    tok = pl.BlockSpec((tm, d), lambda i: (i, 0))
    args = [a] + ([gate] if gate is not None else []) + [x, m4, gn, w]
    specs = [tok] * (len(args) - 3) + [_mod_spec(m4, tm, tiles_per_seq), pl.BlockSpec((2, d), lambda i: (0, 0)),
                                       pl.BlockSpec(w.shape, lambda i: (0, 0))]
    return pl.pallas_call(
        functools.partial(_out_kernel, gate is not None, res_w),
        grid=(n // tm,),
        in_specs=specs,
        out_specs=tok,
        out_shape=jax.ShapeDtypeStruct((n, d), F32),
        compiler_params=_cparams("parallel"),
        name="out_proj",
    )(*args)


def _kv_kernel(x_ref, g_ref, wkv_ref, wf_ref, bf_ref, k_ref, v_ref, kb_ref, vb_ref, lf_ref):
    d = x_ref.shape[1]
    s = _rms(x_ref[...], g_ref[...])
    kv = _dot(s, wkv_ref[...])
    k = kv[:, :d]
    v = kv[:, d:]
    k_ref[...] = k
    v_ref[...] = v
    kb_ref[...] = k.astype(BF16)
    vb_ref[...] = v.astype(BF16)
    z = _dot(s, wf_ref[...]) + bf_ref[...]
    lf_ref[...] = jnp.minimum(z, 0.0) - jnp.log(1.0 + jnp.exp(-jnp.abs(z)))


def _shared_kv(x, g_kv, w_kv, w_f, b_f, tm):
    n, d = x.shape
    h = w_f.shape[1]
    tok = pl.BlockSpec((tm, d), lambda i: (i, 0))
    return pl.pallas_call(
        _kv_kernel,
        grid=(n // tm,),
        in_specs=[tok, pl.BlockSpec((1, d), lambda i: (0, 0)), pl.BlockSpec(w_kv.shape, lambda i: (0, 0)),
                  pl.BlockSpec(w_f.shape, lambda i: (0, 0)), pl.BlockSpec((1, h), lambda i: (0, 0))],
        out_specs=[tok, tok, tok, tok, pl.BlockSpec((tm, h), lambda i: (i, 0))],
        out_shape=[jax.ShapeDtypeStruct((n, d), F32)] * 2 + [jax.ShapeDtypeStruct((n, d), BF16)] * 2
        + [jax.ShapeDtypeStruct((n, h), F32)],
        compiler_params=_cparams("parallel"),
        name="shared_kv",
    )(x, g_kv.reshape(1, d), w_kv, w_f, b_f.reshape(1, h))


def _cumsum_kernel(n_grp, has_new, has_parts, pt_ref, *refs):
    lf_refs = refs[:n_grp]
    rest = refs[n_grp:]
    if has_new:
        new_ref, ck_ref, cknew_ref, carry_scr = rest
    elif has_parts:
        ck_ref, parts_ref, carry_scr = rest
    else:
        ck_ref, carry_scr = rest
    del pt_ref
    j = pl.program_id(1)
    page, h = lf_refs[0].shape

    @pl.when(j == 0)
    def _():
        carry_scr[...] = jnp.zeros_like(carry_scr)

    ri = lax.broadcasted_iota(jnp.int32, (page, page), 0)
    ci = lax.broadcasted_iota(jnp.int32, (page, page), 1)
    tri = jnp.where(ci <= ri, 1.0, 0.0).astype(BF16)
    local = [_dot3(tri, lf[...]) for lf in lf_refs]
    base = carry_scr[...]
    for i in range(n_grp):
        ck = base + local[i]
        ck_ref[i * page:(i + 1) * page, :] = ck
        base = ck[page - 1:page, :]

        if has_parts:
            src_h = lax.broadcasted_iota(jnp.int32, (h, LANES), 0)
            dst = lax.broadcasted_iota(jnp.int32, (h, LANES), 1)
            out = jnp.where(lax.broadcasted_iota(jnp.int32, (page, LANES), 1) // h == N_BIAS, 1.0, 0.0)
            for x, part in enumerate(_split3(ck * LOG2E)):
                out = out + jnp.dot(part, jnp.where(dst == x * h + src_h, 1.0, 0.0).astype(BF16),
                                    preferred_element_type=F32)
            parts_ref[i * page:(i + 1) * page, :] = out.astype(BF16)
    carry_scr[...] = base

    if has_new:
        @pl.when(j == pl.num_programs(1) - 1)
        def _():
            t = new_ref.shape[0]
            cknew_ref[...] = base + _dot3(tri[:t, :t], new_ref[...])


def _cumsum_pages(page_table, lf_pages, lf_new, with_parts=False):
    n_seq, n_pages = page_table.shape
    _, page, h = lf_pages.shape
    has_new = lf_new is not None
    n_grp = _largest_tile(n_pages, 8)

    def page_spec(i):
        return pl.BlockSpec((None, page, h), lambda b, j, pt: (pt[b, j * n_grp + i], 0, 0))

    in_specs = [page_spec(i) for i in range(n_grp)]
    out_specs = [pl.BlockSpec((None, n_grp * page, h), lambda b, j, pt: (b, j, 0))]
    out_shape = [jax.ShapeDtypeStruct((n_seq, n_pages * page, h), F32)]
    args = [lf_pages] * n_grp
    if has_new:
        t = lf_new.shape[1]
        in_specs.append(pl.BlockSpec((None, t, h), lambda b, j, pt: (b, 0, 0)))
        out_specs.append(pl.BlockSpec((None, t, h), lambda b, j, pt: (b, 0, 0)))
        out_shape.append(jax.ShapeDtypeStruct((n_seq, t, h), F32))
        args.append(lf_new)
    elif with_parts:
        out_specs.append(pl.BlockSpec((None, n_grp * page, LANES), lambda b, j, pt: (b, j, 0)))
        out_shape.append(jax.ShapeDtypeStruct((n_seq, n_pages * page, LANES), BF16))
    return pl.pallas_call(
        functools.partial(_cumsum_kernel, n_grp, has_new, with_parts and not has_new),
        grid_spec=pltpu.PrefetchScalarGridSpec(
            num_scalar_prefetch=1, grid=(n_seq, n_pages // n_grp), in_specs=in_specs, out_specs=out_specs,
            scratch_shapes=[pltpu.VMEM((1, h), F32)]),
        out_shape=out_shape,
        compiler_params=_cparams("parallel", "arbitrary"),
        name="cumsum",
    )(page_table, *args)


def _q_kernel(x_ref, m_ref, g_ref, w_ref, q_ref):
    h = _rms(x_ref[...], g_ref[0:1]) * (1.0 + m_ref[1]) + m_ref[0]
    q_ref[...] = (_dot(h, w_ref[...]) * (HEAD_DIM ** -0.5)).astype(BF16)


def _q_proj(x, m4, gn, w, tm, tiles_per_seq):
    n, d = x.shape
    tok = pl.BlockSpec((tm, d), lambda i: (i, 0))
    return pl.pallas_call(
        _q_kernel,
        grid=(n // tm,),
        in_specs=[tok, _mod_spec(m4, tm, tiles_per_seq), pl.BlockSpec((2, d), lambda i: (0, 0)),
                  pl.BlockSpec(w.shape, lambda i: (0, 0))],
        out_specs=tok,
        out_shape=jax.ShapeDtypeStruct((n, d), BF16),
        compiler_params=_cparams("parallel"),
        name="q_proj",
    )(x, m4, gn, w)


def _pair_to_tiles():
    m = np.zeros((LANES, PAIR * LANES), np.float32)
    r = np.arange(LANES)
    m[r, (r // HEAD_DIM) * LANES + r % HEAD_DIM] = 1.0
    return jnp.asarray(m, BF16)


def _bias_place(h, for_keys):
    m = np.zeros((LANES, h * LANES), np.float32)
    for hh in range(h):
        for x in range(N_BIAS):
            if for_keys:
                m[N_BIAS * h + hh, hh * LANES + HEAD_DIM + x] = 1.0
                m[x * h + hh, hh * LANES + HEAD_DIM + N_BIAS + x] = -1.0
            else:
                m[x * h + hh, hh * LANES + HEAD_DIM + x] = 1.0
                m[N_BIAS * h + hh, hh * LANES + HEAD_DIM + N_BIAS + x] = 1.0
    return jnp.asarray(m, BF16)


def _to_head_tiles(x, place, extra):
    cols = []
    for p in range(x.shape[1] // LANES):
        sl = slice(p * PAIR * LANES, (p + 1) * PAIR * LANES)
        cols.append(jnp.dot(x[:, p * LANES:(p + 1) * LANES], place, preferred_element_type=F32) + extra[:, sl])
    return jnp.concatenate(cols, axis=1)


def _q_aug_kernel(x_ref, m_ref, g_ref, w_ref, parts_ref, place_ref, bias_ref, q_ref):
    h = _rms(x_ref[...], g_ref[0:1]) * (1.0 + m_ref[1]) + m_ref[0]
    q = (_dot(h, w_ref[...]) * (HEAD_DIM ** -0.5 * LOG2E)).astype(BF16)
    bias = jnp.dot(parts_ref[...], bias_ref[...], preferred_element_type=F32)
    q_ref[...] = _to_head_tiles(q, place_ref[...], bias).astype(BF16)


def _q_aug(x, m4, gn, w, parts, tm, tiles_per_seq):
    n, d = x.shape
    tok = pl.BlockSpec((tm, d), lambda i: (i, 0))
    place, bias = _pair_to_tiles(), _bias_place(d // HEAD_DIM, False)
    const = lambda a: pl.BlockSpec(a.shape, lambda i: (0, 0))
    return pl.pallas_call(
        _q_aug_kernel,
        grid=(n // tm,),
        in_specs=[tok, _mod_spec(m4, tm, tiles_per_seq), pl.BlockSpec((2, d), lambda i: (0, 0)), const(w),
                  pl.BlockSpec((tm, LANES), lambda i: (i, 0)), const(place), const(bias)],
        out_specs=pl.BlockSpec((tm, PAIR * d), lambda i: (i, 0)),
        out_shape=jax.ShapeDtypeStruct((n, PAIR * d), BF16),
        compiler_params=_cparams("parallel"),
        name="q_aug",
    )(x, m4, gn, w, parts, place, bias)


def _kv_aug_kernel(k_ref, v_ref, parts_ref, place_ref, bias_ref, ka_ref, va_ref):
    tm, d2 = ka_ref.shape
    bias = jnp.dot(parts_ref[...], bias_ref[...], preferred_element_type=F32)
    ka_ref[...] = _to_head_tiles(k_ref[...], place_ref[...], bias).astype(BF16)
    ones = jnp.where(lax.broadcasted_iota(jnp.int32, (1, d2), 1) % LANES >= HEAD_DIM, 1.0, 0.0)
    va_ref[...] = _to_head_tiles(v_ref[...], place_ref[...], ones).astype(BF16)


def _kv_aug(kb, vb, parts, tm):
    n, d = kb.shape
    tok = pl.BlockSpec((tm, d), lambda i: (i, 0))
    wide = pl.BlockSpec((tm, PAIR * d), lambda i: (i, 0))
    place, bias = _pair_to_tiles(), _bias_place(d // HEAD_DIM, True)
    const = lambda a: pl.BlockSpec(a.shape, lambda i: (0, 0))
    return pl.pallas_call(
        _kv_aug_kernel,
        grid=(n // tm,),
        in_specs=[tok, tok, pl.BlockSpec((tm, LANES), lambda i: (i, 0)), const(place), const(bias)],
        out_specs=[wide, wide],
        out_shape=[jax.ShapeDtypeStruct((n, PAIR * d), BF16)] * 2,
        compiler_params=_cparams("parallel"),
        name="kv_aug",
    )(kb, vb, parts, place, bias)


def _flash_kernel(qi_ref, kj_ref, q_ref, k_ref, v_ref, o_ref, m_scr, acc_scr):
    n = pl.program_id(2)
    qi = qi_ref[n]
    kj = kj_ref[n]
    tq = q_ref.shape[0]
    tk = k_ref.shape[0]
    n_heads = q_ref.shape[1] // LANES

    @pl.when(kj == 0)
    def _():
        m_scr[...] = jnp.full_like(m_scr, NEG_INF)
        acc_scr[...] = jnp.zeros_like(acc_scr)

    def update(masked):
        if masked:
            valid = (lax.broadcasted_iota(jnp.int32, (tq, tk), 1)
                     <= lax.broadcasted_iota(jnp.int32, (tq, tk), 0))
        for h in range(n_heads):
            sl = slice(h * LANES, (h + 1) * LANES)
            s = lax.dot_general(q_ref[:, sl], k_ref[:, sl], _NT, preferred_element_type=F32)
            if masked:
                s = jnp.where(valid, s, NEG_INF)
            cols = [s[:, c * LANES:(c + 1) * LANES] for c in range(tk // LANES)]
            m_cur = cols[0]
            for col in cols[1:]:
                m_cur = jnp.maximum(m_cur, col)
            m_old = m_scr[h]
            m_new = jnp.maximum(m_old, jnp.max(m_cur, axis=1, keepdims=True))
            p = jnp.concatenate([jnp.exp2(col - m_new) for col in cols], axis=1).astype(BF16)
            acc_scr[h] = jnp.exp2(m_old - m_new) * acc_scr[h] + jnp.dot(p, v_ref[:, sl],
                                                                        preferred_element_type=F32)
            m_scr[h] = m_new

    @pl.when(kj < qi)
    def _():
        update(False)

    @pl.when(kj == qi)
    def _():
        update(True)
        lane = lax.broadcasted_iota(jnp.int32, (tq, LANES), 1)
        outs = []
        for h in range(n_heads):
            acc = acc_scr[h]
            outs.append(acc / pltpu.roll(acc, HEAD_DIM, 1))
        for p in range(n_heads // PAIR):
            o_ref[:, p * LANES:(p + 1) * LANES] = jnp.where(
                lane < HEAD_DIM, outs[PAIR * p], pltpu.roll(outs[PAIR * p + 1], HEAD_DIM, 1))


def _flash_prompt(qa, ka, va, n_seq, tq, heads_per_step):
    n, d2 = qa.shape
    t = n // n_seq
    nq = t // tq
    width = heads_per_step * LANES
    qi = jnp.asarray([i for i in range(nq) for _ in range(i + 1)], jnp.int32)
    kj = jnp.asarray([j for i in range(nq) for j in range(i + 1)], jnp.int32)
    return pl.pallas_call(
        _flash_kernel,
        grid_spec=pltpu.PrefetchScalarGridSpec(
            num_scalar_prefetch=2,
            grid=(n_seq, d2 // width, int(qi.shape[0])),
            in_specs=[
                pl.BlockSpec((tq, width), lambda b, p, s, qi, kj: (b * nq + qi[s], p)),
                pl.BlockSpec((tq, width), lambda b, p, s, qi, kj: (b * nq + kj[s], p)),
                pl.BlockSpec((tq, width), lambda b, p, s, qi, kj: (b * nq + kj[s], p)),
            ],
            out_specs=pl.BlockSpec((tq, width // PAIR), lambda b, p, s, qi, kj: (b * nq + qi[s], p)),
            scratch_shapes=[pltpu.VMEM((heads_per_step, tq, LANES), F32),
                            pltpu.VMEM((heads_per_step, tq, LANES), F32)]),
        out_shape=jax.ShapeDtypeStruct((n, d2 // PAIR), F32),
        compiler_params=_cparams("parallel", "parallel", "arbitrary"),
        name="fox_prompt",
    )(qi, kj, qa, ka, va)


def _paged_kernel(n_grp, pt_ref, q_ref, cq_ref, ckp_ref, ckn_ref, kn_ref, vn_ref, *rest):
    t, d = q_ref.shape
    n_heads = d // HEAD_DIM
    rows = n_heads * t
    k_refs = rest[:n_grp]
    v_refs = rest[n_grp:2 * n_grp]
    o_ref, qbd_scr, m_scr, l_scr, acc_scr = rest[2 * n_grp:]
    page = k_refs[0].shape[1]
    del pt_ref
    g = pl.program_id(1)

    row_head = lax.broadcasted_iota(jnp.int32, (rows, d), 0) // t
    lane_head = lax.broadcasted_iota(jnp.int32, (rows, d), 1) // HEAD_DIM
    expand = jnp.where(lax.broadcasted_iota(jnp.int32, (rows, n_heads), 0) // t
                       == lax.broadcasted_iota(jnp.int32, (rows, n_heads), 1), 1.0, 0.0).astype(BF16)

    @pl.when(g == 0)
    def _():
        q = q_ref[...].astype(F32)
        q_rows = jnp.concatenate([q] * n_heads, axis=0)
        qbd_scr[...] = jnp.where(row_head == lane_head, q_rows, 0.0).astype(BF16)
        m_scr[...] = jnp.full_like(m_scr, NEG_INF)
        l_scr[...] = jnp.zeros_like(l_scr)
        acc_scr[...] = jnp.zeros_like(acc_scr)

    qbd = qbd_scr[...]
    cq = cq_ref[...]

    def attend(s, weighted_values):
        m_old = m_scr[...]
        m_new = jnp.maximum(m_old, jnp.max(s, axis=1, keepdims=True))
        alpha = jnp.exp(m_old - m_new)
        p = jnp.exp(s - m_new)
        l_scr[...] = alpha * l_scr[...] + jnp.sum(p, axis=1, keepdims=True)
        acc_scr[...] = alpha * acc_scr[...] + weighted_values(p)
        m_scr[...] = m_new

    kts = [kr[...].astype(BF16) for kr in k_refs]
    vts = [vr[...].astype(BF16) for vr in v_refs]

    def past_values(p):
        pv = _dot_nt(p[:, 0:page], vts[0])
        for i in range(1, n_grp):
            pv = pv + _dot_nt(p[:, i * page:(i + 1) * page], vts[i])
        return pv

    s = jnp.concatenate([_dot(qbd, kt) for kt in kts], axis=1)
    s = s + (cq - _dot3(expand, ckp_ref[...]))
    attend(s, past_values)

    @pl.when(g == pl.num_programs(1) - 1)
    def _():
        pad = jnp.zeros((page - t, d), F32)
        k_new = jnp.concatenate([kn_ref[...], pad], axis=0)
        v_new = jnp.concatenate([vn_ref[...], pad], axis=0)
        s_new = _dot_nt(qbd, k_new) + (cq - _dot3(expand, ckn_ref[...]))
        key = lax.broadcasted_iota(jnp.int32, (rows, page), 1)
        query = lax.broadcasted_iota(jnp.int32, (rows, page), 0) % t
        attend(jnp.where((key <= query) & (key < t), s_new, NEG_INF), lambda p: _dot(p, v_new))
        out = acc_scr[...] / l_scr[...]
        out = jnp.where(row_head == lane_head, out, 0.0).reshape(n_heads, t, d)
        o_ref[...] = jnp.sum(out, axis=0)


def _paged_attention(q, pool_kt, pool_vt, page_table, ck_past, ck_new, k_new, v_new, n_grp):
    n_seq, t, d = q.shape
    n_pages = page_table.shape[1]
    page = pool_kt.shape[2]
    h = d // HEAD_DIM
    rows = h * t
    assert n_pages % n_grp == 0
    cq_col = ck_new.transpose(0, 2, 1).reshape(n_seq, rows, 1)
    ckp_rows = ck_past.transpose(0, 2, 1)
    ckn_rows = jnp.pad(ck_new.transpose(0, 2, 1), ((0, 0), (0, 0), (0, page - t)))
    per_seq = lambda shape: pl.BlockSpec((None,) + shape, lambda b, g, pt: (b, 0, 0))

    def page_spec(i):
        return pl.BlockSpec((None, d, page), lambda b, g, pt: (pt[b, g * n_grp + i], 0, 0))

    in_specs = [per_seq((t, d)), per_seq((rows, 1)),
                pl.BlockSpec((None, h, n_grp * page), lambda b, g, pt: (b, 0, g)),
                per_seq((h, page)), per_seq((t, d)), per_seq((t, d))]
    in_specs += [page_spec(i) for i in range(n_grp)] * 2
    return pl.pallas_call(
        functools.partial(_paged_kernel, n_grp),
        grid_spec=pltpu.PrefetchScalarGridSpec(
            num_scalar_prefetch=1, grid=(n_seq, n_pages // n_grp), in_specs=in_specs,
            out_specs=per_seq((t, d)),
            scratch_shapes=[pltpu.VMEM((rows, d), BF16), pltpu.VMEM((rows, 1), F32), pltpu.VMEM((rows, 1), F32),
                            pltpu.VMEM((rows, d), F32)]),
        out_shape=jax.ShapeDtypeStruct((n_seq, t, d), F32),
        compiler_params=_cparams("parallel", "arbitrary"),
        name="fox_sample",
    )(page_table, q, cq_col, ckp_rows, ckn_rows, k_new, v_new, *([pool_kt] * n_grp), *([pool_vt] * n_grp))


def _largest_tile(n, cap):
    t = min(n, cap)
    while n % t:
        t //= 2
    return t


def _trunk(x3, mods, wkv0, shift0, past, w):
    n_seq, t, d = x3.shape
    n = n_seq * t
    depth = mods.shape[0]
    n_a = wkv0.shape[0]
    h = d // HEAD_DIM
    x = x3.reshape(n, d)
    per_seq = t >= 256
    if per_seq:
        tm_ffn, tm_mix, tm_lin = _largest_tile(t, 1024), _largest_tile(t, 256), _largest_tile(t, 512)
        tb = _largest_tile(t, 4 * LONG_CHUNK)
    else:
        tm_ffn = tm_mix = tm_lin = n
        tb = t

    def mod4(l, sub):
        m = mods[l, :, sub]
        if per_seq:
            return m.reshape(n_seq, 3, 1, d)
        return jnp.repeat(m, t, axis=0).transpose(1, 0, 2).reshape(1, 3, n, d)

    def tiles(tm):
        return max(t // tm, 1)

    v_first = None
    wkv_new, shift_new = [], []
    kv = None
    for l in range(depth):
        if l == n_a:
            k_new, v_new, kb, vb, logf = _shared_kv(x, w['g_kv'], w['w_kv'], w['w_f'], w['b_f'], tm_lin)
            if past is None:
                ident = jnp.arange(n // LANES, dtype=jnp.int32).reshape(n_seq, t // LANES)
                _, parts = _cumsum_pages(ident, logf.reshape(n // LANES, LANES, h), None, with_parts=True)
                parts = parts.reshape(n, LANES)
                kv = list(_kv_aug(kb, vb, parts, tm_lin)) + [parts]
            else:
                cache_k, cache_v, cache_logf, page_table = past
                ck_past, ck_new = _cumsum_pages(page_table, cache_logf, logf.reshape(n_seq, t, h))
                kv = [cache_k, cache_v, page_table, ck_past, ck_new,
                      k_new.reshape(n_seq, t, d), v_new.reshape(n_seq, t, d)]
        gn = w['g_norm'][l]
        x = _ffn(x, mod4(l, 0), gn[0], w['w_ffn_in'], w['w_ffn_out'], l, 0, tm_ffn, tiles(tm_ffn))
        m4 = mod4(l, 1)
        if l < n_a:
            p = {key: w[key + '_a'][l] for key in ('mu', 'w_rkv', 'w0', 'w1', 'w2', 'a0', 'a1', 'a2', 'g1', 'g2',
                                                   'kk', 'ka')}
            if l > 0:
                p.update({key: w[key + '_a'][l - 1] for key in ('v0', 'v1', 'v2')})
            hl = shift0[l].reshape(n_seq, 1, d) if per_seq else jnp.repeat(shift0[l], t, axis=0)
            r, lw, k, v, kk, a, gt, hmod = _rwkv_proj(x, m4, gn[1], hl, t, tm_mix, p, v_first if l > 0 else None)
            if l == 0:
                v_first = v
            o, s_new = _wkv_long(r, lw, k, v, kk, a, wkv0[l], w['lnx_w_a'][l], w['lnx_b_a'][l], w['rk_a'][l], t, tb)
            wkv_new.append(s_new)
            shift_new.append(hmod.reshape(n_seq, t, d)[:, -1])
            x = _out_proj(o, gt, x, m4, gn[1], w['w_o_a'][l], tm_lin, tiles(tm_lin))
        else:
            j = l - n_a
            if past is None:
                qa = _q_aug(x, m4, gn[1], w['w_q_b'][j], kv[2], tm_lin, tiles(tm_lin))
                o = _flash_prompt(qa, kv[0], kv[1], n_seq, _largest_tile(t, 512), 4)
            else:
                q = _q_proj(x, m4, gn[1], w['w_q_b'][j], tm_lin, tiles(tm_lin))
                o = _paged_attention(q.reshape(n_seq, t, d), *kv, _largest_tile(kv[2].shape[1], 8)).reshape(n, d)
            x = _out_proj(o, None, x, m4, gn[1], w['w_o_b'][j], tm_lin, tiles(tm_lin))
        x = _ffn(x, mod4(l, 2), gn[2], w['w_ffn_in'], w['w_ffn_out'], l, 1, tm_ffn, tiles(tm_ffn))
    return (x.reshape(n_seq, t, d), jnp.stack(wkv_new), jnp.stack(shift_new), k_new.reshape(n_seq, t, h, HEAD_DIM),
            v_new.reshape(n_seq, t, h, HEAD_DIM), logf.reshape(n_seq, t, h))


def kernel(x_prompt, x_sample, state_wkv, state_shift, cache_k, cache_v, cache_logf, page_table, c_prompt, c_sample,
           w_mod, b_mod, g_norm, w_ffn_in, w_ffn_out, mu_a, w_rkv_a, w_o_a, w0_a, w1_a, w2_a, a0_a, a1_a, a2_a,
           v0_a, v1_a, v2_a, g1_a, g2_a, kk_a, ka_a, rk_a, lnx_w_a, lnx_b_a, g_kv, w_kv, w_f, b_f, w_q_b, w_o_b):
    bp, _, d = x_prompt.shape
    bs = x_sample.shape[0]
    depth = w_mod.shape[0]
    n_a = state_wkv.shape[0]
    n_pool, page = cache_k.shape[:2]
    cast = lambda a: a.astype(BF16)
    w = dict(g_norm=g_norm, w_ffn_in=w_ffn_in, w_ffn_out=w_ffn_out, mu_a=mu_a, w_rkv_a=cast(w_rkv_a),
             w_o_a=cast(w_o_a), w0_a=w0_a, w1_a=cast(w1_a), w2_a=cast(w2_a), a0_a=a0_a, a1_a=cast(a1_a),
             a2_a=cast(a2_a), v0_a=v0_a, v1_a=cast(v1_a), v2_a=cast(v2_a), g1_a=cast(g1_a), g2_a=cast(g2_a),
             kk_a=kk_a, ka_a=ka_a, rk_a=rk_a.reshape(n_a, d), lnx_w_a=lnx_w_a, lnx_b_a=lnx_b_a, g_kv=g_kv,
             w_kv=cast(w_kv), w_f=cast(w_f), b_f=b_f, w_q_b=cast(w_q_b), w_o_b=cast(w_o_b))

    mods = _mods(jnp.concatenate([c_prompt, c_sample], axis=0), w_mod, b_mod)
    mods = mods.reshape(depth, bp + bs, 3, 3, d)

    n_heads = d // HEAD_DIM
    zeros_wkv = jnp.zeros((n_a, bp, n_heads, HEAD_DIM, HEAD_DIM), F32)
    zeros_shift = jnp.zeros((n_a, bp, d), F32)
    y_p, wkv_p, shift_p, k_p, v_p, logf_p = _trunk(x_prompt, mods[:, :bp], zeros_wkv, zeros_shift, None, w)
    pool_t = lambda c: jnp.transpose(c, (0, 2, 3, 1)).reshape(n_pool, d, page)
    past = (pool_t(cache_k), pool_t(cache_v), cache_logf, page_table)
    y_s, wkv_s, shift_s, k_s, v_s, logf_s = _trunk(x_sample, mods[:, bp:], state_wkv, state_shift, past, w)
    return (y_p, y_s, wkv_p, shift_p, k_p, v_p, logf_p, wkv_s, shift_s, k_s, v_s, logf_s)
```

```python
import functools

import jax
import jax.numpy as jnp
import numpy as np
from jax import lax
from jax.experimental import pallas as pl
from jax.experimental.pallas import tpu as pltpu

F32 = jnp.float32
BF16 = jnp.bfloat16

NORM_EPS = 1e-6
GN_EPS = 64e-5
MACARON_W = 0.5
NEG_INF = -1e30
HEAD_DIM = 64
LANES = 128
PAIR = LANES // HEAD_DIM
VMEM_LIMIT = 56 * 1024 * 1024

LOG2E = 1.4426950408889634
N_BIAS = 3

_NT = (((1,), (1,)), ((), ()))
_TN = (((0,), (0,)), ((), ()))


def _cparams(*sem):
    return pltpu.CompilerParams(dimension_semantics=sem, vmem_limit_bytes=VMEM_LIMIT)


def _dot(a, b):
    return jnp.dot(a.astype(BF16), b.astype(BF16), preferred_element_type=F32)


def _dot_nt(a, b):
    return lax.dot_general(a.astype(BF16), b.astype(BF16), _NT, preferred_element_type=F32)


def _dot_tn(a, b):
    return lax.dot_general(a.astype(BF16), b.astype(BF16), _TN, preferred_element_type=F32)


def _split3(x):
    hi = x.astype(BF16)
    r1 = x - hi.astype(F32)
    mid = r1.astype(BF16)
    lo = (r1 - mid.astype(F32)).astype(BF16)
    return hi, mid, lo


def _dot3(sel, x):
    hi, mid, lo = _split3(x)
    out = jnp.dot(sel, hi, preferred_element_type=F32)
    out = out + jnp.dot(sel, mid, preferred_element_type=F32)
    return out + jnp.dot(sel, lo, preferred_element_type=F32)


def _dot3_rhs(x, sel):
    hi, mid, lo = _split3(x)
    out = jnp.dot(hi, sel, preferred_element_type=F32)
    out = out + jnp.dot(mid, sel, preferred_element_type=F32)
    return out + jnp.dot(lo, sel, preferred_element_type=F32)


def _rms(x, g):
    return x * lax.rsqrt(jnp.mean(x * x, axis=-1, keepdims=True) + NORM_EPS) * g


def _sigmoid(x):
    return 1.0 / (1.0 + jnp.exp(-x))


def _silu(x):
    return x * _sigmoid(x)


def _head_ones(n):
    r = lax.broadcasted_iota(jnp.int32, (n, n), 0) // HEAD_DIM
    c = lax.broadcasted_iota(jnp.int32, (n, n), 1) // HEAD_DIM
    return jnp.where(r == c, 1.0, 0.0).astype(BF16)


def _head_sum(x, ones):
    n = ones.shape[0]
    hi = x.astype(BF16)
    lo = (x - hi.astype(F32)).astype(BF16)
    cols = []
    for i in range(0, x.shape[1], n):
        s = jnp.dot(hi[:, i:i + n], ones, preferred_element_type=F32)
        cols.append(s + jnp.dot(lo[:, i:i + n], ones, preferred_element_type=F32))
    return cols[0] if len(cols) == 1 else jnp.concatenate(cols, axis=1)


def _mods_kernel(c_ref, w_ref, b_ref, o_ref):
    o_ref[...] = _dot(_silu(c_ref[...]), w_ref[...]) + b_ref[...]


def _mods(c_all, w_mod, b_mod):
    depth, d, n9 = w_mod.shape
    nb = c_all.shape[0]
    tn = d
    return pl.pallas_call(
        _mods_kernel,
        grid=(depth, n9 // tn),
        in_specs=[
            pl.BlockSpec((nb, d), lambda l, j: (0, 0)),
            pl.BlockSpec((None, d, tn), lambda l, j: (l, 0, j)),
            pl.BlockSpec((None, 1, tn), lambda l, j: (l, 0, j)),
        ],
        out_specs=pl.BlockSpec((None, nb, tn), lambda l, j: (l, 0, j)),
        out_shape=jax.ShapeDtypeStruct((depth, nb, n9), F32),
        compiler_params=_cparams("arbitrary", "arbitrary"),
        name="mods",
    )(c_all, w_mod, b_mod.reshape(depth, 1, n9))


def _mod_spec(m4, tm, tiles_per_seq):
    _, _, rows, d = m4.shape
    if rows == 1:
        return pl.BlockSpec((None, 3, 1, d), lambda i, *_: (i // tiles_per_seq, 0, 0, 0))
    return pl.BlockSpec((None, 3, tm, d), lambda i, *_: (0, 0, i, 0))


def _ffn_kernel(x_ref, m_ref, g_ref, wg_ref, wu_ref, wo_ref, o_ref, h_scr, acc_scr):
    j = pl.program_id(1)

    @pl.when(j == 0)
    def _():
        h = _rms(x_ref[...], g_ref[0:1]) * (1.0 + m_ref[1]) + m_ref[0]
        h_scr[...] = h.astype(BF16)
        acc_scr[...] = jnp.zeros_like(acc_scr)

    h = h_scr[...]
    gate = _dot(h, wg_ref[...])
    up = _dot(h, wu_ref[...])
    acc_scr[...] += _dot(_silu(gate) * up, wo_ref[...])

    @pl.when(j == pl.num_programs(1) - 1)
    def _():
        o_ref[...] = x_ref[...] + MACARON_W * m_ref[2] * _rms(acc_scr[...], g_ref[1:2])


def _ffn(x, m4, gn, w_in, w_out, layer, sub, tm, tiles_per_seq):
    n, d = x.shape
    f = w_out.shape[2]
    tf = 256 if f % 256 == 0 else LANES
    nf = f // tf
    return pl.pallas_call(
        _ffn_kernel,
        grid=(n // tm, nf),
        in_specs=[
            pl.BlockSpec((tm, d), lambda i, j: (i, 0)),
            _mod_spec(m4, tm, tiles_per_seq),
            pl.BlockSpec((2, d), lambda i, j: (0, 0)),
            pl.BlockSpec((None, None, d, tf), lambda i, j: (layer, sub, 0, j)),
            pl.BlockSpec((None, None, d, tf), lambda i, j: (layer, sub, 0, j + nf)),
            pl.BlockSpec((None, None, tf, d), lambda i, j: (layer, sub, j, 0)),
        ],
        out_specs=pl.BlockSpec((tm, d), lambda i, j: (i, 0)),
        out_shape=jax.ShapeDtypeStruct((n, d), F32),
        scratch_shapes=[pltpu.VMEM((tm, d), BF16), pltpu.VMEM((tm, d), F32)],
        compiler_params=_cparams("parallel", "arbitrary"),
        name="ffn",
    )(x, m4, gn, w_in, w_in, w_out)


def _rwkv_proj_kernel(has_vfirst, seq_len, x_ref, m_ref, g_ref, hl_ref, mu_ref, wrkv_ref, w0_ref, w1_ref,
                      w2_ref, a0_ref, a1_ref, a2_ref, g1_ref, g2_ref, kka_ref, kaa_ref, *rest):
    if has_vfirst:
        v0_ref, v1_ref, v2_ref, vf_ref = rest[:4]
        rest = rest[4:]
    r_ref, lw_ref, k_ref, v_ref, kk_ref, a_ref, gt_ref, h_ref, carry_scr = rest
    tm, d = x_ref.shape
    i = pl.program_id(0)

    h = _rms(x_ref[...], g_ref[0:1]) * (1.0 + m_ref[1]) + m_ref[0]
    h_ref[...] = h
    row = lax.broadcasted_iota(jnp.int32, (tm, d), 0)
    rolled = pltpu.roll(h, 1, 0)
    if hl_ref.shape[0] == 1:
        tiles_per_seq = seq_len // tm
        first = jnp.where(i % tiles_per_seq == 0, hl_ref[...], carry_scr[...])
        h_prev = jnp.where(row == 0, first, rolled)
        carry_scr[...] = h[tm - 1:tm, :]
    else:
        h_prev = jnp.where(row % seq_len == 0, hl_ref[...], rolled)
    xx = h_prev - h
    xr, xw, xk, xv, xa, xg = [h + xx * mu_ref[j:j + 1] for j in range(6)]

    r = _dot(xr, wrkv_ref[0])
    k = _dot(xk, wrkv_ref[1])
    v = _dot(xv, wrkv_ref[2])
    z = w0_ref[...] + _dot(jnp.tanh(_dot(xw, w1_ref[...])), w2_ref[...])
    lw = -_sigmoid(z) * jnp.exp(F32(-0.5))
    a = _sigmoid(a0_ref[...] + _dot(_dot(xa, a1_ref[...]), a2_ref[...]))
    gt = _dot(_sigmoid(_dot(xg, g1_ref[...])), g2_ref[...])
    if has_vfirst:
        mix = _sigmoid(v0_ref[...] + _dot(_dot(xv, v1_ref[...]), v2_ref[...]))
        v = v + (vf_ref[...] - v) * mix
    kk = k * kka_ref[...]
    ones = _head_ones(min(d, 2 * LANES))
    kk = kk * lax.rsqrt(jnp.maximum(_head_sum(kk * kk, ones), 1e-24))
    k = k * (1.0 + (a - 1.0) * kaa_ref[...])

    r_ref[...] = r
    lw_ref[...] = lw
    k_ref[...] = k
    v_ref[...] = v
    kk_ref[...] = kk
    a_ref[...] = a
    gt_ref[...] = gt


def _rwkv_proj(x, m4, gn, hl, seq_len, tm, p, v_first):
    n, d = x.shape
    tiles_per_seq = max(seq_len // tm, 1)
    has_vfirst = v_first is not None
    row = lambda a: a.reshape(1, d)
    const = lambda a: pl.BlockSpec(a.shape, lambda i: (0,) * a.ndim)
    tok = pl.BlockSpec((tm, d), lambda i: (i, 0))
    if hl.ndim == 3:
        hl_spec = pl.BlockSpec((None, 1, d), lambda i: (i // tiles_per_seq, 0, 0))
    else:
        hl_spec = tok
    args = [x, m4, gn, hl, p['mu'], p['w_rkv'], row(p['w0']), p['w1'], p['w2'], row(p['a0']), p['a1'], p['a2'],
            p['g1'], p['g2'], row(p['kk']), row(p['ka'])]
    specs = [tok, _mod_spec(m4, tm, tiles_per_seq), const(gn), hl_spec] + [const(a) for a in args[4:]]
    if has_vfirst:
        extra = [row(p['v0']), p['v1'], p['v2']]
        args += extra + [v_first]
        specs += [const(a) for a in extra] + [tok]
    return pl.pallas_call(
        functools.partial(_rwkv_proj_kernel, has_vfirst, seq_len),
        grid=(n // tm,),
        in_specs=specs,
        out_specs=[tok] * 8,
        out_shape=[jax.ShapeDtypeStruct((n, d), F32)] * 8,
        scratch_shapes=[pltpu.VMEM((1, d), F32)],
        compiler_params=_cparams("arbitrary"),
        name="rwkv_proj",
    )(*args)


LONG_CHUNK = 64


def _long_masks():
    n2 = PAIR * LONG_CHUNK
    r = np.arange(n2)[:, None]
    c = np.arange(n2)[None, :]
    masks = [c < r, c <= r, c == r]
    b = 1
    while b < LONG_CHUNK:
        masks.append((r // (2 * b) == c // (2 * b)) & ((r // b) % 2 == 1) & ((c // b) % 2 == 0))
        b *= 2
    return jnp.asarray(np.stack(masks).astype(np.float32))


def _wkv_long_kernel(r_ref, lw_ref, k_ref, v_ref, kk_ref, a_ref, s0_ref, lnw_ref, lnb_ref, rk_ref, mask_ref,
                     o_ref, sout_ref, st_scr, u2_scr, o2_scr, wq_scr, bk_scr, lkk_scr, v2_scr, kt_scr, bt_scr,
                     lrb_scr, lvt_scr, x_scr, gc_scr, lab_scr, tinv_scr, ol_scr, ds_scr, ut_scr):
    c = LONG_CHUNK
    n2 = PAIR * c
    tb, d = r_ref.shape
    n_pairs = d // LANES
    t_idx = pl.program_id(1)

    @pl.when(t_idx == 0)
    def _():
        zero = jnp.zeros((HEAD_DIM, HEAD_DIM), F32)
        for p in range(n_pairs):
            top = jnp.concatenate([s0_ref[PAIR * p], zero], axis=1)
            bottom = jnp.concatenate([zero, s0_ref[PAIR * p + 1]], axis=1)
            st_scr[p] = jnp.concatenate([top, bottom], axis=0).T

    rows_in = min(tb, c)
    n_chunks = max(tb // c, 1)
    items = [(ic, p) for ic in range(n_chunks) for p in range(n_pairs)]
    dotf = lambda x, y: jnp.dot(x, y, preferred_element_type=F32)
    lane = lax.broadcasted_iota(jnp.int32, (c, LANES), 1)
    head_lane = [lane // HEAD_DIM == h for h in range(PAIR)]
    tri = jnp.where(lax.broadcasted_iota(jnp.int32, (c, c), 1) <= lax.broadcasted_iota(jnp.int32, (c, c), 0),
                    1.0, 0.0).astype(BF16)

    def stack(x):
        return jnp.concatenate([jnp.where(m, x, 0.0) for m in head_lane], axis=0)

    for i, (ic, p) in enumerate(items):
        rows, ls = slice(ic * c, ic * c + rows_in), slice(p * LANES, (p + 1) * LANES)
        rc, lwc, kc, vc, kkc, ac = [ref[rows, ls] for ref in (r_ref, lw_ref, k_ref, v_ref, kk_ref, a_ref)]
        if rows_in < c:
            pad = jnp.zeros((c - rows_in, LANES), F32)
            rc, lwc, kc, vc, kkc, ac = [jnp.concatenate([x, pad], axis=0) for x in (rc, lwc, kc, vc, kkc, ac)]
        cum = _dot3(tri, lwc)
        g = jnp.exp(cum)
        gi = jnp.exp(-cum)
        g_end = g[c - 1:c, :]
        bb = ac * kkc * gi
        kb = kc * gi
        wq_scr[i, :n2] = stack(-kkc * jnp.exp(cum - lwc)).astype(BF16)
        wq_scr[i, n2:] = stack(rc * g).astype(BF16)
        bk_scr[i, :n2] = stack(bb).astype(BF16)
        bk_scr[i, n2:] = stack(kb).astype(BF16)
        v2_scr[i] = stack(vc).astype(BF16)
        kt_scr[i] = stack(kb * g_end).astype(BF16)
        bt_scr[i] = stack(bb * g_end).T.astype(BF16)
        gc_scr[i] = jnp.broadcast_to(g_end, (LANES, LANES)).T

    for i in range(len(items)):
        scores = lax.dot_general(wq_scr[i], bk_scr[i], _NT, preferred_element_type=F32)
        l_ab = scores[:n2, :n2] * mask_ref[0]
        lab_scr[i] = l_ab
        tinv_scr[i] = mask_ref[2] + l_ab * mask_ref[3]
        lkk_scr[i, :n2] = (scores[:n2, n2:] * mask_ref[0]).astype(BF16)
        lkk_scr[i, n2:] = (scores[n2:, n2:] * mask_ref[1]).astype(BF16)
        lrb_scr[i] = (scores[n2:, :n2] * mask_ref[1]).astype(BF16)

    for i in range(len(items)):
        lv = dotf(lkk_scr[i], v2_scr[i])
        lvt_scr[i] = lv[:n2].astype(BF16)
        ol_scr[i] = lv[n2:]
        ds_scr[i] = lax.dot_general(kt_scr[i], v2_scr[i], _TN, preferred_element_type=F32)

    for level in range(4, mask_ref.shape[0]):
        for i in range(len(items)):
            x_scr[i] = dotf((lab_scr[i] * mask_ref[level]).astype(BF16), tinv_scr[i].astype(BF16)).astype(BF16)
        for i in range(len(items)):
            tinv = tinv_scr[i]
            tinv_scr[i] = tinv + dotf(tinv.astype(BF16), x_scr[i])

    for i in range(len(items)):
        tinv = tinv_scr[i].astype(BF16)
        ut_scr[i] = dotf(tinv, lvt_scr[i])
        wq_scr[i, :n2] = dotf(tinv, wq_scr[i, :n2]).astype(BF16)

    def advance(ic, carry):
        rows = pl.ds(pl.multiple_of(ic * c, c), rows_in)
        for p in range(n_pairs):
            i = ic * n_pairs + p
            ws = dotf(wq_scr[i], st_scr[p].astype(BF16))
            u2_scr[p] = (ws[:n2] + ut_scr[i]).astype(BF16)
            o2_scr[p] = ws[n2:] + ol_scr[i]
        for p in range(n_pairs):
            i = ic * n_pairs + p
            u2 = u2_scr[p]
            o2 = o2_scr[p] + dotf(lrb_scr[i], u2)
            st_scr[p] = gc_scr[i] * st_scr[p] + dotf(bt_scr[i], u2) + ds_scr[i]
            o = o2[0:c]
            for h in range(1, PAIR):
                o = o + o2[h * c:(h + 1) * c]
            o_ref[rows, p * LANES:(p + 1) * LANES] = o[:rows_in]
        return carry

    lax.fori_loop(0, n_chunks, advance, 0)

    ones = _head_ones(min(d, 2 * LANES))
    o = o_ref[...]
    cen = o - _head_sum(o, ones) * (1.0 / HEAD_DIM)
    var = _head_sum(cen * cen, ones) * (1.0 / HEAD_DIM)
    bonus = _head_sum(r_ref[...] * k_ref[...] * rk_ref[...], ones) * v_ref[...]
    o_ref[...] = cen * lax.rsqrt(var + GN_EPS) * lnw_ref[...] + lnb_ref[...] + bonus

    @pl.when(t_idx == pl.num_programs(1) - 1)
    def _():
        for p in range(n_pairs):
            s = st_scr[p].T
            sout_ref[PAIR * p] = s[:HEAD_DIM, :HEAD_DIM]
            sout_ref[PAIR * p + 1] = s[HEAD_DIM:, HEAD_DIM:]


def _wkv_long(r, lw, k, v, kk, a, s0, lnw, lnb, rk, seq_len, tb):
    assert PAIR == 2
    n, d = r.shape
    n_seq = n // seq_len
    n_pairs = d // LANES
    nt = seq_len // tb
    n_items = max(tb // LONG_CHUNK, 1) * n_pairs
    n2 = PAIR * LONG_CHUNK
    assert n2 == LANES
    masks = _long_masks()
    tok = pl.BlockSpec((tb, d), lambda b, t: (b * nt + t, 0))
    vec = pl.BlockSpec((1, d), lambda b, t: (0, 0))
    st = pl.BlockSpec((None, PAIR * n_pairs, HEAD_DIM, HEAD_DIM), lambda b, t: (b, 0, 0, 0))
    per = lambda rows, dt: pltpu.VMEM((n_items, rows, LANES), dt)
    return pl.pallas_call(
        _wkv_long_kernel,
        grid=(n_seq, nt),
        in_specs=[tok] * 6 + [st, vec, vec, vec, pl.BlockSpec(masks.shape, lambda b, t: (0, 0, 0))],
        out_specs=[tok, st],
        out_shape=[jax.ShapeDtypeStruct((n, d), F32), jax.ShapeDtypeStruct(s0.shape, F32)],
        scratch_shapes=[pltpu.VMEM((n_pairs, LANES, LANES), F32), pltpu.VMEM((n_pairs, n2, LANES), BF16),
                        pltpu.VMEM((n_pairs, n2, LANES), F32)]
        + [per(2 * n2, BF16)] * 3 + [per(n2, BF16)] * 6 + [per(n2, F32)] * 6,
        compiler_params=_cparams("parallel", "arbitrary"),
        name="wkv_long",
    )(r, lw, k, v, kk, a, s0, lnw.reshape(1, d), lnb.reshape(1, d), rk.reshape(1, d), masks)


def _out_kernel(has_gate, res_w, *refs):
    if has_gate:
        a_ref, gt_ref, x_ref, m_ref, g_ref, w_ref, o_ref = refs
        a = a_ref[...] * gt_ref[...]
    else:
        a_ref, x_ref, m_ref, g_ref, w_ref, o_ref = refs
        a = a_ref[...]
    y = _dot(a, w_ref[...])
    o_ref[...] = x_ref[...] + res_w * m_ref[2] * _rms(y, g_ref[1:2])


def _out_proj(a, gate, x, m4, gn, w, tm, tiles_per_seq, res_w=1.0):
    n, d = x.shape
    tok = pl.BlockSpec((tm, d), lambda i: (i, 0))
    args = [a] + ([gate] if gate is not None else []) + [x, m4, gn, w]
    specs = [tok] * (len(args) - 3) + [_mod_spec(m4, tm, tiles_per_seq), pl.BlockSpec((2, d), lambda i: (0, 0)),
                                       pl.BlockSpec(w.shape, lambda i: (0, 0))]
    return pl.pallas_call(
        functools.partial(_out_kernel, gate is not None, res_w),
        grid=(n // tm,),
        in_specs=specs,
        out_specs=tok,
        out_shape=jax.ShapeDtypeStruct((n, d), F32),
        compiler_params=_cparams("parallel"),
        name="out_proj",
    )(*args)


def _kv_kernel(x_ref, g_ref, wkv_ref, wf_ref, bf_ref, k_ref, v_ref, kb_ref, vb_ref, lf_ref):
    d = x_ref.shape[1]
    s = _rms(x_ref[...], g_ref[...])
    kv = _dot(s, wkv_ref[...])
    k = kv[:, :d]
    v = kv[:, d:]
    k_ref[...] = k
    v_ref[...] = v
    kb_ref[...] = k.astype(BF16)
    vb_ref[...] = v.astype(BF16)
    z = _dot(s, wf_ref[...]) + bf_ref[...]
    lf_ref[...] = jnp.minimum(z, 0.0) - jnp.log(1.0 + jnp.exp(-jnp.abs(z)))


def _shared_kv(x, g_kv, w_kv, w_f, b_f, tm):
    n, d = x.shape
    h = w_f.shape[1]
    tok = pl.BlockSpec((tm, d), lambda i: (i, 0))
    return pl.pallas_call(
        _kv_kernel,
        grid=(n // tm,),
        in_specs=[tok, pl.BlockSpec((1, d), lambda i: (0, 0)), pl.BlockSpec(w_kv.shape, lambda i: (0, 0)),
                  pl.BlockSpec(w_f.shape, lambda i: (0, 0)), pl.BlockSpec((1, h), lambda i: (0, 0))],
        out_specs=[tok, tok, tok, tok, pl.BlockSpec((tm, h), lambda i: (i, 0))],
        out_shape=[jax.ShapeDtypeStruct((n, d), F32)] * 2 + [jax.ShapeDtypeStruct((n, d), BF16)] * 2
        + [jax.ShapeDtypeStruct((n, h), F32)],
        compiler_params=_cparams("parallel"),
        name="shared_kv",
    )(x, g_kv.reshape(1, d), w_kv, w_f, b_f.reshape(1, h))


def _cumsum_kernel(n_grp, pt_ref, *refs):
    lf_refs = refs[:n_grp]
    ck_ref, parts_ref, carry_scr = refs[n_grp:]
    del pt_ref
    j = pl.program_id(1)
    page, h = lf_refs[0].shape

    @pl.when(j == 0)
    def _():
        carry_scr[...] = jnp.zeros_like(carry_scr)

    ri = lax.broadcasted_iota(jnp.int32, (page, page), 0)
    ci = lax.broadcasted_iota(jnp.int32, (page, page), 1)
    tri = jnp.where(ci <= ri, 1.0, 0.0).astype(BF16)
    local = [_dot3(tri, lf[...]) for lf in lf_refs]
    base = carry_scr[...]
    for i in range(n_grp):
        ck = base + local[i]
        ck_ref[i * page:(i + 1) * page, :] = ck
        base = ck[page - 1:page, :]

        src_h = lax.broadcasted_iota(jnp.int32, (h, LANES), 0)
        dst = lax.broadcasted_iota(jnp.int32, (h, LANES), 1)
        out = jnp.where(lax.broadcasted_iota(jnp.int32, (page, LANES), 1) // h == N_BIAS, 1.0, 0.0)
        for x, part in enumerate(_split3(ck * LOG2E)):
            out = out + jnp.dot(part, jnp.where(dst == x * h + src_h, 1.0, 0.0).astype(BF16),
                                preferred_element_type=F32)
        parts_ref[i * page:(i + 1) * page, :] = out.astype(BF16)
    carry_scr[...] = base


def _cumsum_pages(page_table, lf_pages):
    n_seq, n_pages = page_table.shape
    _, page, h = lf_pages.shape
    n_grp = _largest_tile(n_pages, 8)

    def page_spec(i):
        return pl.BlockSpec((None, page, h), lambda b, j, pt: (pt[b, j * n_grp + i], 0, 0))

    return pl.pallas_call(
        functools.partial(_cumsum_kernel, n_grp),
        grid_spec=pltpu.PrefetchScalarGridSpec(
            num_scalar_prefetch=1, grid=(n_seq, n_pages // n_grp),
            in_specs=[page_spec(i) for i in range(n_grp)],
            out_specs=[pl.BlockSpec((None, n_grp * page, h), lambda b, j, pt: (b, j, 0)),
                       pl.BlockSpec((None, n_grp * page, LANES), lambda b, j, pt: (b, j, 0))],
            scratch_shapes=[pltpu.VMEM((1, h), F32)]),
        out_shape=[jax.ShapeDtypeStruct((n_seq, n_pages * page, h), F32),
                   jax.ShapeDtypeStruct((n_seq, n_pages * page, LANES), BF16)],
        compiler_params=_cparams("parallel", "arbitrary"),
        name="cumsum",
    )(page_table, *([lf_pages] * n_grp))


def _cumsum_pool_kernel(n_grp, pt_ref, *refs):
    lf_refs = refs[:n_grp]
    new_ref, ck_ref, cknew_ref, carry_scr = refs[n_grp:]
    del pt_ref
    j = pl.program_id(1)
    h, page = lf_refs[0].shape

    @pl.when(j == 0)
    def _():
        carry_scr[...] = jnp.zeros_like(carry_scr)

    ri = lax.broadcasted_iota(jnp.int32, (page, page), 0)
    ci = lax.broadcasted_iota(jnp.int32, (page, page), 1)
    upper = jnp.where(ri <= ci, 1.0, 0.0).astype(BF16)
    local = [_dot3_rhs(lf[...], upper) for lf in lf_refs]
    base = carry_scr[...]
    for i in range(n_grp):
        ck = base + local[i]
        ck_ref[:, i * page:(i + 1) * page] = ck
        base = ck[:, page - 1:page]
    carry_scr[...] = base

    @pl.when(j == pl.num_programs(1) - 1)
    def _():
        t = new_ref.shape[1]
        cknew_ref[...] = base + _dot3_rhs(new_ref[...], upper[:t, :t])


def _cumsum_pool(page_table, lf_pool_t, lf_new_t):
    n_seq, n_pages = page_table.shape
    _, h, page = lf_pool_t.shape
    t = lf_new_t.shape[2]
    n_grp = _largest_tile(n_pages, 8)

    def page_spec(i):
        return pl.BlockSpec((None, h, page), lambda b, j, pt: (pt[b, j * n_grp + i], 0, 0))

    new_spec = pl.BlockSpec((None, h, t), lambda b, j, pt: (b, 0, 0))
    return pl.pallas_call(
        functools.partial(_cumsum_pool_kernel, n_grp),
        grid_spec=pltpu.PrefetchScalarGridSpec(
            num_scalar_prefetch=1, grid=(n_seq, n_pages // n_grp),
            in_specs=[page_spec(i) for i in range(n_grp)] + [new_spec],
            out_specs=[pl.BlockSpec((None, h, n_grp * page), lambda b, j, pt: (b, 0, j)), new_spec],
            scratch_shapes=[pltpu.VMEM((h, 1), F32)]),
        out_shape=[jax.ShapeDtypeStruct((n_seq, h, n_pages * page), F32), jax.ShapeDtypeStruct((n_seq, h, t), F32)],
        compiler_params=_cparams("parallel", "arbitrary"),
        name="cumsum_pool",
    )(page_table, *([lf_pool_t] * n_grp), lf_new_t)


def _q_kernel(x_ref, m_ref, g_ref, w_ref, q_ref):
    h = _rms(x_ref[...], g_ref[0:1]) * (1.0 + m_ref[1]) + m_ref[0]
    q_ref[...] = (_dot(h, w_ref[...]) * (HEAD_DIM ** -0.5)).astype(BF16)


def _q_proj(x, m4, gn, w, tm, tiles_per_seq):
    n, d = x.shape
    tok = pl.BlockSpec((tm, d), lambda i: (i, 0))
    return pl.pallas_call(
        _q_kernel,
        grid=(n // tm,),
        in_specs=[tok, _mod_spec(m4, tm, tiles_per_seq), pl.BlockSpec((2, d), lambda i: (0, 0)),
                  pl.BlockSpec(w.shape, lambda i: (0, 0))],
        out_specs=tok,
        out_shape=jax.ShapeDtypeStruct((n, d), BF16),
        compiler_params=_cparams("parallel"),
        name="q_proj",
    )(x, m4, gn, w)


def _pair_to_tiles():
    m = np.zeros((LANES, PAIR * LANES), np.float32)
    r = np.arange(LANES)
    m[r, (r // HEAD_DIM) * LANES + r % HEAD_DIM] = 1.0
    return jnp.asarray(m, BF16)


def _bias_place(h, for_keys):
    m = np.zeros((LANES, h * LANES), np.float32)
    for hh in range(h):
        for x in range(N_BIAS):
            if for_keys:
                m[N_BIAS * h + hh, hh * LANES + HEAD_DIM + x] = 1.0
                m[x * h + hh, hh * LANES + HEAD_DIM + N_BIAS + x] = -1.0
            else:
                m[x * h + hh, hh * LANES + HEAD_DIM + x] = 1.0
                m[N_BIAS * h + hh, hh * LANES + HEAD_DIM + N_BIAS + x] = 1.0
    return jnp.asarray(m, BF16)


def _to_head_tiles(x, place, extra):
    cols = []
    for p in range(x.shape[1] // LANES):
        sl = slice(p * PAIR * LANES, (p + 1) * PAIR * LANES)
        cols.append(jnp.dot(x[:, p * LANES:(p + 1) * LANES], place, preferred_element_type=F32) + extra[:, sl])
    return jnp.concatenate(cols, axis=1)


def _q_aug_kernel(x_ref, m_ref, g_ref, w_ref, parts_ref, place_ref, bias_ref, q_ref):
    h = _rms(x_ref[...], g_ref[0:1]) * (1.0 + m_ref[1]) + m_ref[0]
    q = (_dot(h, w_ref[...]) * (HEAD_DIM ** -0.5 * LOG2E)).astype(BF16)
    bias = jnp.dot(parts_ref[...], bias_ref[...], preferred_element_type=F32)
    q_ref[...] = _to_head_tiles(q, place_ref[...], bias).astype(BF16)


def _q_aug(x, m4, gn, w, parts, tm, tiles_per_seq):
    n, d = x.shape
    tok = pl.BlockSpec((tm, d), lambda i: (i, 0))
    place, bias = _pair_to_tiles(), _bias_place(d // HEAD_DIM, False)
    const = lambda a: pl.BlockSpec(a.shape, lambda i: (0, 0))
    return pl.pallas_call(
        _q_aug_kernel,
        grid=(n // tm,),
        in_specs=[tok, _mod_spec(m4, tm, tiles_per_seq), pl.BlockSpec((2, d), lambda i: (0, 0)), const(w),
                  pl.BlockSpec((tm, LANES), lambda i: (i, 0)), const(place), const(bias)],
        out_specs=pl.BlockSpec((tm, PAIR * d), lambda i: (i, 0)),
        out_shape=jax.ShapeDtypeStruct((n, PAIR * d), BF16),
        compiler_params=_cparams("parallel"),
        name="q_aug",
    )(x, m4, gn, w, parts, place, bias)


def _kv_aug_kernel(k_ref, v_ref, parts_ref, place_ref, bias_ref, ka_ref, va_ref):
    tm, d2 = ka_ref.shape
    bias = jnp.dot(parts_ref[...], bias_ref[...], preferred_element_type=F32)
    ka_ref[...] = _to_head_tiles(k_ref[...], place_ref[...], bias).astype(BF16)
    ones = jnp.where(lax.broadcasted_iota(jnp.int32, (1, d2), 1) % LANES >= HEAD_DIM, 1.0, 0.0)
    va_ref[...] = _to_head_tiles(v_ref[...], place_ref[...], ones).astype(BF16)


def _kv_aug(kb, vb, parts, tm):
    n, d = kb.shape
    tok = pl.BlockSpec((tm, d), lambda i: (i, 0))
    wide = pl.BlockSpec((tm, PAIR * d), lambda i: (i, 0))
    place, bias = _pair_to_tiles(), _bias_place(d // HEAD_DIM, True)
    const = lambda a: pl.BlockSpec(a.shape, lambda i: (0, 0))
    return pl.pallas_call(
        _kv_aug_kernel,
        grid=(n // tm,),
        in_specs=[tok, tok, pl.BlockSpec((tm, LANES), lambda i: (i, 0)), const(place), const(bias)],
        out_specs=[wide, wide],
        out_shape=[jax.ShapeDtypeStruct((n, PAIR * d), BF16)] * 2,
        compiler_params=_cparams("parallel"),
        name="kv_aug",
    )(kb, vb, parts, place, bias)


def _flash_kernel(qi_ref, kj_ref, q_ref, k_ref, v_ref, o_ref, m_scr, acc_scr):
    n = pl.program_id(2)
    qi = qi_ref[n]
    kj = kj_ref[n]
    tq = q_ref.shape[0]
    tk = k_ref.shape[0]
    n_heads = q_ref.shape[1] // LANES

    @pl.when(kj == 0)
    def _():
        m_scr[...] = jnp.full_like(m_scr, NEG_INF)
        acc_scr[...] = jnp.zeros_like(acc_scr)

    def update(masked):
        if masked:
            valid = (lax.broadcasted_iota(jnp.int32, (tq, tk), 1)
                     <= lax.broadcasted_iota(jnp.int32, (tq, tk), 0))
        for h in range(n_heads):
            sl = slice(h * LANES, (h + 1) * LANES)
            s = lax.dot_general(q_ref[:, sl], k_ref[:, sl], _NT, preferred_element_type=F32)
            if masked:
                s = jnp.where(valid, s, NEG_INF)
            cols = [s[:, c * LANES:(c + 1) * LANES] for c in range(tk // LANES)]
            m_cur = cols[0]
            for col in cols[1:]:
                m_cur = jnp.maximum(m_cur, col)
            m_old = m_scr[h]
            m_new = jnp.maximum(m_old, jnp.max(m_cur, axis=1, keepdims=True))
            p = jnp.concatenate([jnp.exp2(col - m_new) for col in cols], axis=1).astype(BF16)
            acc_scr[h] = jnp.exp2(m_old - m_new) * acc_scr[h] + jnp.dot(p, v_ref[:, sl],
                                                                        preferred_element_type=F32)
            m_scr[h] = m_new

    @pl.when(kj < qi)
    def _():
        update(False)

    @pl.when(kj == qi)
    def _():
        update(True)
        lane = lax.broadcasted_iota(jnp.int32, (tq, LANES), 1)
        outs = []
        for h in range(n_heads):
            acc = acc_scr[h]
            outs.append(acc / pltpu.roll(acc, HEAD_DIM, 1))
        for p in range(n_heads // PAIR):
            o_ref[:, p * LANES:(p + 1) * LANES] = jnp.where(
                lane < HEAD_DIM, outs[PAIR * p], pltpu.roll(outs[PAIR * p + 1], HEAD_DIM, 1))


def _flash_prompt(qa, ka, va, n_seq, tq, heads_per_step):
    n, d2 = qa.shape
    t = n // n_seq
    nq = t // tq
    width = heads_per_step * LANES
    qi = jnp.asarray([i for i in range(nq) for _ in range(i + 1)], jnp.int32)
    kj = jnp.asarray([j for i in range(nq) for j in range(i + 1)], jnp.int32)
    return pl.pallas_call(
        _flash_kernel,
        grid_spec=pltpu.PrefetchScalarGridSpec(
            num_scalar_prefetch=2,
            grid=(n_seq, d2 // width, int(qi.shape[0])),
            in_specs=[
                pl.BlockSpec((tq, width), lambda b, p, s, qi, kj: (b * nq + qi[s], p)),
                pl.BlockSpec((tq, width), lambda b, p, s, qi, kj: (b * nq + kj[s], p)),
                pl.BlockSpec((tq, width), lambda b, p, s, qi, kj: (b * nq + kj[s], p)),
            ],
            out_specs=pl.BlockSpec((tq, width // PAIR), lambda b, p, s, qi, kj: (b * nq + qi[s], p)),
            scratch_shapes=[pltpu.VMEM((heads_per_step, tq, LANES), F32),
                            pltpu.VMEM((heads_per_step, tq, LANES), F32)]),
        out_shape=jax.ShapeDtypeStruct((n, d2 // PAIR), F32),
        compiler_params=_cparams("parallel", "parallel", "arbitrary"),
        name="fox_prompt",
    )(qi, kj, qa, ka, va)


def _paged_kernel(n_grp, pt_ref, q_ref, cq_ref, ckp_ref, ckn_ref, kn_ref, vn_ref, *rest):
    t, d = q_ref.shape
    n_heads = d // HEAD_DIM
    rows = n_heads * t
    k_refs = rest[:n_grp]
    v_refs = rest[n_grp:2 * n_grp]
    o_ref, qbd_scr, m_scr, l_scr, acc_scr = rest[2 * n_grp:]
    page = k_refs[0].shape[1]
    del pt_ref
    g = pl.program_id(1)

    row_head = lax.broadcasted_iota(jnp.int32, (rows, d), 0) // t
    lane_head = lax.broadcasted_iota(jnp.int32, (rows, d), 1) // HEAD_DIM
    expand = jnp.where(lax.broadcasted_iota(jnp.int32, (rows, n_heads), 0) // t
                       == lax.broadcasted_iota(jnp.int32, (rows, n_heads), 1), 1.0, 0.0).astype(BF16)

    @pl.when(g == 0)
    def _():
        q = q_ref[...].astype(F32)
        q_rows = jnp.concatenate([q] * n_heads, axis=0)
        qbd_scr[...] = jnp.where(row_head == lane_head, q_rows, 0.0).astype(BF16)
        m_scr[...] = jnp.full_like(m_scr, NEG_INF)
        l_scr[...] = jnp.zeros_like(l_scr)
        acc_scr[...] = jnp.zeros_like(acc_scr)

    qbd = qbd_scr[...]
    cq = cq_ref[...]

    def attend(s, weighted_values):
        m_old = m_scr[...]
        m_new = jnp.maximum(m_old, jnp.max(s, axis=1, keepdims=True))
        alpha = jnp.exp(m_old - m_new)
        p = jnp.exp(s - m_new)
        l_scr[...] = alpha * l_scr[...] + jnp.sum(p, axis=1, keepdims=True)
        acc_scr[...] = alpha * acc_scr[...] + weighted_values(p)
        m_scr[...] = m_new

    kts = [kr[...].astype(BF16) for kr in k_refs]
    vts = [vr[...].astype(BF16) for vr in v_refs]

    def past_values(p):
        pv = _dot_nt(p[:, 0:page], vts[0])
        for i in range(1, n_grp):
            pv = pv + _dot_nt(p[:, i * page:(i + 1) * page], vts[i])
        return pv

    s = jnp.concatenate([_dot(qbd, kt) for kt in kts], axis=1)
    s = s + (cq - _dot3(expand, ckp_ref[...]))
    attend(s, past_values)

    @pl.when(g == pl.num_programs(1) - 1)
    def _():
        pad = jnp.zeros((page - t, d), F32)
        k_new = jnp.concatenate([kn_ref[...], pad], axis=0)
        v_new = jnp.concatenate([vn_ref[...], pad], axis=0)
        s_new = _dot_nt(qbd, k_new) + (cq - _dot3(expand, ckn_ref[...]))
        key = lax.broadcasted_iota(jnp.int32, (rows, page), 1)
        query = lax.broadcasted_iota(jnp.int32, (rows, page), 0) % t
        attend(jnp.where((key <= query) & (key < t), s_new, NEG_INF), lambda p: _dot(p, v_new))
        out = acc_scr[...] / l_scr[...]
        out = jnp.where(row_head == lane_head, out, 0.0).reshape(n_heads, t, d)
        o_ref[...] = jnp.sum(out, axis=0)


def _paged_attention(q, pool_kt, pool_vt, page_table, ckp_rows, ck_new, k_new, v_new, n_grp):
    n_seq, t, d = q.shape
    n_pages = page_table.shape[1]
    page = pool_kt.shape[2]
    h = d // HEAD_DIM
    rows = h * t
    assert n_pages % n_grp == 0
    cq_col = ck_new.reshape(n_seq, rows, 1)
    ckn_rows = jnp.pad(ck_new, ((0, 0), (0, 0), (0, page - t)))
    per_seq = lambda shape: pl.BlockSpec((None,) + shape, lambda b, g, pt: (b, 0, 0))

    def page_spec(i):
        return pl.BlockSpec((None, d, page), lambda b, g, pt: (pt[b, g * n_grp + i], 0, 0))

    in_specs = [per_seq((t, d)), per_seq((rows, 1)),
                pl.BlockSpec((None, h, n_grp * page), lambda b, g, pt: (b, 0, g)),
                per_seq((h, page)), per_seq((t, d)), per_seq((t, d))]
    in_specs += [page_spec(i) for i in range(n_grp)] * 2
    return pl.pallas_call(
        functools.partial(_paged_kernel, n_grp),
        grid_spec=pltpu.PrefetchScalarGridSpec(
            num_scalar_prefetch=1, grid=(n_seq, n_pages // n_grp), in_specs=in_specs,
            out_specs=per_seq((t, d)),
            scratch_shapes=[pltpu.VMEM((rows, d), BF16), pltpu.VMEM((rows, 1), F32), pltpu.VMEM((rows, 1), F32),
                            pltpu.VMEM((rows, d), F32)]),
        out_shape=jax.ShapeDtypeStruct((n_seq, t, d), F32),
        compiler_params=_cparams("parallel", "arbitrary"),
        name="fox_sample",
    )(page_table, q, cq_col, ckp_rows, ckn_rows, k_new, v_new, *([pool_kt] * n_grp), *([pool_vt] * n_grp))


def _largest_tile(n, cap):
    t = min(n, cap)
    while n % t:
        t //= 2
    return t


def _trunk(x3, mods, wkv0, shift0, past, w):
    n_seq, t, d = x3.shape
    n = n_seq * t
    depth = mods.shape[0]
    n_a = wkv0.shape[0]
    h = d // HEAD_DIM
    x = x3.reshape(n, d)
    per_seq = t >= 256
    if per_seq:
        tm_ffn, tm_mix, tm_lin = _largest_tile(t, 1024), _largest_tile(t, 256), _largest_tile(t, 512)
        tb = _largest_tile(t, 4 * LONG_CHUNK)
    else:
        tm_ffn = tm_mix = tm_lin = n
        tb = t

    def mod4(l, sub):
        m = mods[l, :, sub]
        if per_seq:
            return m.reshape(n_seq, 3, 1, d)
        return jnp.repeat(m, t, axis=0).transpose(1, 0, 2).reshape(1, 3, n, d)

    def tiles(tm):
        return max(t // tm, 1)

    v_first = None
    wkv_new, shift_new = [], []
    kv = None
    for l in range(depth):
        if l == n_a:
            k_new, v_new, kb, vb, logf = _shared_kv(x, w['g_kv'], w['w_kv'], w['w_f'], w['b_f'], tm_lin)
            if past is None:
                ident = jnp.arange(n // LANES, dtype=jnp.int32).reshape(n_seq, t // LANES)
                _, parts = _cumsum_pages(ident, logf.reshape(n // LANES, LANES, h))
                parts = parts.reshape(n, LANES)
                kv = list(_kv_aug(kb, vb, parts, tm_lin)) + [parts]
            else:
                cache_k, cache_v, cache_logf, page_table = past
                ck_past, ck_new = _cumsum_pool(page_table, cache_logf,
                                               logf.reshape(n_seq, t, h).transpose(0, 2, 1))
                kv = [cache_k, cache_v, page_table, ck_past, ck_new,
                      k_new.reshape(n_seq, t, d), v_new.reshape(n_seq, t, d)]
        gn = w['g_norm'][l]
        x = _ffn(x, mod4(l, 0), gn[0], w['w_ffn_in'], w['w_ffn_out'], l, 0, tm_ffn, tiles(tm_ffn))
        m4 = mod4(l, 1)
        if l < n_a:
            p = {key: w[key + '_a'][l] for key in ('mu', 'w_rkv', 'w0', 'w1', 'w2', 'a0', 'a1', 'a2', 'g1', 'g2',
                                                   'kk', 'ka')}
            if l > 0:
                p.update({key: w[key + '_a'][l - 1] for key in ('v0', 'v1', 'v2')})
            hl = shift0[l].reshape(n_seq, 1, d) if per_seq else jnp.repeat(shift0[l], t, axis=0)
            r, lw, k, v, kk, a, gt, hmod = _rwkv_proj(x, m4, gn[1], hl, t, tm_mix, p, v_first if l > 0 else None)
            if l == 0:
                v_first = v
            o, s_new = _wkv_long(r, lw, k, v, kk, a, wkv0[l], w['lnx_w_a'][l], w['lnx_b_a'][l], w['rk_a'][l], t, tb)
            wkv_new.append(s_new)
            shift_new.append(hmod.reshape(n_seq, t, d)[:, -1])
            x = _out_proj(o, gt, x, m4, gn[1], w['w_o_a'][l], tm_lin, tiles(tm_lin))
        else:
            j = l - n_a
            if past is None:
                qa = _q_aug(x, m4, gn[1], w['w_q_b'][j], kv[2], tm_lin, tiles(tm_lin))
                o = _flash_prompt(qa, kv[0], kv[1], n_seq, _largest_tile(t, 512), 4)
            else:
                q = _q_proj(x, m4, gn[1], w['w_q_b'][j], tm_lin, tiles(tm_lin))
                o = _paged_attention(q.reshape(n_seq, t, d), *kv, _largest_tile(kv[2].shape[1], 8)).reshape(n, d)
            x = _out_proj(o, None, x, m4, gn[1], w['w_o_b'][j], tm_lin, tiles(tm_lin))
        x = _ffn(x, mod4(l, 2), gn[2], w['w_ffn_in'], w['w_ffn_out'], l, 1, tm_ffn, tiles(tm_ffn))
    return (x.reshape(n_seq, t, d), jnp.stack(wkv_new), jnp.stack(shift_new), k_new.reshape(n_seq, t, h, HEAD_DIM),
            v_new.reshape(n_seq, t, h, HEAD_DIM), logf.reshape(n_seq, t, h))


def kernel(x_prompt, x_sample, state_wkv, state_shift, cache_k, cache_v, cache_logf, page_table, c_prompt, c_sample,
           w_mod, b_mod, g_norm, w_ffn_in, w_ffn_out, mu_a, w_rkv_a, w_o_a, w0_a, w1_a, w2_a, a0_a, a1_a, a2_a,
           v0_a, v1_a, v2_a, g1_a, g2_a, kk_a, ka_a, rk_a, lnx_w_a, lnx_b_a, g_kv, w_kv, w_f, b_f, w_q_b, w_o_b):
    bp, _, d = x_prompt.shape
    bs = x_sample.shape[0]
    depth = w_mod.shape[0]
    n_a = state_wkv.shape[0]
    n_pool, page = cache_k.shape[:2]
    cast = lambda a: a.astype(BF16)
    w = dict(g_norm=g_norm, w_ffn_in=w_ffn_in, w_ffn_out=w_ffn_out, mu_a=mu_a, w_rkv_a=cast(w_rkv_a),
             w_o_a=cast(w_o_a), w0_a=w0_a, w1_a=cast(w1_a), w2_a=cast(w2_a), a0_a=a0_a, a1_a=cast(a1_a),
             a2_a=cast(a2_a), v0_a=v0_a, v1_a=cast(v1_a), v2_a=cast(v2_a), g1_a=cast(g1_a), g2_a=cast(g2_a),
             kk_a=kk_a, ka_a=ka_a, rk_a=rk_a.reshape(n_a, d), lnx_w_a=lnx_w_a, lnx_b_a=lnx_b_a, g_kv=g_kv,
             w_kv=cast(w_kv), w_f=cast(w_f), b_f=b_f, w_q_b=cast(w_q_b), w_o_b=cast(w_o_b))

    mods = _mods(jnp.concatenate([c_prompt, c_sample], axis=0), w_mod, b_mod)
    mods = mods.reshape(depth, bp + bs, 3, 3, d)

    n_heads = d // HEAD_DIM
    zeros_wkv = jnp.zeros((n_a, bp, n_heads, HEAD_DIM, HEAD_DIM), F32)
    zeros_shift = jnp.zeros((n_a, bp, d), F32)
    y_p, wkv_p, shift_p, k_p, v_p, logf_p = _trunk(x_prompt, mods[:, :bp], zeros_wkv, zeros_shift, None, w)
    pool_t = lambda c: jnp.transpose(c, (0, 2, 3, 1)).reshape(n_pool, d, page)
    past = (pool_t(cache_k), pool_t(cache_v), jnp.transpose(cache_logf, (0, 2, 1)), page_table)
    y_s, wkv_s, shift_s, k_s, v_s, logf_s = _trunk(x_sample, mods[:, bp:], state_wkv, state_shift, past, w)
    return (y_p, y_s, wkv_p, shift_p, k_p, v_p, logf_p, wkv_s, shift_s, k_s, v_s, logf_s)
```

```python
import functools

import jax
import jax.numpy as jnp
import numpy as np
from jax import lax
from jax.experimental import pallas as pl
from jax.experimental.pallas import tpu as pltpu

F32 = jnp.float32
BF16 = jnp.bfloat16

NORM_EPS = 1e-6
GN_EPS = 64e-5
MACARON_W = 0.5
NEG_INF = -1e30
HEAD_DIM = 64
LANES = 128
PAIR = LANES // HEAD_DIM
VMEM_LIMIT = 56 * 1024 * 1024

LOG2E = 1.4426950408889634
N_BIAS = 3

_NT = (((1,), (1,)), ((), ()))
_TN = (((0,), (0,)), ((), ()))


def _cparams(*sem):
    return pltpu.CompilerParams(dimension_semantics=sem, vmem_limit_bytes=VMEM_LIMIT)


def _dot(a, b):
    return jnp.dot(a.astype(BF16), b.astype(BF16), preferred_element_type=F32)


def _dot_nt(a, b):
    return lax.dot_general(a.astype(BF16), b.astype(BF16), _NT, preferred_element_type=F32)


def _dot_tn(a, b):
    return lax.dot_general(a.astype(BF16), b.astype(BF16), _TN, preferred_element_type=F32)


def _split3(x):
    hi = x.astype(BF16)
    r1 = x - hi.astype(F32)
    mid = r1.astype(BF16)
    lo = (r1 - mid.astype(F32)).astype(BF16)
    return hi, mid, lo


def _dot3(sel, x):
    hi, mid, lo = _split3(x)
    out = jnp.dot(sel, hi, preferred_element_type=F32)
    out = out + jnp.dot(sel, mid, preferred_element_type=F32)
    return out + jnp.dot(sel, lo, preferred_element_type=F32)


def _dot3_rhs(x, sel):
    hi, mid, lo = _split3(x)
    out = jnp.dot(hi, sel, preferred_element_type=F32)
    out = out + jnp.dot(mid, sel, preferred_element_type=F32)
    return out + jnp.dot(lo, sel, preferred_element_type=F32)


def _rms(x, g):
    return x * lax.rsqrt(jnp.mean(x * x, axis=-1, keepdims=True) + NORM_EPS) * g


def _sigmoid(x):
    return 1.0 / (1.0 + jnp.exp(-x))


def _silu(x):
    return x * _sigmoid(x)


def _head_ones(n):
    r = lax.broadcasted_iota(jnp.int32, (n, n), 0) // HEAD_DIM
    c = lax.broadcasted_iota(jnp.int32, (n, n), 1) // HEAD_DIM
    return jnp.where(r == c, 1.0, 0.0).astype(BF16)


def _head_sum(x, ones):
    n = ones.shape[0]
    hi = x.astype(BF16)
    lo = (x - hi.astype(F32)).astype(BF16)
    cols = []
    for i in range(0, x.shape[1], n):
        s = jnp.dot(hi[:, i:i + n], ones, preferred_element_type=F32)
        cols.append(s + jnp.dot(lo[:, i:i + n], ones, preferred_element_type=F32))
    return cols[0] if len(cols) == 1 else jnp.concatenate(cols, axis=1)


def _mods_kernel(c_ref, w_ref, b_ref, o_ref):
    o_ref[...] = _dot(_silu(c_ref[...]), w_ref[...]) + b_ref[...]


def _mods(c_all, w_mod, b_mod):
    depth, d, n9 = w_mod.shape
    nb = c_all.shape[0]
    tn = d
    return pl.pallas_call(
        _mods_kernel,
        grid=(depth, n9 // tn),
        in_specs=[
            pl.BlockSpec((nb, d), lambda l, j: (0, 0)),
            pl.BlockSpec((None, d, tn), lambda l, j: (l, 0, j)),
            pl.BlockSpec((None, 1, tn), lambda l, j: (l, 0, j)),
        ],
        out_specs=pl.BlockSpec((None, nb, tn), lambda l, j: (l, 0, j)),
        out_shape=jax.ShapeDtypeStruct((depth, nb, n9), F32),
        compiler_params=_cparams("arbitrary", "arbitrary"),
        name="mods",
    )(c_all, w_mod, b_mod.reshape(depth, 1, n9))


def _mod_spec(m4, tm, tiles_per_seq):
    _, _, rows, d = m4.shape
    if rows == 1:
        return pl.BlockSpec((None, 3, 1, d), lambda i, *_: (i // tiles_per_seq, 0, 0, 0))
    return pl.BlockSpec((None, 3, tm, d), lambda i, *_: (0, 0, i, 0))


def _ffn_kernel(x_ref, m_ref, g_ref, wg_ref, wu_ref, wo_ref, o_ref, h_scr, acc_scr):
    j = pl.program_id(1)

    @pl.when(j == 0)
    def _():
        h = _rms(x_ref[...], g_ref[0:1]) * (1.0 + m_ref[1]) + m_ref[0]
        h_scr[...] = h.astype(BF16)
        acc_scr[...] = jnp.zeros_like(acc_scr)

    h = h_scr[...]
    gate = _dot(h, wg_ref[...])
    up = _dot(h, wu_ref[...])
    acc_scr[...] += _dot(_silu(gate) * up, wo_ref[...])

    @pl.when(j == pl.num_programs(1) - 1)
    def _():
        o_ref[...] = x_ref[...] + MACARON_W * m_ref[2] * _rms(acc_scr[...], g_ref[1:2])


def _ffn(x, m4, gn, w_in, w_out, layer, sub, tm, tiles_per_seq):
    n, d = x.shape
    f = w_out.shape[2]
    tf = 256 if f % 256 == 0 else LANES
    nf = f // tf
    return pl.pallas_call(
        _ffn_kernel,
        grid=(n // tm, nf),
        in_specs=[
            pl.BlockSpec((tm, d), lambda i, j: (i, 0)),
            _mod_spec(m4, tm, tiles_per_seq),
            pl.BlockSpec((2, d), lambda i, j: (0, 0)),
            pl.BlockSpec((None, None, d, tf), lambda i, j: (layer, sub, 0, j)),
            pl.BlockSpec((None, None, d, tf), lambda i, j: (layer, sub, 0, j + nf)),
            pl.BlockSpec((None, None, tf, d), lambda i, j: (layer, sub, j, 0)),
        ],
        out_specs=pl.BlockSpec((tm, d), lambda i, j: (i, 0)),
        out_shape=jax.ShapeDtypeStruct((n, d), F32),
        scratch_shapes=[pltpu.VMEM((tm, d), BF16), pltpu.VMEM((tm, d), F32)],
        compiler_params=_cparams("parallel", "arbitrary"),
        name="ffn",
    )(x, m4, gn, w_in, w_in, w_out)


def _rwkv_proj_kernel(has_vfirst, seq_len, x_ref, m_ref, g_ref, hl_ref, mu_ref, wrkv_ref, w0_ref, w1_ref,
                      w2_ref, a0_ref, a1_ref, a2_ref, g1_ref, g2_ref, kka_ref, kaa_ref, *rest):
    if has_vfirst:
        v0_ref, v1_ref, v2_ref, vf_ref = rest[:4]
        rest = rest[4:]
    r_ref, lw_ref, k_ref, v_ref, kk_ref, a_ref, gt_ref, h_ref, carry_scr = rest
    tm, d = x_ref.shape
    i = pl.program_id(0)

    h = _rms(x_ref[...], g_ref[0:1]) * (1.0 + m_ref[1]) + m_ref[0]
    h_ref[...] = h
    row = lax.broadcasted_iota(jnp.int32, (tm, d), 0)
    rolled = pltpu.roll(h, 1, 0)
    if hl_ref.shape[0] == 1:
        tiles_per_seq = seq_len // tm
        first = jnp.where(i % tiles_per_seq == 0, hl_ref[...], carry_scr[...])
        h_prev = jnp.where(row == 0, first, rolled)
        carry_scr[...] = h[tm - 1:tm, :]
    else:
        h_prev = jnp.where(row % seq_len == 0, hl_ref[...], rolled)
    xx = h_prev - h
    xr, xw, xk, xv, xa, xg = [h + xx * mu_ref[j:j + 1] for j in range(6)]

    r = _dot(xr, wrkv_ref[0])
    k = _dot(xk, wrkv_ref[1])
    v = _dot(xv, wrkv_ref[2])
    z = w0_ref[...] + _dot(jnp.tanh(_dot(xw, w1_ref[...])), w2_ref[...])
    lw = -_sigmoid(z) * jnp.exp(F32(-0.5))
    a = _sigmoid(a0_ref[...] + _dot(_dot(xa, a1_ref[...]), a2_ref[...]))
    gt = _dot(_sigmoid(_dot(xg, g1_ref[...])), g2_ref[...])
    if has_vfirst:
        mix = _sigmoid(v0_ref[...] + _dot(_dot(xv, v1_ref[...]), v2_ref[...]))
        v = v + (vf_ref[...] - v) * mix
    kk = k * kka_ref[...]
    ones = _head_ones(min(d, 2 * LANES))
    kk = kk * lax.rsqrt(jnp.maximum(_head_sum(kk * kk, ones), 1e-24))
    k = k * (1.0 + (a - 1.0) * kaa_ref[...])

    r_ref[...] = r
    lw_ref[...] = lw
    k_ref[...] = k
    v_ref[...] = v
    kk_ref[...] = kk
    a_ref[...] = a
    gt_ref[...] = gt


def _rwkv_proj(x, m4, gn, hl, seq_len, tm, p, v_first):
    n, d = x.shape
    tiles_per_seq = max(seq_len // tm, 1)
    has_vfirst = v_first is not None
    row = lambda a: a.reshape(1, d)
    const = lambda a: pl.BlockSpec(a.shape, lambda i: (0,) * a.ndim)
    tok = pl.BlockSpec((tm, d), lambda i: (i, 0))
    if hl.ndim == 3:
        hl_spec = pl.BlockSpec((None, 1, d), lambda i: (i // tiles_per_seq, 0, 0))
    else:
        hl_spec = tok
    args = [x, m4, gn, hl, p['mu'], p['w_rkv'], row(p['w0']), p['w1'], p['w2'], row(p['a0']), p['a1'], p['a2'],
            p['g1'], p['g2'], row(p['kk']), row(p['ka'])]
    specs = [tok, _mod_spec(m4, tm, tiles_per_seq), const(gn), hl_spec] + [const(a) for a in args[4:]]
    if has_vfirst:
        extra = [row(p['v0']), p['v1'], p['v2']]
        args += extra + [v_first]
        specs += [const(a) for a in extra] + [tok]
    return pl.pallas_call(
        functools.partial(_rwkv_proj_kernel, has_vfirst, seq_len),
        grid=(n // tm,),
        in_specs=specs,
        out_specs=[tok] * 8,
        out_shape=[jax.ShapeDtypeStruct((n, d), F32)] * 8,
        scratch_shapes=[pltpu.VMEM((1, d), F32)],
        compiler_params=_cparams("arbitrary"),
        name="rwkv_proj",
    )(*args)


LONG_CHUNK = 64


def _long_masks():
    n2 = PAIR * LONG_CHUNK
    r = np.arange(n2)[:, None]
    c = np.arange(n2)[None, :]
    masks = [c < r, c <= r, c == r]
    b = 1
    while b < LONG_CHUNK:
        masks.append((r // (2 * b) == c // (2 * b)) & ((r // b) % 2 == 1) & ((c // b) % 2 == 0))
        b *= 2
    return jnp.asarray(np.stack(masks).astype(np.float32))


def _wkv_long_kernel(r_ref, lw_ref, k_ref, v_ref, kk_ref, a_ref, s0_ref, lnw_ref, lnb_ref, rk_ref, mask_ref,
                     o_ref, sout_ref, st_scr, u2_scr, o2_scr, wq_scr, bk_scr, lkk_scr, v2_scr, kt_scr, bt_scr,
                     lrb_scr, lvt_scr, x_scr, gc_scr, lab_scr, tinv_scr, ol_scr, ds_scr, ut_scr):
    c = LONG_CHUNK
    n2 = PAIR * c
    tb, d = r_ref.shape
    n_pairs = d // LANES
    t_idx = pl.program_id(1)

    @pl.when(t_idx == 0)
    def _():
        zero = jnp.zeros((HEAD_DIM, HEAD_DIM), F32)
        for p in range(n_pairs):
            top = jnp.concatenate([s0_ref[PAIR * p], zero], axis=1)
            bottom = jnp.concatenate([zero, s0_ref[PAIR * p + 1]], axis=1)
            st_scr[p] = jnp.concatenate([top, bottom], axis=0).T

    rows_in = min(tb, c)
    n_chunks = max(tb // c, 1)
    items = [(ic, p) for ic in range(n_chunks) for p in range(n_pairs)]
    dotf = lambda x, y: jnp.dot(x, y, preferred_element_type=F32)
    lane = lax.broadcasted_iota(jnp.int32, (c, LANES), 1)
    head_lane = [lane // HEAD_DIM == h for h in range(PAIR)]
    tri = jnp.where(lax.broadcasted_iota(jnp.int32, (c, c), 1) <= lax.broadcasted_iota(jnp.int32, (c, c), 0),
                    1.0, 0.0).astype(BF16)

    def stack(x):
        return jnp.concatenate([jnp.where(m, x, 0.0) for m in head_lane], axis=0)

    for i, (ic, p) in enumerate(items):
        rows, ls = slice(ic * c, ic * c + rows_in), slice(p * LANES, (p + 1) * LANES)
        rc, lwc, kc, vc, kkc, ac = [ref[rows, ls] for ref in (r_ref, lw_ref, k_ref, v_ref, kk_ref, a_ref)]
        if rows_in < c:
            pad = jnp.zeros((c - rows_in, LANES), F32)
            rc, lwc, kc, vc, kkc, ac = [jnp.concatenate([x, pad], axis=0) for x in (rc, lwc, kc, vc, kkc, ac)]
        cum = _dot3(tri, lwc)
        g = jnp.exp(cum)
        gi = jnp.exp(-cum)
        g_end = g[c - 1:c, :]
        bb = ac * kkc * gi
        kb = kc * gi
        wq_scr[i, :n2] = stack(-kkc * jnp.exp(cum - lwc)).astype(BF16)
        wq_scr[i, n2:] = stack(rc * g).astype(BF16)
        bk_scr[i, :n2] = stack(bb).astype(BF16)
        bk_scr[i, n2:] = stack(kb).astype(BF16)
        v2_scr[i] = stack(vc).astype(BF16)
        kt_scr[i] = stack(kb * g_end).astype(BF16)
        bt_scr[i] = stack(bb * g_end).T.astype(BF16)
        gc_scr[i] = jnp.broadcast_to(g_end, (LANES, LANES)).T

    for i in range(len(items)):
        scores = lax.dot_general(wq_scr[i], bk_scr[i], _NT, preferred_element_type=F32)
        l_ab = scores[:n2, :n2] * mask_ref[0]
        lab_scr[i] = l_ab
        tinv_scr[i] = mask_ref[2] + l_ab * mask_ref[3]
        lkk_scr[i, :n2] = (scores[:n2, n2:] * mask_ref[0]).astype(BF16)
        lkk_scr[i, n2:] = (scores[n2:, n2:] * mask_ref[1]).astype(BF16)
        lrb_scr[i] = (scores[n2:, :n2] * mask_ref[1]).astype(BF16)

    for i in range(len(items)):
        lv = dotf(lkk_scr[i], v2_scr[i])
        lvt_scr[i] = lv[:n2].astype(BF16)
        ol_scr[i] = lv[n2:]
        ds_scr[i] = lax.dot_general(kt_scr[i], v2_scr[i], _TN, preferred_element_type=F32)

    for level in range(4, mask_ref.shape[0]):
        for i in range(len(items)):
            x_scr[i] = dotf((lab_scr[i] * mask_ref[level]).astype(BF16), tinv_scr[i].astype(BF16)).astype(BF16)
        for i in range(len(items)):
            tinv = tinv_scr[i]
            tinv_scr[i] = tinv + dotf(tinv.astype(BF16), x_scr[i])

    for i in range(len(items)):
        tinv = tinv_scr[i].astype(BF16)
        ut_scr[i] = dotf(tinv, lvt_scr[i])
        wq_scr[i, :n2] = dotf(tinv, wq_scr[i, :n2]).astype(BF16)

    def advance(ic, carry):
        rows = pl.ds(pl.multiple_of(ic * c, c), rows_in)
        for p in range(n_pairs):
            i = ic * n_pairs + p
            ws = dotf(wq_scr[i], st_scr[p].astype(BF16))
            u2_scr[p] = (ws[:n2] + ut_scr[i]).astype(BF16)
            o2_scr[p] = ws[n2:] + ol_scr[i]
        for p in range(n_pairs):
            i = ic * n_pairs + p
            u2 = u2_scr[p]
            o2 = o2_scr[p] + dotf(lrb_scr[i], u2)
            st_scr[p] = gc_scr[i] * st_scr[p] + dotf(bt_scr[i], u2) + ds_scr[i]
            o = o2[0:c]
            for h in range(1, PAIR):
                o = o + o2[h * c:(h + 1) * c]
            o_ref[rows, p * LANES:(p + 1) * LANES] = o[:rows_in]
        return carry

    lax.fori_loop(0, n_chunks, advance, 0)

    ones = _head_ones(min(d, 2 * LANES))
    o = o_ref[...]
    cen = o - _head_sum(o, ones) * (1.0 / HEAD_DIM)
    var = _head_sum(cen * cen, ones) * (1.0 / HEAD_DIM)
    bonus = _head_sum(r_ref[...] * k_ref[...] * rk_ref[...], ones) * v_ref[...]
    o_ref[...] = cen * lax.rsqrt(var + GN_EPS) * lnw_ref[...] + lnb_ref[...] + bonus

    @pl.when(t_idx == pl.num_programs(1) - 1)
    def _():
        for p in range(n_pairs):
            s = st_scr[p].T
            sout_ref[PAIR * p] = s[:HEAD_DIM, :HEAD_DIM]
            sout_ref[PAIR * p + 1] = s[HEAD_DIM:, HEAD_DIM:]


def _wkv_long(r, lw, k, v, kk, a, s0, lnw, lnb, rk, seq_len, tb):
    assert PAIR == 2
    n, d = r.shape
    n_seq = n // seq_len
    n_pairs = d // LANES
    nt = seq_len // tb
    n_items = max(tb // LONG_CHUNK, 1) * n_pairs
    n2 = PAIR * LONG_CHUNK
    assert n2 == LANES
    masks = _long_masks()
    tok = pl.BlockSpec((tb, d), lambda b, t: (b * nt + t, 0))
    vec = pl.BlockSpec((1, d), lambda b, t: (0, 0))
    st = pl.BlockSpec((None, PAIR * n_pairs, HEAD_DIM, HEAD_DIM), lambda b, t: (b, 0, 0, 0))
    per = lambda rows, dt: pltpu.VMEM((n_items, rows, LANES), dt)
    return pl.pallas_call(
        _wkv_long_kernel,
        grid=(n_seq, nt),
        in_specs=[tok] * 6 + [st, vec, vec, vec, pl.BlockSpec(masks.shape, lambda b, t: (0, 0, 0))],
        out_specs=[tok, st],
        out_shape=[jax.ShapeDtypeStruct((n, d), F32), jax.ShapeDtypeStruct(s0.shape, F32)],
        scratch_shapes=[pltpu.VMEM((n_pairs, LANES, LANES), F32), pltpu.VMEM((n_pairs, n2, LANES), BF16),
                        pltpu.VMEM((n_pairs, n2, LANES), F32)]
        + [per(2 * n2, BF16)] * 3 + [per(n2, BF16)] * 6 + [per(n2, F32)] * 6,
        compiler_params=_cparams("parallel", "arbitrary"),
        name="wkv_long",
    )(r, lw, k, v, kk, a, s0, lnw.reshape(1, d), lnb.reshape(1, d), rk.reshape(1, d), masks)


def _out_kernel(has_gate, res_w, *refs):
    if has_gate:
        a_ref, gt_ref, x_ref, m_ref, g_ref, w_ref, o_ref = refs
        a = a_ref[...] * gt_ref[...]
    else:
        a_ref, x_ref, m_ref, g_ref, w_ref, o_ref = refs
        a = a_ref[...]
    y = _dot(a, w_ref[...])
    o_ref[...] = x_ref[...] + res_w * m_ref[2] * _rms(y, g_ref[1:2])


def _out_proj(a, gate, x, m4, gn, w, tm, tiles_per_seq, res_w=1.0):
    n, d = x.shape
    tok = pl.BlockSpec((tm, d), lambda i: (i, 0))
    args = [a] + ([gate] if gate is not None else []) + [x, m4, gn, w]
    specs = [tok] * (len(args) - 3) + [_mod_spec(m4, tm, tiles_per_seq), pl.BlockSpec((2, d), lambda i: (0, 0)),
                                       pl.BlockSpec(w.shape, lambda i: (0, 0))]
    return pl.pallas_call(
        functools.partial(_out_kernel, gate is not None, res_w),
        grid=(n // tm,),
        in_specs=specs,
        out_specs=tok,
        out_shape=jax.ShapeDtypeStruct((n, d), F32),
        compiler_params=_cparams("parallel"),
        name="out_proj",
    )(*args)


def _kv_kernel(x_ref, g_ref, wkv_ref, wf_ref, bf_ref, k_ref, v_ref, kb_ref, vb_ref, lf_ref):
    d = x_ref.shape[1]
    s = _rms(x_ref[...], g_ref[...])
    kv = _dot(s, wkv_ref[...])
    k = kv[:, :d]
    v = kv[:, d:]
    k_ref[...] = k
    v_ref[...] = v
    kb_ref[...] = k.astype(BF16)
    vb_ref[...] = v.astype(BF16)
    z = _dot(s, wf_ref[...]) + bf_ref[...]
    lf_ref[...] = jnp.minimum(z, 0.0) - jnp.log(1.0 + jnp.exp(-jnp.abs(z)))


def _shared_kv(x, g_kv, w_kv, w_f, b_f, tm):
    n, d = x.shape
    h = w_f.shape[1]
    tok = pl.BlockSpec((tm, d), lambda i: (i, 0))
    return pl.pallas_call(
        _kv_kernel,
        grid=(n // tm,),
        in_specs=[tok, pl.BlockSpec((1, d), lambda i: (0, 0)), pl.BlockSpec(w_kv.shape, lambda i: (0, 0)),
                  pl.BlockSpec(w_f.shape, lambda i: (0, 0)), pl.BlockSpec((1, h), lambda i: (0, 0))],
        out_specs=[tok, tok, tok, tok, pl.BlockSpec((tm, h), lambda i: (i, 0))],
        out_shape=[jax.ShapeDtypeStruct((n, d), F32)] * 2 + [jax.ShapeDtypeStruct((n, d), BF16)] * 2
        + [jax.ShapeDtypeStruct((n, h), F32)],
        compiler_params=_cparams("parallel"),
        name="shared_kv",
    )(x, g_kv.reshape(1, d), w_kv, w_f, b_f.reshape(1, h))


def _cumsum_kernel(n_grp, pt_ref, *refs):
    lf_refs = refs[:n_grp]
    ck_ref, parts_ref, carry_scr = refs[n_grp:]
    del pt_ref
    j = pl.program_id(1)
    page, h = lf_refs[0].shape

    @pl.when(j == 0)
    def _():
        carry_scr[...] = jnp.zeros_like(carry_scr)

    ri = lax.broadcasted_iota(jnp.int32, (page, page), 0)
    ci = lax.broadcasted_iota(jnp.int32, (page, page), 1)
    tri = jnp.where(ci <= ri, 1.0, 0.0).astype(BF16)
    local = [_dot3(tri, lf[...]) for lf in lf_refs]
    base = carry_scr[...]
    for i in range(n_grp):
        ck = base + local[i]
        ck_ref[i * page:(i + 1) * page, :] = ck
        base = ck[page - 1:page, :]

        src_h = lax.broadcasted_iota(jnp.int32, (h, LANES), 0)
        dst = lax.broadcasted_iota(jnp.int32, (h, LANES), 1)
        out = jnp.where(lax.broadcasted_iota(jnp.int32, (page, LANES), 1) // h == N_BIAS, 1.0, 0.0)
        for x, part in enumerate(_split3(ck * LOG2E)):
            out = out + jnp.dot(part, jnp.where(dst == x * h + src_h, 1.0, 0.0).astype(BF16),
                                preferred_element_type=F32)
        parts_ref[i * page:(i + 1) * page, :] = out.astype(BF16)
    carry_scr[...] = base


def _cumsum_pages(page_table, lf_pages):
    n_seq, n_pages = page_table.shape
    _, page, h = lf_pages.shape
    n_grp = _largest_tile(n_pages, 8)

    def page_spec(i):
        return pl.BlockSpec((None, page, h), lambda b, j, pt: (pt[b, j * n_grp + i], 0, 0))

    return pl.pallas_call(
        functools.partial(_cumsum_kernel, n_grp),
        grid_spec=pltpu.PrefetchScalarGridSpec(
            num_scalar_prefetch=1, grid=(n_seq, n_pages // n_grp),
            in_specs=[page_spec(i) for i in range(n_grp)],
            out_specs=[pl.BlockSpec((None, n_grp * page, h), lambda b, j, pt: (b, j, 0)),
                       pl.BlockSpec((None, n_grp * page, LANES), lambda b, j, pt: (b, j, 0))],
            scratch_shapes=[pltpu.VMEM((1, h), F32)]),
        out_shape=[jax.ShapeDtypeStruct((n_seq, n_pages * page, h), F32),
                   jax.ShapeDtypeStruct((n_seq, n_pages * page, LANES), BF16)],
        compiler_params=_cparams("parallel", "arbitrary"),
        name="cumsum",
    )(page_table, *([lf_pages] * n_grp))


def _cumsum_pool_kernel(n_grp, pt_ref, *refs):
    lf_refs = refs[:n_grp]
    new_ref, ck_ref, cknew_ref, carry_scr = refs[n_grp:]
    del pt_ref
    j = pl.program_id(1)
    h, page = lf_refs[0].shape

    @pl.when(j == 0)
    def _():
        carry_scr[...] = jnp.zeros_like(carry_scr)

    ri = lax.broadcasted_iota(jnp.int32, (page, page), 0)
    ci = lax.broadcasted_iota(jnp.int32, (page, page), 1)
    upper = jnp.where(ri <= ci, 1.0, 0.0).astype(BF16)
    local = [_dot3_rhs(lf[...], upper) for lf in lf_refs]
    base = carry_scr[...]
    for i in range(n_grp):
        ck = base + local[i]
        ck_ref[:, i * page:(i + 1) * page] = ck
        base = ck[:, page - 1:page]
    carry_scr[...] = base

    @pl.when(j == pl.num_programs(1) - 1)
    def _():
        t = new_ref.shape[1]
        cknew_ref[...] = base + _dot3_rhs(new_ref[...], upper[:t, :t])


def _cumsum_pool(page_table, lf_pool_t, lf_new_t):
    n_seq, n_pages = page_table.shape
    _, h, page = lf_pool_t.shape
    t = lf_new_t.shape[2]
    n_grp = _largest_tile(n_pages, 8)

    def page_spec(i):
        return pl.BlockSpec((None, h, page), lambda b, j, pt: (pt[b, j * n_grp + i], 0, 0))

    new_spec = pl.BlockSpec((None, h, t), lambda b, j, pt: (b, 0, 0))
    return pl.pallas_call(
        functools.partial(_cumsum_pool_kernel, n_grp),
        grid_spec=pltpu.PrefetchScalarGridSpec(
            num_scalar_prefetch=1, grid=(n_seq, n_pages // n_grp),
            in_specs=[page_spec(i) for i in range(n_grp)] + [new_spec],
            out_specs=[pl.BlockSpec((None, h, n_grp * page), lambda b, j, pt: (b, 0, j)), new_spec],
            scratch_shapes=[pltpu.VMEM((h, 1), F32)]),
        out_shape=[jax.ShapeDtypeStruct((n_seq, h, n_pages * page), F32), jax.ShapeDtypeStruct((n_seq, h, t), F32)],
        compiler_params=_cparams("parallel", "arbitrary"),
        name="cumsum_pool",
    )(page_table, *([lf_pool_t] * n_grp), lf_new_t)


def _q_kernel(x_ref, m_ref, g_ref, w_ref, q_ref):
    h = _rms(x_ref[...], g_ref[0:1]) * (1.0 + m_ref[1]) + m_ref[0]
    q_ref[...] = (_dot(h, w_ref[...]) * (HEAD_DIM ** -0.5)).astype(BF16)


def _q_proj(x, m4, gn, w, tm, tiles_per_seq):
    n, d = x.shape
    tok = pl.BlockSpec((tm, d), lambda i: (i, 0))
    return pl.pallas_call(
        _q_kernel,
        grid=(n // tm,),
        in_specs=[tok, _mod_spec(m4, tm, tiles_per_seq), pl.BlockSpec((2, d), lambda i: (0, 0)),
                  pl.BlockSpec(w.shape, lambda i: (0, 0))],
        out_specs=tok,
        out_shape=jax.ShapeDtypeStruct((n, d), BF16),
        compiler_params=_cparams("parallel"),
        name="q_proj",
    )(x, m4, gn, w)


def _pair_to_tiles():
    m = np.zeros((LANES, PAIR * LANES), np.float32)
    r = np.arange(LANES)
    m[r, (r // HEAD_DIM) * LANES + r % HEAD_DIM] = 1.0
    return jnp.asarray(m, BF16)


def _bias_place(h, for_keys):
    m = np.zeros((LANES, h * LANES), np.float32)
    for hh in range(h):
        for x in range(N_BIAS):
            if for_keys:
                m[N_BIAS * h + hh, hh * LANES + HEAD_DIM + x] = 1.0
                m[x * h + hh, hh * LANES + HEAD_DIM + N_BIAS + x] = -1.0
            else:
                m[x * h + hh, hh * LANES + HEAD_DIM + x] = 1.0
                m[N_BIAS * h + hh, hh * LANES + HEAD_DIM + N_BIAS + x] = 1.0
    return jnp.asarray(m, BF16)


def _to_head_tiles(x, place, extra):
    cols = []
    for p in range(x.shape[1] // LANES):
        sl = slice(p * PAIR * LANES, (p + 1) * PAIR * LANES)
        cols.append(jnp.dot(x[:, p * LANES:(p + 1) * LANES], place, preferred_element_type=F32) + extra[:, sl])
    return jnp.concatenate(cols, axis=1)


def _q_aug_kernel(x_ref, m_ref, g_ref, w_ref, parts_ref, place_ref, bias_ref, q_ref):
    h = _rms(x_ref[...], g_ref[0:1]) * (1.0 + m_ref[1]) + m_ref[0]
    q = (_dot(h, w_ref[...]) * (HEAD_DIM ** -0.5 * LOG2E)).astype(BF16)
    bias = jnp.dot(parts_ref[...], bias_ref[...], preferred_element_type=F32)
    q_ref[...] = _to_head_tiles(q, place_ref[...], bias).astype(BF16)


def _q_aug(x, m4, gn, w, parts, tm, tiles_per_seq):
    n, d = x.shape
    tok = pl.BlockSpec((tm, d), lambda i: (i, 0))
    place, bias = _pair_to_tiles(), _bias_place(d // HEAD_DIM, False)
    const = lambda a: pl.BlockSpec(a.shape, lambda i: (0, 0))
    return pl.pallas_call(
        _q_aug_kernel,
        grid=(n // tm,),
        in_specs=[tok, _mod_spec(m4, tm, tiles_per_seq), pl.BlockSpec((2, d), lambda i: (0, 0)), const(w),
                  pl.BlockSpec((tm, LANES), lambda i: (i, 0)), const(place), const(bias)],
        out_specs=pl.BlockSpec((tm, PAIR * d), lambda i: (i, 0)),
        out_shape=jax.ShapeDtypeStruct((n, PAIR * d), BF16),
        compiler_params=_cparams("parallel"),
        name="q_aug",
    )(x, m4, gn, w, parts, place, bias)


def _kv_aug_kernel(k_ref, v_ref, parts_ref, place_ref, bias_ref, ka_ref, va_ref):
    tm, d2 = ka_ref.shape
    bias = jnp.dot(parts_ref[...], bias_ref[...], preferred_element_type=F32)
    ka_ref[...] = _to_head_tiles(k_ref[...], place_ref[...], bias).astype(BF16)
    ones = jnp.where(lax.broadcasted_iota(jnp.int32, (1, d2), 1) % LANES >= HEAD_DIM, 1.0, 0.0)
    va_ref[...] = _to_head_tiles(v_ref[...], place_ref[...], ones).astype(BF16)


def _kv_aug(kb, vb, parts, tm):
    n, d = kb.shape
    tok = pl.BlockSpec((tm, d), lambda i: (i, 0))
    wide = pl.BlockSpec((tm, PAIR * d), lambda i: (i, 0))
    place, bias = _pair_to_tiles(), _bias_place(d // HEAD_DIM, True)
    const = lambda a: pl.BlockSpec(a.shape, lambda i: (0, 0))
    return pl.pallas_call(
        _kv_aug_kernel,
        grid=(n // tm,),
        in_specs=[tok, tok, pl.BlockSpec((tm, LANES), lambda i: (i, 0)), const(place), const(bias)],
        out_specs=[wide, wide],
        out_shape=[jax.ShapeDtypeStruct((n, PAIR * d), BF16)] * 2,
        compiler_params=_cparams("parallel"),
        name="kv_aug",
    )(kb, vb, parts, place, bias)


def _flash_kernel(qi_ref, kj_ref, q_ref, k_ref, v_ref, o_ref, m_scr, acc_scr):
    n = pl.program_id(2)
    qi = qi_ref[n]
    kj = kj_ref[n]
    tq = q_ref.shape[0]
    tk = k_ref.shape[0]
    n_heads = q_ref.shape[1] // LANES

    @pl.when(kj == 0)
    def _():
        m_scr[...] = jnp.full_like(m_scr, NEG_INF)
        acc_scr[...] = jnp.zeros_like(acc_scr)

    def update(masked):
        if masked:
            valid = (lax.broadcasted_iota(jnp.int32, (tq, tk), 1)
                     <= lax.broadcasted_iota(jnp.int32, (tq, tk), 0))
        for h in range(n_heads):
            sl = slice(h * LANES, (h + 1) * LANES)
            s = lax.dot_general(q_ref[:, sl], k_ref[:, sl], _NT, preferred_element_type=F32)
            if masked:
                s = jnp.where(valid, s, NEG_INF)
            cols = [s[:, c * LANES:(c + 1) * LANES] for c in range(tk // LANES)]
            m_cur = cols[0]
            for col in cols[1:]:
                m_cur = jnp.maximum(m_cur, col)
            m_old = m_scr[h]
            m_new = jnp.maximum(m_old, jnp.max(m_cur, axis=1, keepdims=True))
            p = jnp.concatenate([jnp.exp2(col - m_new) for col in cols], axis=1).astype(BF16)
            acc_scr[h] = jnp.exp2(m_old - m_new) * acc_scr[h] + jnp.dot(p, v_ref[:, sl],
                                                                        preferred_element_type=F32)
            m_scr[h] = m_new

    @pl.when(kj < qi)
    def _():
        update(False)

    @pl.when(kj == qi)
    def _():
        update(True)
        lane = lax.broadcasted_iota(jnp.int32, (tq, LANES), 1)
        outs = []
        for h in range(n_heads):
            acc = acc_scr[h]
            outs.append(acc / pltpu.roll(acc, HEAD_DIM, 1))
        for p in range(n_heads // PAIR):
            o_ref[:, p * LANES:(p + 1) * LANES] = jnp.where(
                lane < HEAD_DIM, outs[PAIR * p], pltpu.roll(outs[PAIR * p + 1], HEAD_DIM, 1))


def _flash_prompt(qa, ka, va, n_seq, tq, heads_per_step):
    n, d2 = qa.shape
    t = n // n_seq
    nq = t // tq
    width = heads_per_step * LANES
    qi = jnp.asarray([i for i in range(nq) for _ in range(i + 1)], jnp.int32)
    kj = jnp.asarray([j for i in range(nq) for j in range(i + 1)], jnp.int32)
    return pl.pallas_call(
        _flash_kernel,
        grid_spec=pltpu.PrefetchScalarGridSpec(
            num_scalar_prefetch=2,
            grid=(n_seq, d2 // width, int(qi.shape[0])),
            in_specs=[
                pl.BlockSpec((tq, width), lambda b, p, s, qi, kj: (b * nq + qi[s], p)),
                pl.BlockSpec((tq, width), lambda b, p, s, qi, kj: (b * nq + kj[s], p)),
                pl.BlockSpec((tq, width), lambda b, p, s, qi, kj: (b * nq + kj[s], p)),
            ],
            out_specs=pl.BlockSpec((tq, width // PAIR), lambda b, p, s, qi, kj: (b * nq + qi[s], p)),
            scratch_shapes=[pltpu.VMEM((heads_per_step, tq, LANES), F32),
                            pltpu.VMEM((heads_per_step, tq, LANES), F32)]),
        out_shape=jax.ShapeDtypeStruct((n, d2 // PAIR), F32),
        compiler_params=_cparams("parallel", "parallel", "arbitrary"),
        name="fox_prompt",
    )(qi, kj, qa, ka, va)


def _paged_kernel(n_grp, pt_ref, q_ref, cq_ref, ckp_ref, ckn_ref, kn_ref, vn_ref, *rest):
    t, d = q_ref.shape
    n_heads = d // HEAD_DIM
    rows = n_heads * t
    k_refs = rest[:n_grp]
    v_refs = rest[n_grp:2 * n_grp]
    o_ref, qbd_scr, m_scr, l_scr, acc_scr = rest[2 * n_grp:]
    page = k_refs[0].shape[1]
    del pt_ref
    g = pl.program_id(1)

    row_head = lax.broadcasted_iota(jnp.int32, (rows, d), 0) // t
    lane_head = lax.broadcasted_iota(jnp.int32, (rows, d), 1) // HEAD_DIM
    expand = jnp.where(lax.broadcasted_iota(jnp.int32, (rows, n_heads), 0) // t
                       == lax.broadcasted_iota(jnp.int32, (rows, n_heads), 1), 1.0, 0.0).astype(BF16)

    @pl.when(g == 0)
    def _():
        q = q_ref[...].astype(F32)
        q_rows = jnp.concatenate([q] * n_heads, axis=0)
        qbd_scr[...] = jnp.where(row_head == lane_head, q_rows, 0.0).astype(BF16)
        m_scr[...] = jnp.full_like(m_scr, NEG_INF)
        l_scr[...] = jnp.zeros_like(l_scr)
        acc_scr[...] = jnp.zeros_like(acc_scr)

    qbd = qbd_scr[...]
    cq = cq_ref[...]

    def attend(s, weighted_values):
        m_old = m_scr[...]
        m_new = jnp.maximum(m_old, jnp.max(s, axis=1, keepdims=True))
        alpha = jnp.exp(m_old - m_new)
        p = jnp.exp(s - m_new)
        l_scr[...] = alpha * l_scr[...] + jnp.sum(p, axis=1, keepdims=True)
        acc_scr[...] = alpha * acc_scr[...] + weighted_values(p)
        m_scr[...] = m_new

    kts = [kr[...].astype(BF16) for kr in k_refs]
    vts = [vr[...].astype(BF16) for vr in v_refs]

    def past_values(p):
        pv = _dot_nt(p[:, 0:page], vts[0])
        for i in range(1, n_grp):
            pv = pv + _dot_nt(p[:, i * page:(i + 1) * page], vts[i])
        return pv

    s = jnp.concatenate([_dot(qbd, kt) for kt in kts], axis=1)
    s = s + (cq - _dot3(expand, ckp_ref[...]))
    attend(s, past_values)

    @pl.when(g == pl.num_programs(1) - 1)
    def _():
        pad = jnp.zeros((page - t, d), F32)
        k_new = jnp.concatenate([kn_ref[...], pad], axis=0)
        v_new = jnp.concatenate([vn_ref[...], pad], axis=0)
        s_new = _dot_nt(qbd, k_new) + (cq - _dot3(expand, ckn_ref[...]))
        key = lax.broadcasted_iota(jnp.int32, (rows, page), 1)
        query = lax.broadcasted_iota(jnp.int32, (rows, page), 0) % t
        attend(jnp.where((key <= query) & (key < t), s_new, NEG_INF), lambda p: _dot(p, v_new))
        out = acc_scr[...] / l_scr[...]
        out = jnp.where(row_head == lane_head, out, 0.0).reshape(n_heads, t, d)
        o_ref[...] = jnp.sum(out, axis=0)


def _paged_attention(q, pool_kt, pool_vt, page_table, ckp_rows, ck_new, k_new, v_new, n_grp):
    n_seq, t, d = q.shape
    n_pages = page_table.shape[1]
    page = pool_kt.shape[2]
    h = d // HEAD_DIM
    rows = h * t
    assert n_pages % n_grp == 0
    cq_col = ck_new.reshape(n_seq, rows, 1)
    ckn_rows = jnp.pad(ck_new, ((0, 0), (0, 0), (0, page - t)))
    per_seq = lambda shape: pl.BlockSpec((None,) + shape, lambda b, g, pt: (b, 0, 0))

    def page_spec(i):
        return pl.BlockSpec((None, d, page), lambda b, g, pt: (pt[b, g * n_grp + i], 0, 0))

    in_specs = [per_seq((t, d)), per_seq((rows, 1)),
                pl.BlockSpec((None, h, n_grp * page), lambda b, g, pt: (b, 0, g)),
                per_seq((h, page)), per_seq((t, d)), per_seq((t, d))]
    in_specs += [page_spec(i) for i in range(n_grp)] * 2
    return pl.pallas_call(
        functools.partial(_paged_kernel, n_grp),
        grid_spec=pltpu.PrefetchScalarGridSpec(
            num_scalar_prefetch=1, grid=(n_seq, n_pages // n_grp), in_specs=in_specs,
            out_specs=per_seq((t, d)),
            scratch_shapes=[pltpu.VMEM((rows, d), BF16), pltpu.VMEM((rows, 1), F32), pltpu.VMEM((rows, 1), F32),
                            pltpu.VMEM((rows, d), F32)]),
        out_shape=jax.ShapeDtypeStruct((n_seq, t, d), F32),
        compiler_params=_cparams("parallel", "arbitrary"),
        name="fox_sample",
    )(page_table, q, cq_col, ckp_rows, ckn_rows, k_new, v_new, *([pool_kt] * n_grp), *([pool_vt] * n_grp))


def _largest_tile(n, cap):
    t = min(n, cap)
    while n % t:
        t //= 2
    return t


def _trunk(x3, mods, wkv0, shift0, past, w):
    n_seq, t, d = x3.shape
    n = n_seq * t
    depth = mods.shape[0]
    n_a = wkv0.shape[0]
    h = d // HEAD_DIM
    x = x3.reshape(n, d)
    per_seq = t >= 256
    if per_seq:
        tm_ffn, tm_mix, tm_lin = _largest_tile(t, 1024), _largest_tile(t, 256), _largest_tile(t, 512)
        tb = _largest_tile(t, 4 * LONG_CHUNK)
    else:
        tm_ffn = tm_mix = tm_lin = n
        tb = t

    def mod4(l, sub):
        m = mods[l, :, sub]
        if per_seq:
            return m.reshape(n_seq, 3, 1, d)
        return jnp.repeat(m, t, axis=0).transpose(1, 0, 2).reshape(1, 3, n, d)

    def tiles(tm):
        return max(t // tm, 1)

    v_first = None
    wkv_new, shift_new = [], []
    kv = None
    for l in range(depth):
        if l == n_a:
            k_new, v_new, kb, vb, logf = _shared_kv(x, w['g_kv'], w['w_kv'], w['w_f'], w['b_f'], tm_lin)
            if past is None:
                ident = jnp.arange(n // LANES, dtype=jnp.int32).reshape(n_seq, t // LANES)
                _, parts = _cumsum_pages(ident, logf.reshape(n // LANES, LANES, h))
                parts = parts.reshape(n, LANES)
                kv = list(_kv_aug(kb, vb, parts, tm_lin)) + [parts]
            else:
                cache_k, cache_v, cache_logf, page_table = past
                ck_past, ck_new = _cumsum_pool(page_table, cache_logf,
                                               logf.reshape(n_seq, t, h).transpose(0, 2, 1))
                kv = [cache_k, cache_v, page_table, ck_past, ck_new,
                      k_new.reshape(n_seq, t, d), v_new.reshape(n_seq, t, d)]
        gn = w['g_norm'][l]
        x = _ffn(x, mod4(l, 0), gn[0], w['w_ffn_in'], w['w_ffn_out'], l, 0, tm_ffn, tiles(tm_ffn))
        m4 = mod4(l, 1)
        if l < n_a:
            p = {key: w[key + '_a'][l] for key in ('mu', 'w_rkv', 'w0', 'w1', 'w2', 'a0', 'a1', 'a2', 'g1', 'g2',
                                                   'kk', 'ka')}
            if l > 0:
                p.update({key: w[key + '_a'][l - 1] for key in ('v0', 'v1', 'v2')})
            hl = shift0[l].reshape(n_seq, 1, d) if per_seq else jnp.repeat(shift0[l], t, axis=0)
            r, lw, k, v, kk, a, gt, hmod = _rwkv_proj(x, m4, gn[1], hl, t, tm_mix, p, v_first if l > 0 else None)
            if l == 0:
                v_first = v
            o, s_new = _wkv_long(r, lw, k, v, kk, a, wkv0[l], w['lnx_w_a'][l], w['lnx_b_a'][l], w['rk_a'][l], t, tb)
            wkv_new.append(s_new)
            shift_new.append(hmod.reshape(n_seq, t, d)[:, -1])
            x = _out_proj(o, gt, x, m4, gn[1], w['w_o_a'][l], tm_lin, tiles(tm_lin))
        else:
            j = l - n_a
            if past is None:
                qa = _q_aug(x, m4, gn[1], w['w_q_b'][j], kv[2], tm_lin, tiles(tm_lin))
                o = _flash_prompt(qa, kv[0], kv[1], n_seq, _largest_tile(t, 512), 8)
            else:
                q = _q_proj(x, m4, gn[1], w['w_q_b'][j], tm_lin, tiles(tm_lin))
                o = _paged_attention(q.reshape(n_seq, t, d), *kv, _largest_tile(kv[2].shape[1], 8)).reshape(n, d)
            x = _out_proj(o, None, x, m4, gn[1], w['w_o_b'][j], tm_lin, tiles(tm_lin))
        x = _ffn(x, mod4(l, 2), gn[2], w['w_ffn_in'], w['w_ffn_out'], l, 1, tm_ffn, tiles(tm_ffn))
    return (x.reshape(n_seq, t, d), jnp.stack(wkv_new), jnp.stack(shift_new), k_new.reshape(n_seq, t, h, HEAD_DIM),
            v_new.reshape(n_seq, t, h, HEAD_DIM), logf.reshape(n_seq, t, h))


def kernel(x_prompt, x_sample, state_wkv, state_shift, cache_k, cache_v, cache_logf, page_table, c_prompt, c_sample,
           w_mod, b_mod, g_norm, w_ffn_in, w_ffn_out, mu_a, w_rkv_a, w_o_a, w0_a, w1_a, w2_a, a0_a, a1_a, a2_a,
           v0_a, v1_a, v2_a, g1_a, g2_a, kk_a, ka_a, rk_a, lnx_w_a, lnx_b_a, g_kv, w_kv, w_f, b_f, w_q_b, w_o_b):
    bp, _, d = x_prompt.shape
    bs = x_sample.shape[0]
    depth = w_mod.shape[0]
    n_a = state_wkv.shape[0]
    n_pool, page = cache_k.shape[:2]
    cast = lambda a: a.astype(BF16)
    w = dict(g_norm=g_norm, w_ffn_in=w_ffn_in, w_ffn_out=w_ffn_out, mu_a=mu_a, w_rkv_a=cast(w_rkv_a),
             w_o_a=cast(w_o_a), w0_a=w0_a, w1_a=cast(w1_a), w2_a=cast(w2_a), a0_a=a0_a, a1_a=cast(a1_a),
             a2_a=cast(a2_a), v0_a=v0_a, v1_a=cast(v1_a), v2_a=cast(v2_a), g1_a=cast(g1_a), g2_a=cast(g2_a),
             kk_a=kk_a, ka_a=ka_a, rk_a=rk_a.reshape(n_a, d), lnx_w_a=lnx_w_a, lnx_b_a=lnx_b_a, g_kv=g_kv,
             w_kv=cast(w_kv), w_f=cast(w_f), b_f=b_f, w_q_b=cast(w_q_b), w_o_b=cast(w_o_b))

    mods = _mods(jnp.concatenate([c_prompt, c_sample], axis=0), w_mod, b_mod)
    mods = mods.reshape(depth, bp + bs, 3, 3, d)

    n_heads = d // HEAD_DIM
    zeros_wkv = jnp.zeros((n_a, bp, n_heads, HEAD_DIM, HEAD_DIM), F32)
    zeros_shift = jnp.zeros((n_a, bp, d), F32)
    y_p, wkv_p, shift_p, k_p, v_p, logf_p = _trunk(x_prompt, mods[:, :bp], zeros_wkv, zeros_shift, None, w)
    pool_t = lambda c: jnp.transpose(c, (0, 2, 3, 1)).reshape(n_pool, d, page)
    past = (pool_t(cache_k), pool_t(cache_v), jnp.transpose(cache_logf, (0, 2, 1)), page_table)
    y_s, wkv_s, shift_s, k_s, v_s, logf_s = _trunk(x_sample, mods[:, bp:], state_wkv, state_shift, past, w)
    return (y_p, y_s, wkv_p, shift_p, k_p, v_p, logf_p, wkv_s, shift_s, k_s, v_s, logf_s)
```
